```python
import jax
import jax.numpy as jnp
from jax import lax
import numpy as np

D_MODEL = 1024
BATCH = 4
SEQ = 4096
DEPTH = 1
DEC_BATCH = 128
DEC_SEQ = 1
PAST_LEN = 8192
PAGE_SIZE = 128

MIX_WIDTH = D_MODEL
RWKV_WIDTH = MIX_WIDTH // 2
RWKV_HEAD = 64
RWKV_HEADS = RWKV_WIDTH // RWKV_HEAD
W_LORA = 64
A_LORA = 64
G_LORA = 128
RWKV_COLS = 3 * RWKV_WIDTH + W_LORA + A_LORA + G_LORA
SWA_WIDTH = MIX_WIDTH - RWKV_WIDTH
SWA_HEAD = 64
SWA_HEADS = SWA_WIDTH // SWA_HEAD
SWA_KV_HEADS = 2
KV_WIDTH = SWA_KV_HEADS * SWA_HEAD
SWA_COLS = SWA_WIDTH + 2 * KV_WIDTH
IN_COLS = RWKV_COLS + SWA_COLS
WINDOW = 128
BLOCK = 128
ROPE_THETA = 10000.0
MEM_TOKENS = 256
X_HEADS = 4
X_HEAD = 128
X_WIDTH = X_HEADS * X_HEAD
N_GROUPS = 4
EXPERTS_PER_GROUP = 8
TOP_K = 2
EXPERT_HIDDEN = 128
NORM_EPS = 1e-5
GN_EPS = 64e-5
L2_EPS = 1e-12

kernel_name = 'hymba_rwkv7_swa_sink_hmoe_step'

F32 = jnp.float32


def rms_norm(x, gain):
    xf = x.astype(F32)
    y = xf * lax.rsqrt(jnp.mean(xf * xf, axis=-1, keepdims=True) + NORM_EPS)
    return (y * gain.astype(F32)).astype(x.dtype)


def rope(x, pos):
    dh = x.shape[-1]
    half = dh // 2
    inv_freq = ROPE_THETA ** (-jnp.arange(half, dtype=F32) * 2.0 / dh)
    ang = pos.astype(F32)[:, None] * inv_freq[None, :]
    cos = jnp.cos(ang)[None, :, None, :]
    sin = jnp.sin(ang)[None, :, None, :]
    xf = x.astype(F32)
    x1, x2 = xf[..., :half], xf[..., half:]
    return jnp.concatenate([x1 * cos - x2 * sin, x2 * cos + x1 * sin], axis=-1).astype(x.dtype)


def rwkv7_time_mix(p, shift_prev, wkv0, lp):
    B, T, _ = p.shape
    C = RWKV_WIDTH
    pf = p.astype(F32)
    prev = jnp.concatenate([shift_prev.astype(F32)[:, None, :], pf[:, :-1]], axis=1)
    xm = pf + (prev - pf) * lp['mu_shift'].astype(F32)
    r = xm[..., :C]
    k = xm[..., C:2 * C]
    v = xm[..., 2 * C:3 * C]
    o = 3 * C
    xw = xm[..., o:o + W_LORA]
    xa = xm[..., o + W_LORA:o + W_LORA + A_LORA]
    xg = xm[..., o + W_LORA + A_LORA:]
    w = -jax.nn.softplus(-(lp['w_decay0'].astype(F32) + jnp.tanh(xw) @ lp['w_decay_up'].astype(F32))) - 0.5
    decay = jnp.exp(-jnp.exp(w))
    a = jax.nn.sigmoid(lp['a0'].astype(F32) + xa @ lp['w_a_up'].astype(F32))
    g = jax.nn.sigmoid(xg) @ lp['w_g_up'].astype(F32)

    def heads(t):
        return t.reshape(B, T, RWKV_HEADS, RWKV_HEAD)

    kk = heads(k * lp['k_k'].astype(F32))
    kk = kk / jnp.maximum(jnp.sqrt(jnp.sum(kk * kk, axis=-1, keepdims=True)), L2_EPS)
    k = k * (1.0 + (a - 1.0) * lp['k_a'].astype(F32))
    rh, wh, kh, vh, ah = heads(r), heads(decay), heads(k), heads(v), heads(a)
    seq = (jnp.moveaxis(rh, 1, 0), jnp.moveaxis(wh, 1, 0), jnp.moveaxis(kh, 1, 0),
           jnp.moveaxis(vh, 1, 0), jnp.moveaxis(-kk, 1, 0), jnp.moveaxis(kk * ah, 1, 0))

    def step(S, inp):
        r_t, w_t, k_t, v_t, a_t, b_t = inp
        sa = jnp.einsum('bhij,bhj->bhi', S, a_t)
        S = S * w_t[:, :, None, :] + sa[..., None] * b_t[:, :, None, :] + v_t[..., None] * k_t[:, :, None, :]
        return S, jnp.einsum('bhij,bhj->bhi', S, r_t)

    S_final, y = lax.scan(step, wkv0.astype(F32), seq)
    y = jnp.moveaxis(y, 0, 1)
    mu = jnp.mean(y, axis=-1, keepdims=True)
    var = jnp.mean(jnp.square(y - mu), axis=-1, keepdims=True)
    yn = ((y - mu) * lax.rsqrt(var + GN_EPS)).reshape(B, T, C)
    yn = yn * lp['ln_x_w'].astype(F32) + lp['ln_x_b'].astype(F32)
    bonus = jnp.sum(rh * kh * lp['r_k'].astype(F32), axis=-1, keepdims=True) * vh
    out = (yn + bonus.reshape(B, T, C)) * g
    return out.astype(p.dtype), p[:, -1], S_final.astype(wkv0.dtype)


def sink_softmax(s, sink):
    hkv, gq = s.shape[-4], s.shape[-3]
    col = jnp.broadcast_to(sink.astype(F32).reshape(hkv, gq, 1, 1), s.shape[:-1] + (1,))
    return jax.nn.softmax(jnp.concatenate([s, col], axis=-1), axis=-1)[..., :-1]


def swa_banded(q, k, v, sink):
    B, T, Hq, Dh = q.shape
    Hkv = k.shape[2]
    G = Hq // Hkv
    NB = T // BLOCK
    qb = q.reshape(B, NB, BLOCK, Hkv, G, Dh).astype(F32)

    def band(t):
        tb = t.reshape(B, NB, BLOCK, Hkv, Dh)
        prev = jnp.concatenate([jnp.zeros_like(tb[:, :1]), tb[:, :-1]], axis=1)
        return jnp.concatenate([prev, tb], axis=2)

    kc, vc = band(k), band(v)
    s = jnp.einsum('bnqkgd,bnskd->bnkgqs', qb, kc.astype(F32)) * (Dh ** -0.5)
    i = jnp.arange(BLOCK)[:, None]
    j = jnp.arange(2 * BLOCK)[None, :]
    diff = i + BLOCK - j
    blk = jnp.arange(NB)[:, None, None]
    valid = (diff >= 0) & (diff < WINDOW) & ((blk > 0) | (j >= BLOCK))
    s = jnp.where(valid[None, :, None, None], s, -jnp.inf)
    pr = sink_softmax(s, sink)
    o = jnp.einsum('bnkgqs,bnskd->bnqkgd', pr.astype(v.dtype), vc)
    return o.reshape(B, T, Hq * Dh)


def swa_with_buffer(q, k, v, pos, k_past, v_past, sink):
    B, T, Hq, Dh = q.shape
    Hkv = k.shape[2]
    G = Hq // Hkv
    WB = k_past.shape[1]
    kc = jnp.concatenate([k_past.astype(k.dtype), k], axis=1)
    vc = jnp.concatenate([v_past.astype(v.dtype), v], axis=1)
    kpos = jnp.concatenate([pos[0] - WB + jnp.arange(WB, dtype=jnp.int32), pos])
    diff = pos[:, None] - kpos[None, :]
    valid = (diff >= 0) & (diff < WINDOW)
    qg = q.reshape(B, T, Hkv, G, Dh).astype(F32)
    s = jnp.einsum('btkgd,bskd->bkgts', qg, kc.astype(F32)) * (Dh ** -0.5)
    s = jnp.where(valid, s, -jnp.inf)
    pr = sink_softmax(s, sink)
    o = jnp.einsum('bkgts,bskd->btkgd', pr.astype(vc.dtype), vc)
    return o.reshape(B, T, Hq * Dh), kc[:, -WB:], vc[:, -WB:]


def hybrid_mixer(h, pos, shift_prev, wkv0, k_past, v_past, lp):
    B, T, _ = h.shape
    proj = h @ lp['w_in']
    p_rwkv = proj[..., :RWKV_COLS]
    p_swa = proj[..., RWKV_COLS:]
    y_rwkv, new_shift, new_wkv = rwkv7_time_mix(p_rwkv, shift_prev, wkv0, lp)
    q = rope(p_swa[..., :SWA_WIDTH].reshape(B, T, SWA_HEADS, SWA_HEAD), pos)
    k = rope(p_swa[..., SWA_WIDTH:SWA_WIDTH + KV_WIDTH].reshape(B, T, SWA_KV_HEADS, SWA_HEAD), pos)
    v = p_swa[..., SWA_WIDTH + KV_WIDTH:].reshape(B, T, SWA_KV_HEADS, SWA_HEAD)
    if k_past is None:
        y_swa = swa_banded(q, k, v, lp['attn_sink'])
        wb = min(WINDOW, T)
        new_k, new_v = k[:, T - wb:], v[:, T - wb:]
    else:
        y_swa, new_k, new_v = swa_with_buffer(q, k, v, pos, k_past, v_past, lp['attn_sink'])
    y = jnp.concatenate([y_rwkv, y_swa.astype(y_rwkv.dtype)], axis=-1) @ lp['w_out']
    return y, new_shift, new_wkv, new_k, new_v


def memory_kv(mem, lp):
    B, M, _ = mem.shape
    mn = rms_norm(mem, lp['norm_mem'])
    mk = (mn @ lp['w_ck']).reshape(B, M, X_HEADS, X_HEAD)
    mv = (mn @ lp['w_cv']).reshape(B, M, X_HEADS, X_HEAD)
    return mk, mv


def memory_cross_attention(h, mk, mv, lp):
    B, T, _ = h.shape
    q = (h @ lp['w_cq']).reshape(B, T, X_HEADS, X_HEAD)
    s = jnp.einsum('bthd,bshd->bhts', q.astype(F32), mk.astype(F32)) * (X_HEAD ** -0.5)
    pr = jax.nn.softmax(s, axis=-1)
    o = jnp.einsum('bhts,bshd->bthd', pr.astype(mv.dtype), mv).reshape(B, T, X_WIDTH)
    return o @ lp['w_co']


def hierarchical_moe(h, lp):
    B, T, D = h.shape
    hf = h.reshape(B * T, D)
    lg = (hf @ lp['w_group_router'] + lp['b_group_router']).astype(F32)
    pg = jax.nn.softmax(lg, axis=-1)
    gsel = jnp.argmax(lg, axis=-1)
    onehot_g = jax.nn.one_hot(gsel, N_GROUPS, dtype=F32)
    p_sel = jnp.sum(pg * onehot_g, axis=-1)
    le = (hf @ lp['w_expert_router'] + lp['b_expert_router']).astype(F32)
    le = le.reshape(-1, N_GROUPS, EXPERTS_PER_GROUP)
    le_sel = jnp.take_along_axis(le, gsel[:, None, None], axis=1)[:, 0]
    top_v, top_i = lax.top_k(le_sel, TOP_K)
    top_w = jax.nn.softmax(top_v, axis=-1)
    gate_e = jnp.sum(jax.nn.one_hot(top_i, EXPERTS_PER_GROUP, dtype=F32) * top_w[..., None], axis=1)
    gate = onehot_g[:, :, None] * (p_sel[:, None] * gate_e)[:, None, :]
    out = jnp.zeros((B * T, D), h.dtype)
    for gi in range(N_GROUPS):
        hg = jnp.einsum('nd,edf->nef', hf, lp['w_exp_gate'][gi])
        hu = jnp.einsum('nd,edf->nef', hf, lp['w_exp_up'][gi])
        act = jax.nn.silu(hg) * hu * gate[:, gi, :, None].astype(hf.dtype)
        out = out + jnp.einsum('nef,efd->nd', act, lp['w_exp_down'][gi])
    return out.reshape(B, T, D)


def decoder_layer(x, pos, shift_prev, wkv0, k_past, v_past, mem_k, mem_v, lp):
    y, new_shift, new_wkv, new_k, new_v = hybrid_mixer(rms_norm(x, lp['norm_mix']), pos, shift_prev, wkv0, k_past, v_past, lp)
    x = x + y
    x = x + memory_cross_attention(rms_norm(x, lp['norm_cross']), mem_k, mem_v, lp)
    x = x + hierarchical_moe(rms_norm(x, lp['norm_ffn']), lp)
    return x, new_shift, new_wkv, new_k, new_v


def setup_inputs(seed: int = 0) -> dict:
    key = jax.random.key(seed)
    ks = iter(jax.random.split(key, 64))
    L = DEPTH
    G, E, FH = N_GROUPS, EXPERTS_PER_GROUP, EXPERT_HIDDEN
    wb = min(WINDOW, PAST_LEN)

    def nrm(shape, scale=1.0):
        return jax.random.normal(next(ks), shape, F32) * scale

    def gain(shape):
        return 1.0 + 0.1 * jax.random.normal(next(ks), shape, F32)

    def unif(shape, lo, hi):
        return jax.random.uniform(next(ks), shape, F32, lo, hi)

    return {
        'x_prompt': nrm((BATCH, SEQ, D_MODEL)),
        'x_sample': nrm((DEC_BATCH, DEC_SEQ, D_MODEL)),
        'mem_prompt': nrm((BATCH, MEM_TOKENS, D_MODEL)),
        'state_rwkv_shift': nrm((L, DEC_BATCH, RWKV_COLS)),
        'state_rwkv_wkv': nrm((L, DEC_BATCH, RWKV_HEADS, RWKV_HEAD, RWKV_HEAD), 0.5),
        'cache_swa_k': nrm((L, DEC_BATCH, wb, SWA_KV_HEADS, SWA_HEAD)),
        'cache_swa_v': nrm((L, DEC_BATCH, wb, SWA_KV_HEADS, SWA_HEAD)),
        'cache_mem_k': nrm((L, DEC_BATCH, MEM_TOKENS, X_HEADS, X_HEAD)),
        'cache_mem_v': nrm((L, DEC_BATCH, MEM_TOKENS, X_HEADS, X_HEAD)),
        'norm_mix': gain((L, D_MODEL)),
        'w_in': nrm((L, D_MODEL, IN_COLS), D_MODEL ** -0.5),
        'mu_shift': unif((L, RWKV_COLS), 0.0, 1.0),
        'w_decay0': unif((L, RWKV_WIDTH), -6.0, -1.0),
        'w_decay_up': nrm((L, W_LORA, RWKV_WIDTH), 0.5 * W_LORA ** -0.5),
        'a0': nrm((L, RWKV_WIDTH), 0.1),
        'w_a_up': nrm((L, A_LORA, RWKV_WIDTH), A_LORA ** -0.5),
        'w_g_up': nrm((L, G_LORA, RWKV_WIDTH), G_LORA ** -0.5),
        'k_k': 0.85 + nrm((L, RWKV_WIDTH), 0.1),
        'k_a': gain((L, RWKV_WIDTH)),
        'r_k': nrm((L, RWKV_HEADS, RWKV_HEAD), 0.1),
        'ln_x_w': gain((L, RWKV_WIDTH)),
        'ln_x_b': nrm((L, RWKV_WIDTH), 0.01),
        'attn_sink': nrm((L, SWA_HEADS), 0.5),
        'w_out': nrm((L, MIX_WIDTH, D_MODEL), MIX_WIDTH ** -0.5),
        'norm_cross': gain((L, D_MODEL)),
        'norm_mem': gain((L, D_MODEL)),
        'w_cq': nrm((L, D_MODEL, X_WIDTH), D_MODEL ** -0.5),
        'w_ck': nrm((L, D_MODEL, X_WIDTH), D_MODEL ** -0.5),
        'w_cv': nrm((L, D_MODEL, X_WIDTH), D_MODEL ** -0.5),
        'w_co': nrm((L, X_WIDTH, D_MODEL), X_WIDTH ** -0.5),
        'norm_ffn': gain((L, D_MODEL)),
        'w_group_router': nrm((L, D_MODEL, G), D_MODEL ** -0.5),
        'b_group_router': nrm((L, G), 0.01),
        'w_expert_router': nrm((L, D_MODEL, G * E), D_MODEL ** -0.5),
        'b_expert_router': nrm((L, G * E), 0.01),
        'w_exp_gate': nrm((L, G, E, D_MODEL, FH), D_MODEL ** -0.5),
        'w_exp_up': nrm((L, G, E, D_MODEL, FH), D_MODEL ** -0.5),
        'w_exp_down': nrm((L, G, E, FH, D_MODEL), FH ** -0.5),
        'norm_final': gain((D_MODEL,)),
    }


def reference(x_prompt, x_sample, mem_prompt, state_rwkv_shift, state_rwkv_wkv, cache_swa_k, cache_swa_v,
              cache_mem_k, cache_mem_v, norm_mix, w_in, mu_shift, w_decay0, w_decay_up, a0, w_a_up, w_g_up,
              k_k, k_a, r_k, ln_x_w, ln_x_b, attn_sink, w_out, norm_cross, norm_mem, w_cq, w_ck, w_cv, w_co,
              norm_ffn, w_group_router, b_group_router, w_expert_router, b_expert_router, w_exp_gate,
              w_exp_up, w_exp_down, norm_final):
    bp = x_prompt.shape[0]
    pos_p = jnp.arange(x_prompt.shape[1], dtype=jnp.int32)
    pos_s = PAST_LEN + jnp.arange(x_sample.shape[1], dtype=jnp.int32)
    xp, xs = x_prompt, x_sample
    p_shift, p_wkv, p_k, p_v, p_mk, p_mv = [], [], [], [], [], []
    s_shift, s_wkv, s_k, s_v = [], [], [], []
    for l in range(DEPTH):
        lp = dict(norm_mix=norm_mix[l], w_in=w_in[l], mu_shift=mu_shift[l], w_decay0=w_decay0[l],
                  w_decay_up=w_decay_up[l], a0=a0[l], w_a_up=w_a_up[l], w_g_up=w_g_up[l], k_k=k_k[l],
                  k_a=k_a[l], r_k=r_k[l], ln_x_w=ln_x_w[l], ln_x_b=ln_x_b[l], attn_sink=attn_sink[l],
                  w_out=w_out[l], norm_cross=norm_cross[l], norm_mem=norm_mem[l], w_cq=w_cq[l], w_ck=w_ck[l],
                  w_cv=w_cv[l], w_co=w_co[l], norm_ffn=norm_ffn[l], w_group_router=w_group_router[l],
                  b_group_router=b_group_router[l], w_expert_router=w_expert_router[l],
                  b_expert_router=b_expert_router[l], w_exp_gate=w_exp_gate[l], w_exp_up=w_exp_up[l],
                  w_exp_down=w_exp_down[l])
        mk_p, mv_p = memory_kv(mem_prompt, lp)
        shift0 = jnp.zeros((bp, RWKV_COLS), xp.dtype)
        wkv_init = jnp.zeros((bp, RWKV_HEADS, RWKV_HEAD, RWKV_HEAD), state_rwkv_wkv.dtype)
        xp, sh, wk, kn, vn = decoder_layer(xp, pos_p, shift0, wkv_init, None, None, mk_p, mv_p, lp)
        p_shift.append(sh); p_wkv.append(wk); p_k.append(kn); p_v.append(vn); p_mk.append(mk_p); p_mv.append(mv_p)
        xs, sh, wk, kn, vn = decoder_layer(xs, pos_s, state_rwkv_shift[l], state_rwkv_wkv[l], cache_swa_k[l],
                                           cache_swa_v[l], cache_mem_k[l], cache_mem_v[l], lp)
        s_shift.append(sh); s_wkv.append(wk); s_k.append(kn); s_v.append(vn)
    y_prompt = rms_norm(xp, norm_final)
    y_sample = rms_norm(xs, norm_final)
    prompt_shift = jnp.stack(p_shift)
    prompt_wkv = jnp.stack(p_wkv)
    prompt_swa_k = jnp.stack(p_k)
    prompt_swa_v = jnp.stack(p_v)
    prompt_mem_k = jnp.stack(p_mk)
    prompt_mem_v = jnp.stack(p_mv)
    sample_shift = jnp.stack(s_shift)
    sample_wkv = jnp.stack(s_wkv)
    sample_swa_k = jnp.stack(s_k)
    sample_swa_v = jnp.stack(s_v)
    return (y_prompt, y_sample, prompt_shift, prompt_wkv, prompt_swa_k, prompt_swa_v, prompt_mem_k,
            prompt_mem_v, sample_shift, sample_wkv, sample_swa_k, sample_swa_v)
```

```python
import functools

import jax
import jax.numpy as jnp
from jax import lax
from jax.experimental import pallas as pl
from jax.experimental.pallas import tpu as pltpu

F32 = jnp.float32
BF16 = jnp.bfloat16

D_MODEL = 1024
RWKV_WIDTH = 512
HEAD = 64
N_HEADS = 8
RWKV_COLS = 1792
SWA_COLS = 768
KV_WIDTH = 128
WINDOW = 128
PAST_LEN = 8192
ROPE_THETA = 10000.0
MEM_TOKENS = 256
X_HEADS = 4
X_HEAD = 128
X_WIDTH = 512
N_GROUPS = 4
EXPERTS_PER_GROUP = 8
EXPERT_HIDDEN = 128
NORM_EPS = 1e-5
GN_EPS = 64e-5
L2_EPS = 1e-12
CHUNK = 64
NEG = -1e30
MIB = 1024 * 1024


def _params(semantics, vmem_mib):
    return pltpu.CompilerParams(dimension_semantics=semantics, vmem_limit_bytes=vmem_mib * MIB)


def _bf(x):
    return x.astype(BF16)


def _dot(a, b):
    return jnp.dot(a, b, preferred_element_type=F32)


def _dot_nt(a, b):
    return lax.dot_general(a, b, (((1,), (1,)), ((), ())), preferred_element_type=F32)


def _dot_tn(a, b):
    return lax.dot_general(a, b, (((0,), (0,)), ((), ())), preferred_element_type=F32)


def _split(x):
    hi = x.astype(BF16)
    lo = (x - hi.astype(F32)).astype(BF16)
    return hi, lo


def _dot3(a, b, dot=_dot):
    ah, al = _split(a)
    bh, bl = _split(b)
    return dot(ah, bh) + (dot(ah, bl) + dot(al, bh))


def _dot3_nt(a, b):
    return _dot3(a, b, _dot_nt)


def _dot_sel_rhs(a, sel):
    ah, al = _split(a)
    return _dot(ah, sel) + _dot(al, sel)


def _dot_sel_lhs(sel, b):
    bh, bl = _split(b)
    return _dot(sel, bh) + _dot(sel, bl)


def _rmsnorm(x, gain):
    return x * lax.rsqrt(jnp.mean(x * x, axis=-1, keepdims=True) + NORM_EPS) * gain


def _sigmoid(x):
    return 1.0 / (1.0 + jnp.exp(-x))


def _iota(shape, dim):
    return lax.broadcasted_iota(jnp.int32, shape, dim)


def _mask_bf(cond):
    return jnp.where(cond, 1.0, 0.0).astype(BF16)


def _block_ones(n, blk_shift):
    return _mask_bf((_iota((n, n), 0) >> blk_shift) == (_iota((n, n), 1) >> blk_shift))


def _softmax_sink(s, sink_col):
    m = jnp.maximum(jnp.max(s, axis=-1, keepdims=True), sink_col)
    e = jnp.exp(s - m)
    den = jnp.sum(e, axis=-1, keepdims=True) + jnp.exp(sink_col - m)
    return e / den


def _rope(x, cos, sin_signed):
    width = x.shape[1]
    reps = width // 128
    if reps > 1:
        cos = jnp.concatenate([cos] * reps, axis=1)
        sin_signed = jnp.concatenate([sin_signed] * reps, axis=1)
    first_half = (_iota(x.shape, 1) & (HEAD - 1)) < HEAD // 2
    rot = jnp.where(first_half, pltpu.roll(x, width - HEAD // 2, axis=1), pltpu.roll(x, HEAD // 2, axis=1))
    return x * cos + rot * sin_signed


def _rope_tables(pos, inv_freq):
    ang = pos * inv_freq
    cos, sin = jnp.cos(ang), jnp.sin(ang)
    first_half = (_iota(ang.shape, 1) & (HEAD - 1)) < HEAD // 2
    return cos, jnp.where(first_half, -sin, sin)


def _norm_matmul_kernel(x_ref, g_ref, w_ref, o1_ref, o2_ref, *, split):
    h = _rmsnorm(x_ref[...], g_ref[...])
    w = w_ref[...]
    p = _dot(_bf(h), w) if w.dtype == BF16 else _dot3(h, w)
    o1_ref[...] = p[:, :split]
    o2_ref[...] = p[:, split:]


def _norm_matmul(x, gain, w, split, tm):
    n, d = x.shape
    cols = w.shape[1]
    return pl.pallas_call(
        functools.partial(_norm_matmul_kernel, split=split),
        grid=(n // tm,),
        in_specs=[pl.BlockSpec((tm, d), lambda i: (i, 0)),
                  pl.BlockSpec((1, d), lambda i: (0, 0)),
                  pl.BlockSpec((d, cols), lambda i: (0, 0))],
        out_specs=[pl.BlockSpec((tm, split), lambda i: (i, 0)),
                   pl.BlockSpec((tm, cols - split), lambda i: (i, 0))],
        out_shape=[jax.ShapeDtypeStruct((n, split), F32), jax.ShapeDtypeStruct((n, cols - split), F32)],
        compiler_params=_params(("parallel",), 48),
        name="norm_matmul",
    )(x, gain, w)


def _rwkv_prep(p, prev, mu, w0, wd, a0, wa, wg, k_k, k_a, ones_bd):
    c = RWKV_WIDTH
    xm = p + (prev - p) * mu
    r, k, v = xm[:, :c], xm[:, c:2 * c], xm[:, 2 * c:3 * c]
    xw, xa, xg = xm[:, 3 * c:3 * c + 64], xm[:, 3 * c + 64:3 * c + 128], xm[:, 3 * c + 128:]
    z = -(w0 + _dot3(jnp.tanh(xw), wd))
    w = -(jnp.maximum(z, 0.0) + jnp.log(1.0 + jnp.exp(-jnp.abs(z)))) - 0.5
    logw = -jnp.exp(w)
    rate = _sigmoid(a0 + _dot3(xa, wa))
    gate = _dot3(_sigmoid(xg), wg)
    kk = k * k_k
    norm = jnp.sqrt(_dot_sel_rhs(kk * kk, ones_bd))
    kk = kk / jnp.maximum(norm, L2_EPS)
    k2 = k * (1.0 + (rate - 1.0) * k_a)
    return r, logw, k2, v, -kk, kk * rate, gate


def _rwkv_post(y, r, k2, v, gate, r_k, ln_w, ln_b, ones_bd):
    mean = _dot_sel_rhs(y, ones_bd) * (1.0 / HEAD)
    d = y - mean
    var = _dot_sel_rhs(d * d, ones_bd) * (1.0 / HEAD)
    yn = d * lax.rsqrt(var + GN_EPS) * ln_w + ln_b
    bonus = _dot_sel_rhs(r * k2 * r_k, ones_bd) * v
    return (yn + bonus) * gate


def _rwkv_prompt_kernel(p_ref, mu_ref, w0_ref, wd_ref, a0_ref, wa_ref, wg_ref, kk_ref, ka_ref, rk_ref, lnw_ref,
                        lnb_ref, y_ref, s_ref, carry_ref, st_ref, ar_ref, bk_ref, bkp_ref, v_ref, pc_ref, yh_ref,
                        *, tc):
    c = pl.program_id(1)

    @pl.when(c == 0)
    def _init():
        carry_ref[...] = jnp.zeros_like(carry_ref)
        st_ref[...] = jnp.zeros_like(st_ref)

    p = p_ref[0]
    first_row = _iota((tc, 1), 0) == 0
    prev = jnp.where(first_row, carry_ref[0:1, :], pltpu.roll(p, 1, axis=0))
    carry_ref[0:1, :] = p[tc - 1:tc, :]
    ones_bd = _block_ones(RWKV_WIDTH, 6)
    r, logw, k2, v, a, b, gate = _rwkv_prep(p, prev, mu_ref[...], w0_ref[...], wd_ref[...], a0_ref[...],
                                             wa_ref[...], wg_ref[...], kk_ref[...], ka_ref[...], ones_bd)
    ri, ci = _iota((tc, tc), 0), _iota((tc, tc), 1)
    same_chunk = (ri >> 6) == (ci >> 6)
    cum = _dot_sel_lhs(_mask_bf(same_chunk & (ci <= ri)), logw)
    tot = _dot_sel_lhs(_mask_bf(same_chunk), logw)
    e_inv = jnp.exp(-cum)
    e_end = jnp.exp(tot - cum)
    pieces = ((ar_ref, a * jnp.exp(cum - logw), r * jnp.exp(cum)),
              (bk_ref, b * e_inv, k2 * e_inv),
              (bkp_ref, b * e_end, k2 * e_end))
    pc = jnp.exp(tot)
    for h in range(N_HEADS):
        sl = slice(h * HEAD, (h + 1) * HEAD)
        for ref, x0, x1 in pieces:
            ref[h, 0] = _bf(x0[:, sl])
            ref[h, 1] = _bf(x1[:, sl])
        v_ref[h] = _bf(v[:, sl])
        pc_ref[h] = pc[:, sl]

    rr, cc = _iota((CHUNK, CHUNK), 0), _iota((CHUNK, CHUNK), 1)
    strict, incl, eye = cc < rr, cc <= rr, cc == rr

    def chunk(ci_, carry):
        start = pl.multiple_of(ci_ * CHUNK, CHUNK)
        rows = pl.ds(start, CHUNK)
        for h in range(N_HEADS):
            ar = jnp.concatenate([ar_ref[h, 0, rows, :], ar_ref[h, 1, rows, :]], axis=0)
            bk = jnp.concatenate([bk_ref[h, 0, rows, :], bk_ref[h, 1, rows, :]], axis=0)
            g = _dot_nt(ar, bk)
            s0 = st_ref[h]
            x0 = _dot_nt(ar, _bf(s0))
            vh = v_ref[h, rows, :]
            a_kv = jnp.concatenate([jnp.where(strict, g[:CHUNK, CHUNK:], 0.0),
                                    jnp.where(incl, g[CHUNK:, CHUNK:], 0.0)], axis=0)
            av = _dot(_bf(a_kv), vh)
            a_ab = jnp.where(strict, g[:CHUNK, :CHUNK], 0.0)
            t = jnp.where(eye, 1.0, a_ab)
            apow = a_ab
            for _ in range(5):
                apb = _bf(apow)
                apow = _dot(apb, apb)
                t = t + _dot(_bf(t), _bf(apow))
            u = _dot(_bf(t), _bf(x0[:CHUNK] + av[:CHUNK]))
            a_rb = jnp.where(incl, g[CHUNK:, :CHUNK], 0.0)
            yh_ref[h, rows, :] = x0[CHUNK:] + av[CHUNK:] + _dot(_bf(a_rb), _bf(u))
            uv = jnp.concatenate([_bf(u), vh], axis=0)
            bkp = jnp.concatenate([bkp_ref[h, 0, rows, :], bkp_ref[h, 1, rows, :]], axis=0)
            st_ref[h] = s0 * pc_ref[h, pl.ds(start, 1), :] + _dot_tn(uv, bkp)
        return carry

    lax.fori_loop(0, tc // CHUNK, chunk, 0)
    y = jnp.concatenate([yh_ref[h] for h in range(N_HEADS)], axis=1)
    y_ref[0] = _bf(_rwkv_post(y, r, k2, v, gate, rk_ref[...], lnw_ref[...], lnb_ref[...], ones_bd))
    s_ref[0] = st_ref[...]


def _rwkv_prompt(p, lp, tc=256):
    bsz, t, _ = p.shape
    row = lambda n: pl.BlockSpec((1, n), lambda b, c: (0, 0))
    mat = lambda m, n: pl.BlockSpec((m, n), lambda b, c: (0, 0))
    head_scr = lambda dt: pltpu.VMEM((N_HEADS, 2, tc, HEAD), dt)
    return pl.pallas_call(
        functools.partial(_rwkv_prompt_kernel, tc=tc),
        grid=(bsz, t // tc),
        in_specs=[pl.BlockSpec((1, tc, RWKV_COLS), lambda b, c: (b, c, 0)),
                  row(RWKV_COLS), row(512), mat(64, 512), row(512), mat(64, 512), mat(128, 512),
                  row(512), row(512), row(512), row(512), row(512)],
        out_specs=[pl.BlockSpec((1, tc, RWKV_WIDTH), lambda b, c: (b, c, 0)),
                   pl.BlockSpec((1, N_HEADS, HEAD, HEAD), lambda b, c: (b, 0, 0, 0))],
        out_shape=[jax.ShapeDtypeStruct((bsz, t, RWKV_WIDTH), BF16),
                   jax.ShapeDtypeStruct((bsz, N_HEADS, HEAD, HEAD), F32)],
        scratch_shapes=[pltpu.VMEM((8, RWKV_COLS), F32), pltpu.VMEM((N_HEADS, HEAD, HEAD), F32),
                        head_scr(BF16), head_scr(BF16), head_scr(BF16),
                        pltpu.VMEM((N_HEADS, tc, HEAD), BF16), pltpu.VMEM((N_HEADS, tc, HEAD), F32),
                        pltpu.VMEM((N_HEADS, tc, HEAD), F32)],
        compiler_params=_params(("arbitrary", "arbitrary"), 48),
        name="rwkv_prompt",
    )(p, lp["mu_shift"], lp["w_decay0"], lp["w_decay_up"], lp["a0"], lp["w_a_up"], lp["w_g_up"], lp["k_k"],
      lp["k_a"], lp["r_k"], lp["ln_x_w"], lp["ln_x_b"])


def _rwkv_sample_prep_kernel(p_ref, prev_ref, mu_ref, w0_ref, wd_ref, a0_ref, wa_ref, wg_ref, kk_ref, ka_ref,
                             r_ref, w_ref, k_ref, v_ref, a_ref, b_ref, g_ref):
    ones_bd = _block_ones(RWKV_WIDTH, 6)
    r, logw, k2, v, a, b, gate = _rwkv_prep(p_ref[...], prev_ref[...], mu_ref[...], w0_ref[...], wd_ref[...],
                                             a0_ref[...], wa_ref[...], wg_ref[...], kk_ref[...], ka_ref[...], ones_bd)
    r_ref[...] = r
    w_ref[...] = jnp.exp(logw)
    k_ref[...] = k2
    v_ref[...] = v
    a_ref[...] = a
    b_ref[...] = b
    g_ref[...] = gate


def _rwkv_sample_step_kernel(r_ref, w_ref, k_ref, v_ref, a_ref, b_ref, g_ref, s_ref, rk_ref, lnw_ref, lnb_ref,
                             y_ref, so_ref, yacc_ref, *, bt):
    eye = _iota((HEAD, HEAD), 0) == _iota((HEAD, HEAD), 1)
    for bi in range(bt):
        for h in range(N_HEADS):
            sl = slice(h * HEAD, (h + 1) * HEAD)
            row = lambda ref: ref[bi:bi + 1, sl]
            s = s_ref[bi, h]
            sa = jnp.sum(s * row(a_ref), axis=1, keepdims=True)
            v_col = jnp.sum(jnp.where(eye, row(v_ref), 0.0), axis=1, keepdims=True)
            s_new = s * row(w_ref) + sa * row(b_ref) + v_col * row(k_ref)
            so_ref[bi, h] = s_new
            y_col = jnp.sum(s_new * row(r_ref), axis=1, keepdims=True)
            yacc_ref[bi:bi + 1, sl] = jnp.sum(jnp.where(eye, y_col, 0.0), axis=0, keepdims=True)
    ones_bd = _block_ones(RWKV_WIDTH, 6)
    y_ref[...] = _rwkv_post(yacc_ref[...], r_ref[...], k_ref[...], v_ref[...], g_ref[...], rk_ref[...],
                            lnw_ref[...], lnb_ref[...], ones_bd)


def _rwkv_sample(p, prev, wkv, lp, bt=8):
    n = p.shape[0]
    full = lambda a: pl.BlockSpec(a.shape, lambda i: (0,) * a.ndim)
    names = ("mu_shift", "w_decay0", "w_decay_up", "a0", "w_a_up", "w_g_up", "k_k", "k_a")
    prep_in = [p, prev] + [lp[k] for k in names]
    vecs = pl.pallas_call(
        _rwkv_sample_prep_kernel,
        grid=(1,),
        in_specs=[full(a) for a in prep_in],
        out_specs=[pl.BlockSpec((n, RWKV_WIDTH), lambda i: (0, 0))] * 7,
        out_shape=[jax.ShapeDtypeStruct((n, RWKV_WIDTH), F32)] * 7,
        compiler_params=_params(("arbitrary",), 32),
        name="rwkv_sample_prep",
    )(*prep_in)
    vec_spec = pl.BlockSpec((bt, RWKV_WIDTH), lambda i: (i, 0))
    st_spec = pl.BlockSpec((bt, N_HEADS, HEAD, HEAD), lambda i: (i, 0, 0, 0))
    row = pl.BlockSpec((1, RWKV_WIDTH), lambda i: (0, 0))
    return pl.pallas_call(
        functools.partial(_rwkv_sample_step_kernel, bt=bt),
        grid=(n // bt,),
        in_specs=[vec_spec] * 7 + [st_spec, row, row, row],
        out_specs=[vec_spec, st_spec],
        out_shape=[jax.ShapeDtypeStruct((n, RWKV_WIDTH), F32), jax.ShapeDtypeStruct(wkv.shape, F32)],
        scratch_shapes=[pltpu.VMEM((bt, RWKV_WIDTH), F32)],
        compiler_params=_params(("parallel",), 32),
        name="rwkv_sample_step",
    )(*vecs, wkv, lp["r_k"], lp["ln_x_w"], lp["ln_x_b"])


def _swa_prompt_kernel(p_ref, invf_ref, sink_ref, y_ref, ko_ref, vo_ref, kprev_ref, vprev_ref):
    n = pl.program_id(1)
    blk = WINDOW

    @pl.when(n == 0)
    def _init():
        kprev_ref[...] = jnp.zeros_like(kprev_ref)
        vprev_ref[...] = jnp.zeros_like(vprev_ref)

    p = p_ref[0]
    pos = (n * blk + _iota((blk, 1), 0)).astype(F32)
    cos, sin_signed = _rope_tables(pos, invf_ref[...])
    q = _rope(p[:, :RWKV_WIDTH], cos, sin_signed)
    k = _rope(p[:, RWKV_WIDTH:RWKV_WIDTH + KV_WIDTH], cos, sin_signed)
    v = p[:, RWKV_WIDTH + KV_WIDTH:]
    kcat = jnp.concatenate([kprev_ref[...], k], axis=0)
    vcat = jnp.concatenate([vprev_ref[...], v], axis=0)
    ii, jj = _iota((blk, 2 * blk), 0), _iota((blk, 2 * blk), 1)
    lo = jnp.where(n == 0, blk, 0)
    valid = (jj > ii) & (jj <= ii + blk) & (jj >= lo)
    bias = jnp.where(valid, 0.0, NEG)
    bias4 = jnp.concatenate([bias] * 4, axis=0)
    outs = []
    for g in range(2):
        heads = range(4 * g, 4 * g + 4)
        qg = jnp.concatenate([q[:, h * HEAD:(h + 1) * HEAD] for h in heads], axis=0)
        s = _dot_nt(_bf(qg), _bf(kcat[:, g * HEAD:(g + 1) * HEAD])) * (HEAD ** -0.5) + bias4
        sink_col = jnp.concatenate([jnp.broadcast_to(sink_ref[0:1, h:h + 1], (blk, 1)) for h in heads], axis=0)
        o = _dot(_bf(_softmax_sink(s, sink_col)), _bf(vcat[:, g * HEAD:(g + 1) * HEAD]))
        outs += [o[i * blk:(i + 1) * blk] for i in range(4)]
    y_ref[0] = _bf(jnp.concatenate(outs, axis=1))
    ko_ref[0] = k
    vo_ref[0] = v
    kprev_ref[...] = k
    vprev_ref[...] = v


def _swa_prompt(p, inv_freq, sink):
    bsz, t, _ = p.shape
    blk = WINDOW
    return pl.pallas_call(
        _swa_prompt_kernel,
        grid=(bsz, t // blk),
        in_specs=[pl.BlockSpec((1, blk, SWA_COLS), lambda b, n: (b, n, 0)),
                  pl.BlockSpec((1, 128), lambda b, n: (0, 0)),
                  pl.BlockSpec((1, N_HEADS), lambda b, n: (0, 0))],
        out_specs=[pl.BlockSpec((1, blk, RWKV_WIDTH), lambda b, n: (b, n, 0)),
                   pl.BlockSpec((1, blk, KV_WIDTH), lambda b, n: (b, 0, 0)),
                   pl.BlockSpec((1, blk, KV_WIDTH), lambda b, n: (b, 0, 0))],
        out_shape=[jax.ShapeDtypeStruct((bsz, t, RWKV_WIDTH), BF16),
                   jax.ShapeDtypeStruct((bsz, blk, KV_WIDTH), F32),
                   jax.ShapeDtypeStruct((bsz, blk, KV_WIDTH), F32)],
        scratch_shapes=[pltpu.VMEM((blk, KV_WIDTH), F32), pltpu.VMEM((blk, KV_WIDTH), F32)],
        compiler_params=_params(("arbitrary", "arbitrary"), 32),
        name="swa_prompt",
    )(p, inv_freq, sink)


def _swa_sample_kernel(p_ref, ck_ref, cv_ref, invf_ref, sinkc_ref, y_ref, nk_ref, nv_ref, *, bt):
    p = p_ref[...]
    cos, sin_signed = _rope_tables(jnp.full((1, 1), float(PAST_LEN), F32), invf_ref[...])
    q = _rope(p[:, :RWKV_WIDTH], cos, sin_signed) * (HEAD ** -0.5)
    k = _rope(p[:, RWKV_WIDTH:RWKV_WIDTH + KV_WIDTH], cos, sin_signed)
    v = p[:, RWKV_WIDTH + KV_WIDTH:]
    last = _iota((WINDOW, 1), 0) == WINDOW - 1
    head_mask = _iota((N_HEADS, RWKV_WIDTH), 0) == (_iota((N_HEADS, RWKV_WIDTH), 1) >> 6)
    tile_kv = lambda x: jnp.concatenate([x[:, :HEAD]] * 4 + [x[:, HEAD:]] * 4, axis=1)
    for bi in range(bt):
        k_new = jnp.where(last, k[bi:bi + 1, :], pltpu.roll(ck_ref[bi], WINDOW - 1, axis=0))
        v_new = jnp.where(last, v[bi:bi + 1, :], pltpu.roll(cv_ref[bi], WINDOW - 1, axis=0))
        nk_ref[bi] = k_new
        nv_ref[bi] = v_new
        q_bd = jnp.where(head_mask, q[bi:bi + 1, :], 0.0)
        s = _dot3_nt(q_bd, tile_kv(k_new))
        o = _dot3(_softmax_sink(s, sinkc_ref[...]), v_new)
        y_ref[bi:bi + 1, :] = jnp.sum(jnp.where(head_mask, tile_kv(o), 0.0), axis=0, keepdims=True)


def _swa_sample(p, ck, cv, inv_freq, sink_col, bt=8):
    n = p.shape[0]
    cache_spec = pl.BlockSpec((bt, WINDOW, KV_WIDTH), lambda i: (i, 0, 0))
    return pl.pallas_call(
        functools.partial(_swa_sample_kernel, bt=bt),
        grid=(n // bt,),
        in_specs=[pl.BlockSpec((bt, SWA_COLS), lambda i: (i, 0)), cache_spec, cache_spec,
                  pl.BlockSpec((1, 128), lambda i: (0, 0)), pl.BlockSpec((N_HEADS, 1), lambda i: (0, 0))],
        out_specs=[pl.BlockSpec((bt, RWKV_WIDTH), lambda i: (i, 0)), cache_spec, cache_spec],
        out_shape=[jax.ShapeDtypeStruct((n, RWKV_WIDTH), F32), jax.ShapeDtypeStruct(ck.shape, F32),
                   jax.ShapeDtypeStruct(cv.shape, F32)],
        compiler_params=_params(("parallel",), 32),
        name="swa_sample",
    )(p, ck, cv, inv_freq, sink_col)


def _mix_out(yr, ys, x, wo1, wo2):
    return x + _dot(yr, wo1) + _dot(ys, wo2)


def _out_cross_kernel(yr_ref, ys_ref, x_ref, wo1_ref, wo2_ref, gc_ref, wq_ref, mk_ref, mv_ref, wco_ref, o_ref):
    x1 = _mix_out(yr_ref[0], ys_ref[0], x_ref[0], wo1_ref[...], wo2_ref[...])
    q = _bf(_dot(_bf(_rmsnorm(x1, gc_ref[...])), wq_ref[...]))
    mk, mv = mk_ref[0], mv_ref[0]
    outs = []
    for h in range(X_HEADS):
        sl = slice(h * X_HEAD, (h + 1) * X_HEAD)
        s = _dot_nt(q[:, sl], mk[:, sl]) * (X_HEAD ** -0.5)
        e = jnp.exp(s - jnp.max(s, axis=-1, keepdims=True))
        pr = e / jnp.sum(e, axis=-1, keepdims=True)
        outs.append(_dot(_bf(pr), mv[:, sl]))
    o_ref[0] = x1 + _dot(_bf(jnp.concatenate(outs, axis=1)), wco_ref[...])


def _out_cross(yr, ys, x, wo1, wo2, gain, wq, mk, mv, wco, tm=512):
    bsz, t, d = x.shape
    tile = lambda w: pl.BlockSpec((1, tm, w), lambda b, i: (b, i, 0))
    const = lambda a: pl.BlockSpec(a.shape, lambda b, i: (0,) * a.ndim)
    mem = pl.BlockSpec((1, MEM_TOKENS, X_WIDTH), lambda b, i: (b, 0, 0))
    return pl.pallas_call(
        _out_cross_kernel,
        grid=(bsz, t // tm),
        in_specs=[tile(RWKV_WIDTH), tile(RWKV_WIDTH), tile(d), const(wo1), const(wo2), const(gain), const(wq),
                  mem, mem, const(wco)],
        out_specs=tile(d),
        out_shape=jax.ShapeDtypeStruct((bsz, t, d), F32),
        compiler_params=_params(("parallel", "parallel"), 48),
        name="out_cross",
    )(yr, ys, x, wo1, wo2, gain, wq, mk, mv, wco)


def _out_q_kernel(yr_ref, ys_ref, x_ref, wo1_ref, wo2_ref, gc_ref, wq_ref, x1_ref, q_ref):
    x1 = x_ref[...] + _dot3(yr_ref[...], wo1_ref[...]) + _dot3(ys_ref[...], wo2_ref[...])
    x1_ref[...] = x1
    q_ref[...] = _dot3(_rmsnorm(x1, gc_ref[...]), wq_ref[...])


def _cross_sample_kernel(q_ref, mk_ref, mv_ref, o_ref, *, bt):
    head_mask = _iota((8, X_WIDTH), 0) == (_iota((8, X_WIDTH), 1) >> 7)
    for bi in range(bt):
        q_bd = jnp.where(head_mask, q_ref[bi:bi + 1, :], 0.0)
        s = _dot3_nt(q_bd, mk_ref[bi]) * (X_HEAD ** -0.5)
        e = jnp.exp(s - jnp.max(s, axis=-1, keepdims=True))
        pr = e / jnp.sum(e, axis=-1, keepdims=True)
        o = _dot3(pr, mv_ref[bi])
        o_ref[bi:bi + 1, :] = jnp.sum(jnp.where(head_mask, o, 0.0), axis=0, keepdims=True)


def _lin_res_kernel(x_ref, a_ref, w_ref, o_ref):
    o_ref[...] = x_ref[...] + _dot3(a_ref[...], w_ref[...])


def _single_step(kernel_fn, out_shape, name, *args):
    full = lambda a: pl.BlockSpec(a.shape, lambda i: (0,) * len(a.shape))
    outs = out_shape if isinstance(out_shape, (list, tuple)) else [out_shape]
    out_specs = [full(o) for o in outs]
    return pl.pallas_call(
        kernel_fn, grid=(1,), in_specs=[full(a) for a in args],
        out_specs=out_specs if isinstance(out_shape, (list, tuple)) else out_specs[0],
        out_shape=out_shape, compiler_params=_params(("arbitrary",), 32), name=name,
    )(*args)


def _out_cross_sample(yr, ys, x, wo1, wo2, gain, wq, mk, mv, wco, bt=8):
    n, d = x.shape
    x1, q = _single_step(_out_q_kernel, [jax.ShapeDtypeStruct((n, d), F32), jax.ShapeDtypeStruct((n, X_WIDTH), F32)],
                         "out_q_sample", yr, ys, x, wo1, wo2, gain, wq)
    mem = pl.BlockSpec((bt, MEM_TOKENS, X_WIDTH), lambda i: (i, 0, 0))
    vec = pl.BlockSpec((bt, X_WIDTH), lambda i: (i, 0))
    o = pl.pallas_call(
        functools.partial(_cross_sample_kernel, bt=bt),
        grid=(n // bt,),
        in_specs=[vec, mem, mem],
        out_specs=vec,
        out_shape=jax.ShapeDtypeStruct((n, X_WIDTH), F32),
        compiler_params=_params(("parallel",), 40),
        name="cross_sample",
    )(q, mk, mv)
    return _single_step(_lin_res_kernel, jax.ShapeDtypeStruct((n, d), F32), "cross_out_sample", x1, o, wco)


def _route(logits):
    lane_i = _iota(logits.shape, 1)
    lane = lane_i.astype(F32)
    big = 1024.0
    n_exp = float(N_GROUPS * EXPERTS_PER_GROUP)
    lg = jnp.where((lane >= n_exp) & (lane < n_exp + N_GROUPS), logits, NEG)
    g_max = jnp.max(lg, axis=-1, keepdims=True)
    g_idx = jnp.min(jnp.where(lg == g_max, lane, big), axis=-1, keepdims=True) - n_exp
    p_sel = 1.0 / jnp.sum(jnp.exp(lg - g_max), axis=-1, keepdims=True)
    le = jnp.where((lane_i >> 3).astype(F32) == g_idx, logits, NEG)
    t1 = jnp.max(le, axis=-1, keepdims=True)
    i1 = jnp.min(jnp.where(le == t1, lane, big), axis=-1, keepdims=True)
    le2 = jnp.where(lane == i1, NEG, le)
    t2 = jnp.max(le2, axis=-1, keepdims=True)
    i2 = jnp.min(jnp.where(le2 == t2, lane, big), axis=-1, keepdims=True)
    e2 = jnp.exp(t2 - t1)
    w1 = 1.0 / (1.0 + e2)
    return p_sel * (jnp.where(lane == i1, w1, 0.0) + jnp.where(lane == i2, e2 * w1, 0.0))


def _moe_kernel(x_ref, gf_ref, wr_ref, br_ref, wg_ref, wu_ref, wd_ref, gfin_ref, o_ref, h_ref, gate_ref, acc_ref):
    g = pl.program_id(1)

    @pl.when(g == 0)
    def _init():
        h = _rmsnorm(x_ref[...], gf_ref[...])
        h_ref[...] = _bf(h)
        gate_ref[...] = _route(_dot3(h, wr_ref[...]) + br_ref[...])
        acc_ref[...] = jnp.zeros_like(acc_ref)

    hb = h_ref[...]
    hg = _dot(hb, wg_ref[0])
    hu = _dot(hb, wu_ref[0])
    width = EXPERTS_PER_GROUP * EXPERT_HIDDEN
    expand = _mask_bf(_iota((128, width), 0) == g * EXPERTS_PER_GROUP + (_iota((128, width), 1) >> 7))
    act = hg * _sigmoid(hg) * hu * _dot_sel_rhs(gate_ref[...], expand)
    acc_ref[...] += _dot(_bf(act), wd_ref[0])

    @pl.when(g == N_GROUPS - 1)
    def _fin():
        o_ref[...] = _rmsnorm(x_ref[...] + acc_ref[...], gfin_ref[...])


def _moe(x, gain, w_router, b_router, wg, wu, wd, gain_final, tm):
    n, d = x.shape
    width = EXPERTS_PER_GROUP * EXPERT_HIDDEN
    const = lambda a: pl.BlockSpec(a.shape, lambda i, g: (0,) * a.ndim)
    return pl.pallas_call(
        _moe_kernel,
        grid=(n // tm, N_GROUPS),
        in_specs=[pl.BlockSpec((tm, d), lambda i, g: (i, 0)), const(gain), const(w_router), const(b_router),
                  pl.BlockSpec((1, d, width), lambda i, g: (g, 0, 0)),
                  pl.BlockSpec((1, d, width), lambda i, g: (g, 0, 0)),
                  pl.BlockSpec((1, width, d), lambda i, g: (g, 0, 0)), const(gain_final)],
        out_specs=pl.BlockSpec((tm, d), lambda i, g: (i, 0)),
        out_shape=jax.ShapeDtypeStruct((n, d), F32),
        scratch_shapes=[pltpu.VMEM((tm, d), BF16), pltpu.VMEM((tm, 128), F32), pltpu.VMEM((tm, d), F32)],
        compiler_params=_params(("parallel", "arbitrary"), 56),
        name="moe",
    )(x, gain, w_router, b_router, wg, wu, wd, gain_final)


def kernel(x_prompt, x_sample, mem_prompt, state_rwkv_shift, state_rwkv_wkv, cache_swa_k, cache_swa_v, cache_mem_k, cache_mem_v, norm_mix, w_in, mu_shift, w_decay0, w_decay_up, a0, w_a_up, w_g_up, k_k, k_a, r_k, ln_x_w, ln_x_b, attn_sink, w_out, norm_cross, norm_mem, w_cq, w_ck, w_cv, w_co, norm_ffn, w_group_router, b_group_router, w_expert_router, b_expert_router, w_exp_gate, w_exp_up, w_exp_down, norm_final):
    bsz, t, d = x_prompt.shape
    n_s = x_sample.shape[0]
    wb = cache_swa_k.shape[2]
    width = EXPERTS_PER_GROUP * EXPERT_HIDDEN

    w_in_bf = w_in[0].astype(BF16)
    wo1, wo2 = w_out[0, :RWKV_WIDTH].astype(BF16), w_out[0, RWKV_WIDTH:].astype(BF16)
    wq, wco = w_cq[0].astype(BF16), w_co[0].astype(BF16)
    w_ckv = jnp.concatenate([w_ck[0], w_cv[0]], axis=1).astype(BF16)
    w_router = jnp.pad(jnp.concatenate([w_expert_router[0], w_group_router[0]], axis=1), ((0, 0), (0, 128 - 36)))
    b_router = jnp.pad(jnp.concatenate([b_expert_router[0], b_group_router[0]]), (0, 128 - 36)).reshape(1, 128)
    wg = jnp.transpose(w_exp_gate[0], (0, 2, 1, 3)).reshape(N_GROUPS, d, width).astype(BF16)
    wu = jnp.transpose(w_exp_up[0], (0, 2, 1, 3)).reshape(N_GROUPS, d, width).astype(BF16)
    wd = w_exp_down[0].reshape(N_GROUPS, width, d).astype(BF16)
    row = lambda a: a.reshape(1, -1)
    lp = dict(mu_shift=mu_shift, w_decay0=w_decay0, w_decay_up=w_decay_up[0], a0=a0, w_a_up=w_a_up[0],
              w_g_up=w_g_up[0], k_k=k_k, k_a=k_a, r_k=row(r_k), ln_x_w=ln_x_w, ln_x_b=ln_x_b)
    half = HEAD // 2
    inv_freq = ROPE_THETA ** (-jnp.arange(half, dtype=F32) * 2.0 / HEAD)
    inv_freq = jnp.tile(inv_freq, 128 // half).reshape(1, 128)

    p_rwkv, p_swa = _norm_matmul(x_prompt.reshape(bsz * t, d), norm_mix, w_in_bf, RWKV_COLS, 512)
    p_rwkv = p_rwkv.reshape(bsz, t, RWKV_COLS)
    y_rwkv, wkv_p = _rwkv_prompt(p_rwkv, lp)
    y_swa, k_p, v_p = _swa_prompt(p_swa.reshape(bsz, t, SWA_COLS), inv_freq, attn_sink)
    mk_p, mv_p = _norm_matmul(mem_prompt.reshape(bsz * MEM_TOKENS, d), norm_mem, w_ckv, X_WIDTH, 256)
    mk_b = mk_p.astype(BF16).reshape(bsz, MEM_TOKENS, X_WIDTH)
    mv_b = mv_p.astype(BF16).reshape(bsz, MEM_TOKENS, X_WIDTH)
    x2 = _out_cross(y_rwkv, y_swa, x_prompt, wo1, wo2, norm_cross, wq, mk_b, mv_b, wco)
    y_prompt = _moe(x2.reshape(bsz * t, d), norm_ffn, w_router, b_router, wg, wu, wd, row(norm_final), 512)

    xs = x_sample.reshape(n_s, d)
    ps_rwkv, ps_swa = _norm_matmul(xs, norm_mix, w_in[0], RWKV_COLS, n_s)
    ys_rwkv, wkv_s = _rwkv_sample(ps_rwkv, state_rwkv_shift[0], state_rwkv_wkv[0], lp)
    ys_swa, k_s, v_s = _swa_sample(ps_swa, cache_swa_k[0].reshape(n_s, wb, KV_WIDTH),
                                   cache_swa_v[0].reshape(n_s, wb, KV_WIDTH), inv_freq, attn_sink.reshape(N_HEADS, 1))
    x2s = _out_cross_sample(ys_rwkv, ys_swa, xs, w_out[0, :RWKV_WIDTH], w_out[0, RWKV_WIDTH:], norm_cross, w_cq[0],
                            cache_mem_k[0].reshape(n_s, MEM_TOKENS, X_WIDTH),
                            cache_mem_v[0].reshape(n_s, MEM_TOKENS, X_WIDTH), w_co[0])
    y_sample = _moe(x2s, norm_ffn, w_router, b_router, wg, wu, wd, row(norm_final), n_s)

    kv_shape = (1, -1, wb, KV_WIDTH // HEAD, HEAD)
    mem_shape = (1, bsz, MEM_TOKENS, X_HEADS, X_HEAD)
    return (y_prompt.reshape(bsz, t, d), y_sample.reshape(n_s, 1, d),
            p_rwkv[:, -1][None], wkv_p[None], k_p.reshape(kv_shape), v_p.reshape(kv_shape),
            mk_p.reshape(mem_shape), mv_p.reshape(mem_shape),
            ps_rwkv[None], wkv_s[None], k_s.reshape(kv_shape), v_s.reshape(kv_shape))
```

```python
import functools

import jax
import jax.numpy as jnp
from jax import lax
from jax.experimental import pallas as pl
from jax.experimental.pallas import tpu as pltpu

F32 = jnp.float32
BF16 = jnp.bfloat16

D_MODEL = 1024
RWKV_WIDTH = 512
HEAD = 64
N_HEADS = 8
RWKV_COLS = 1792
SWA_COLS = 768
KV_WIDTH = 128
WINDOW = 128
PAST_LEN = 8192
ROPE_THETA = 10000.0
MEM_TOKENS = 256
X_HEADS = 4
X_HEAD = 128
X_WIDTH = 512
N_GROUPS = 4
EXPERTS_PER_GROUP = 8
EXPERT_HIDDEN = 128
NORM_EPS = 1e-5
GN_EPS = 64e-5
L2_EPS = 1e-12
CHUNK = 64
NEG = -1e30
MIB = 1024 * 1024


def _params(semantics, vmem_mib):
    return pltpu.CompilerParams(dimension_semantics=semantics, vmem_limit_bytes=vmem_mib * MIB)


def _bf(x):
    return x.astype(BF16)


def _dot(a, b):
    return jnp.dot(a, b, preferred_element_type=F32)


def _dot_nt(a, b):
    return lax.dot_general(a, b, (((1,), (1,)), ((), ())), preferred_element_type=F32)


def _dot_tn(a, b):
    return lax.dot_general(a, b, (((0,), (0,)), ((), ())), preferred_element_type=F32)


def _split(x):
    hi = x.astype(BF16)
    lo = (x - hi.astype(F32)).astype(BF16)
    return hi, lo


def _dot3(a, b, dot=_dot):
    ah, al = _split(a)
    bh, bl = _split(b)
    return dot(ah, bh) + (dot(ah, bl) + dot(al, bh))


def _dot3_nt(a, b):
    return _dot3(a, b, _dot_nt)


def _dot_sel_rhs(a, sel):
    ah, al = _split(a)
    return _dot(ah, sel) + _dot(al, sel)


def _dot_sel_lhs(sel, b):
    bh, bl = _split(b)
    return _dot(sel, bh) + _dot(sel, bl)


def _rmsnorm(x, gain):
    return x * lax.rsqrt(jnp.mean(x * x, axis=-1, keepdims=True) + NORM_EPS) * gain


def _sigmoid(x):
    return 1.0 / (1.0 + jnp.exp(-x))


def _iota(shape, dim):
    return lax.broadcasted_iota(jnp.int32, shape, dim)


def _mask_bf(cond):
    return jnp.where(cond, 1.0, 0.0).astype(BF16)


def _block_ones(n, blk_shift):
    return _mask_bf((_iota((n, n), 0) >> blk_shift) == (_iota((n, n), 1) >> blk_shift))


def _softmax_sink(s, sink_col):
    m = jnp.maximum(jnp.max(s, axis=-1, keepdims=True), sink_col)
    e = jnp.exp(s - m)
    den = jnp.sum(e, axis=-1, keepdims=True) + jnp.exp(sink_col - m)
    return e / den


def _rope(x, cos, sin_signed):
    width = x.shape[1]
    reps = width // 128
    if reps > 1:
        cos = jnp.concatenate([cos] * reps, axis=1)
        sin_signed = jnp.concatenate([sin_signed] * reps, axis=1)
    first_half = (_iota(x.shape, 1) & (HEAD - 1)) < HEAD // 2
    rot = jnp.where(first_half, pltpu.roll(x, width - HEAD // 2, axis=1), pltpu.roll(x, HEAD // 2, axis=1))
    return x * cos + rot * sin_signed


def _rope_tables(pos, inv_freq):
    ang = pos * inv_freq
    cos, sin = jnp.cos(ang), jnp.sin(ang)
    first_half = (_iota(ang.shape, 1) & (HEAD - 1)) < HEAD // 2
    return cos, jnp.where(first_half, -sin, sin)


def _norm_matmul_kernel(x_ref, g_ref, w_ref, o1_ref, o2_ref, *, split):
    h = _rmsnorm(x_ref[...], g_ref[...])
    w = w_ref[...]
    p = _dot(_bf(h), w) if w.dtype == BF16 else _dot3(h, w)
    o1_ref[...] = p[:, :split]
    o2_ref[...] = p[:, split:]


def _norm_matmul(x, gain, w, split, tm):
    n, d = x.shape
    cols = w.shape[1]
    return pl.pallas_call(
        functools.partial(_norm_matmul_kernel, split=split),
        grid=(n // tm,),
        in_specs=[pl.BlockSpec((tm, d), lambda i: (i, 0)),
                  pl.BlockSpec((1, d), lambda i: (0, 0)),
                  pl.BlockSpec((d, cols), lambda i: (0, 0))],
        out_specs=[pl.BlockSpec((tm, split), lambda i: (i, 0)),
                   pl.BlockSpec((tm, cols - split), lambda i: (i, 0))],
        out_shape=[jax.ShapeDtypeStruct((n, split), F32), jax.ShapeDtypeStruct((n, cols - split), F32)],
        compiler_params=_params(("parallel",), 48),
        name="norm_matmul",
    )(x, gain, w)


def _rwkv_prep(p, prev, mu, w0, wd, a0, wa, wg, k_k, k_a, ones_bd):
    c = RWKV_WIDTH
    xm = p + (prev - p) * mu
    r, k, v = xm[:, :c], xm[:, c:2 * c], xm[:, 2 * c:3 * c]
    xw, xa, xg = xm[:, 3 * c:3 * c + 64], xm[:, 3 * c + 64:3 * c + 128], xm[:, 3 * c + 128:]
    z = -(w0 + _dot3(jnp.tanh(xw), wd))
    w = -(jnp.maximum(z, 0.0) + jnp.log(1.0 + jnp.exp(-jnp.abs(z)))) - 0.5
    logw = -jnp.exp(w)
    rate = _sigmoid(a0 + _dot3(xa, wa))
    gate = _dot3(_sigmoid(xg), wg)
    kk = k * k_k
    norm = jnp.sqrt(_dot_sel_rhs(kk * kk, ones_bd))
    kk = kk / jnp.maximum(norm, L2_EPS)
    k2 = k * (1.0 + (rate - 1.0) * k_a)
    return r, logw, k2, v, -kk, kk * rate, gate


def _rwkv_post(y, r, k2, v, gate, r_k, ln_w, ln_b, ones_bd):
    mean = _dot_sel_rhs(y, ones_bd) * (1.0 / HEAD)
    d = y - mean
    var = _dot_sel_rhs(d * d, ones_bd) * (1.0 / HEAD)
    yn = d * lax.rsqrt(var + GN_EPS) * ln_w + ln_b
    bonus = _dot_sel_rhs(r * k2 * r_k, ones_bd) * v
    return (yn + bonus) * gate


def _rwkv_prompt_kernel(p_ref, mu_ref, w0_ref, wd_ref, a0_ref, wa_ref, wg_ref, kk_ref, ka_ref, rk_ref, lnw_ref,
                        lnb_ref, y_ref, s_ref, carry_ref, st_ref, ar_ref, bk_ref, bkp_ref, v_ref, pc_ref, yh_ref,
                        *, tc):
    c = pl.program_id(1)

    @pl.when(c == 0)
    def _init():
        carry_ref[...] = jnp.zeros_like(carry_ref)
        st_ref[...] = jnp.zeros_like(st_ref)

    p = p_ref[0]
    first_row = _iota((tc, 1), 0) == 0
    prev = jnp.where(first_row, carry_ref[0:1, :], pltpu.roll(p, 1, axis=0))
    carry_ref[0:1, :] = p[tc - 1:tc, :]
    ones_bd = _block_ones(RWKV_WIDTH, 6)
    r, logw, k2, v, a, b, gate = _rwkv_prep(p, prev, mu_ref[...], w0_ref[...], wd_ref[...], a0_ref[...],
                                             wa_ref[...], wg_ref[...], kk_ref[...], ka_ref[...], ones_bd)
    ri, ci = _iota((tc, tc), 0), _iota((tc, tc), 1)
    same_chunk = (ri >> 6) == (ci >> 6)
    cum = _dot_sel_lhs(_mask_bf(same_chunk & (ci <= ri)), logw)
    tot = _dot_sel_lhs(_mask_bf(same_chunk), logw)
    e_inv = jnp.exp(-cum)
    e_end = jnp.exp(tot - cum)
    pieces = ((ar_ref, a * jnp.exp(cum - logw), r * jnp.exp(cum)),
              (bk_ref, b * e_inv, k2 * e_inv),
              (bkp_ref, b * e_end, k2 * e_end))
    pc = jnp.exp(tot)
    for h in range(N_HEADS):
        sl = slice(h * HEAD, (h + 1) * HEAD)
        for ref, x0, x1 in pieces:
            ref[h, 0] = _bf(x0[:, sl])
            ref[h, 1] = _bf(x1[:, sl])
        v_ref[h] = _bf(v[:, sl])
        pc_ref[h] = pc[:, sl]

    rr, cc = _iota((CHUNK, CHUNK), 0), _iota((CHUNK, CHUNK), 1)
    strict, incl, eye = cc < rr, cc <= rr, cc == rr

    n_ch = tc // CHUNK
    ids = [(ci_, h) for ci_ in range(n_ch) for h in range(N_HEADS)]
    rows = lambda ci_: slice(ci_ * CHUNK, (ci_ + 1) * CHUNK)
    both = lambda ref, i: jnp.concatenate([ref[i[1], 0, rows(i[0]), :], ref[i[1], 1, rows(i[0]), :]], axis=0)
    at = {i: ar_ref[i[1], 0, rows(i[0]), :] for i in ids}
    rt = {i: ar_ref[i[1], 1, rows(i[0]), :] for i in ids}
    vh = {i: v_ref[i[1], rows(i[0]), :] for i in ids}
    bp = {i: bkp_ref[i[1], 0, rows(i[0]), :] for i in ids}
    g = {i: _dot_nt(both(ar_ref, i), both(bk_ref, i)) for i in ids}
    n0 = {i: _dot_tn(vh[i], bkp_ref[i[1], 1, rows(i[0]), :]) for i in ids}
    a_ab = {i: jnp.where(strict, g[i][:CHUNK, :CHUNK], 0.0) for i in ids}
    a_rb = {i: _bf(jnp.where(incl, g[i][CHUNK:, :CHUNK], 0.0)) for i in ids}
    av = {i: _dot(_bf(jnp.concatenate([jnp.where(strict, g[i][:CHUNK, CHUNK:], 0.0),
                                       jnp.where(incl, g[i][CHUNK:, CHUNK:], 0.0)], axis=0)), vh[i]) for i in ids}
    t = {i: jnp.where(eye, 1.0, a_ab[i]) for i in ids}
    apow = {i: _bf(a_ab[i]) for i in ids}
    for step in range(5):
        apow = {i: _bf(_dot(apow[i], apow[i])) for i in ids}
        t = {i: t[i] + _dot(_bf(t[i]), apow[i]) for i in ids}
    t = {i: _bf(t[i]) for i in ids}
    w = {i: _bf(_dot(t[i], at[i])) for i in ids}
    u0 = {i: _bf(_dot(t[i], _bf(av[i][:CHUNK]))) for i in ids}
    wy = {i: _bf(rt[i].astype(F32) + _dot(a_rb[i], w[i])) for i in ids}
    y0 = {i: av[i][CHUNK:] + _dot(a_rb[i], u0[i]) for i in ids}
    mb = {i: _bf(_dot_tn(w[i], bp[i])) for i in ids}
    n0 = {i: n0[i] + _dot_tn(u0[i], bp[i]) for i in ids}
    state = [st_ref[h] for h in range(N_HEADS)]
    for ci_ in range(n_ch):
        sb = [_bf(s) for s in state]
        ys = [y0[(ci_, h)] + _dot_nt(wy[(ci_, h)], sb[h]) for h in range(N_HEADS)]
        state = [state[h] * pc_ref[h, ci_ * CHUNK:ci_ * CHUNK + 1, :] + _dot(sb[h], mb[(ci_, h)]) + n0[(ci_, h)]
                 for h in range(N_HEADS)]
        for h in range(N_HEADS):
            yh_ref[h, rows(ci_), :] = ys[h]
    for h in range(N_HEADS):
        st_ref[h] = state[h]
    y = jnp.concatenate([yh_ref[h] for h in range(N_HEADS)], axis=1)
    y_ref[0] = _bf(_rwkv_post(y, r, k2, v, gate, rk_ref[...], lnw_ref[...], lnb_ref[...], ones_bd))
    s_ref[0] = st_ref[...]


def _rwkv_prompt(p, lp, tc=256):
    bsz, t, _ = p.shape
    row = lambda n: pl.BlockSpec((1, n), lambda b, c: (0, 0))
    mat = lambda m, n: pl.BlockSpec((m, n), lambda b, c: (0, 0))
    head_scr = lambda dt: pltpu.VMEM((N_HEADS, 2, tc, HEAD), dt)
    return pl.pallas_call(
        functools.partial(_rwkv_prompt_kernel, tc=tc),
        grid=(bsz, t // tc),
        in_specs=[pl.BlockSpec((1, tc, RWKV_COLS), lambda b, c: (b, c, 0)),
                  row(RWKV_COLS), row(512), mat(64, 512), row(512), mat(64, 512), mat(128, 512),
                  row(512), row(512), row(512), row(512), row(512)],
        out_specs=[pl.BlockSpec((1, tc, RWKV_WIDTH), lambda b, c: (b, c, 0)),
                   pl.BlockSpec((1, N_HEADS, HEAD, HEAD), lambda b, c: (b, 0, 0, 0))],
        out_shape=[jax.ShapeDtypeStruct((bsz, t, RWKV_WIDTH), BF16),
                   jax.ShapeDtypeStruct((bsz, N_HEADS, HEAD, HEAD), F32)],
        scratch_shapes=[pltpu.VMEM((8, RWKV_COLS), F32), pltpu.VMEM((N_HEADS, HEAD, HEAD), F32),
                        head_scr(BF16), head_scr(BF16), head_scr(BF16),
                        pltpu.VMEM((N_HEADS, tc, HEAD), BF16), pltpu.VMEM((N_HEADS, tc, HEAD), F32),
                        pltpu.VMEM((N_HEADS, tc, HEAD), F32)],
        compiler_params=_params(("arbitrary", "arbitrary"), 48),
        name="rwkv_prompt",
    )(p, lp["mu_shift"], lp["w_decay0"], lp["w_decay_up"], lp["a0"], lp["w_a_up"], lp["w_g_up"], lp["k_k"],
      lp["k_a"], lp["r_k"], lp["ln_x_w"], lp["ln_x_b"])


def _rwkv_sample_prep_kernel(p_ref, prev_ref, mu_ref, w0_ref, wd_ref, a0_ref, wa_ref, wg_ref, kk_ref, ka_ref,
                             r_ref, k_ref, v_ref, g_ref, *t_refs):
    ones_bd = _block_ones(RWKV_WIDTH, 6)
    r, logw, k2, v, a, b, gate = _rwkv_prep(p_ref[...], prev_ref[...], mu_ref[...], w0_ref[...], wd_ref[...],
                                             a0_ref[...], wa_ref[...], wg_ref[...], kk_ref[...], ka_ref[...], ones_bd)
    r_ref[...] = r
    k_ref[...] = k2
    v_ref[...] = v
    g_ref[...] = gate
    for ref, x in zip(t_refs, (r, jnp.exp(logw), k2, v, a, b)):
        ref[...] = x.T


def _rwkv_sample_step_kernel(s_ref, r_ref, w_ref, k_ref, v_ref, a_ref, b_ref, so_ref, y_ref, *, unroll):
    a, w, b, k, r = a_ref[...], w_ref[...], b_ref[...], k_ref[...], r_ref[...]

    def rows(i0, carry):
        base = pl.multiple_of(i0 * unroll, unroll)
        v8 = v_ref[pl.ds(base, unroll), :]
        s = [s_ref[0, base + u] for u in range(unroll)]
        sa = [jnp.sum(s[u] * a, axis=0, keepdims=True) for u in range(unroll)]
        s = [s[u] * w + sa[u] * b + v8[u:u + 1, :] * k for u in range(unroll)]
        for u in range(unroll):
            so_ref[0, base + u] = s[u]
        y_ref[pl.ds(base, unroll), :] = jnp.concatenate(
            [jnp.sum(s[u] * r, axis=0, keepdims=True) for u in range(unroll)], axis=0)
        return carry

    lax.fori_loop(0, HEAD // unroll, rows, 0)


def _rwkv_sample(p, prev, wkv_t, lp, unroll=8):
    n = p.shape[0]
    full = lambda a: pl.BlockSpec(a.shape, lambda i: (0,) * a.ndim)
    names = ("mu_shift", "w_decay0", "w_decay_up", "a0", "w_a_up", "w_g_up", "k_k", "k_a")
    prep_in = [p, prev] + [lp[k] for k in names]
    vecs = pl.pallas_call(
        _rwkv_sample_prep_kernel,
        grid=(1,),
        in_specs=[full(a) for a in prep_in],
        out_specs=[pl.BlockSpec((n, RWKV_WIDTH), lambda i: (0, 0))] * 4
        + [pl.BlockSpec((RWKV_WIDTH, n), lambda i: (0, 0))] * 6,
        out_shape=[jax.ShapeDtypeStruct((n, RWKV_WIDTH), F32)] * 4
        + [jax.ShapeDtypeStruct((RWKV_WIDTH, n), F32)] * 6,
        compiler_params=_params(("arbitrary",), 32),
        name="rwkv_sample_prep",
    )(*prep_in)
    vec_t = pl.BlockSpec((HEAD, n), lambda h: (h, 0))
    st_spec = pl.BlockSpec((1, HEAD, HEAD, n), lambda h: (h, 0, 0, 0))
    wkv_new, y_t = pl.pallas_call(
        functools.partial(_rwkv_sample_step_kernel, unroll=unroll),
        grid=(N_HEADS,),
        in_specs=[st_spec] + [vec_t] * 6,
        out_specs=[st_spec, vec_t],
        out_shape=[jax.ShapeDtypeStruct(wkv_t.shape, F32), jax.ShapeDtypeStruct((RWKV_WIDTH, n), F32)],
        compiler_params=_params(("parallel",), 32),
        name="rwkv_sample_step",
    )(wkv_t, *vecs[4:])
    return vecs[:4], y_t, wkv_new


def _rope_table_kernel(invf_ref, cos_ref, sin_ref, *, rows):
    pos = (pl.program_id(0) * rows + _iota((rows, 1), 0)).astype(F32)
    cos_ref[...], sin_ref[...] = _rope_tables(pos, invf_ref[...])


def _rope_table(t, inv_freq, rows=512):
    spec = pl.BlockSpec((rows, 128), lambda i: (i, 0))
    return pl.pallas_call(
        functools.partial(_rope_table_kernel, rows=rows),
        grid=(t // rows,),
        in_specs=[pl.BlockSpec((1, 128), lambda i: (0, 0))],
        out_specs=[spec, spec],
        out_shape=[jax.ShapeDtypeStruct((t, 128), F32)] * 2,
        compiler_params=_params(("parallel",), 32),
        name="rope_table",
    )(inv_freq)


def _swa_prompt_kernel(p_ref, cos_ref, sin_ref, sink_ref, y_ref, ko_ref, vo_ref, kprev_ref, vprev_ref, *, nb):
    n = pl.program_id(1)
    blk = WINDOW

    @pl.when(n == 0)
    def _init():
        kprev_ref[...] = jnp.zeros_like(kprev_ref)
        vprev_ref[...] = jnp.zeros_like(vprev_ref)

    p = p_ref[0]
    cos, sin_signed = cos_ref[...], sin_ref[...]
    q = _rope(p[:, :RWKV_WIDTH], cos, sin_signed)
    k = _rope(p[:, RWKV_WIDTH:RWKV_WIDTH + KV_WIDTH], cos, sin_signed)
    v = p[:, RWKV_WIDTH + KV_WIDTH:]
    kcat = jnp.concatenate([kprev_ref[...], k], axis=0)
    vcat = jnp.concatenate([vprev_ref[...], v], axis=0)
    ii, jj = _iota((blk, 2 * blk), 0), _iota((blk, 2 * blk), 1)
    in_window = (jj > ii) & (jj <= ii + blk)
    bias = jnp.concatenate([jnp.where(in_window, 0.0, NEG)] * 4, axis=0)
    lo = jnp.where(n == 0, blk, 0)
    bias_first = jnp.concatenate([jnp.where(in_window & (jj >= lo), 0.0, NEG)] * 4, axis=0)
    units = [(j, g) for j in range(nb) for g in range(2)]
    rows = lambda j, m: slice(j * blk, (j + m) * blk)
    heads = lambda g: range(4 * g, 4 * g + 4)
    cols = lambda g: slice(g * HEAD, (g + 1) * HEAD)
    sink_col = [jnp.concatenate([jnp.broadcast_to(sink_ref[0:1, h:h + 1], (blk, 1)) for h in heads(g)], axis=0)
                for g in range(2)]
    qg = {(j, g): _bf(jnp.concatenate([q[rows(j, 1), h * HEAD:(h + 1) * HEAD] for h in heads(g)], axis=0))
          for j, g in units}
    s = {(j, g): _dot_nt(qg[(j, g)], _bf(kcat[rows(j, 2), cols(g)])) * (HEAD ** -0.5)
         + (bias_first if j == 0 else bias) for j, g in units}
    pr = {(j, g): _bf(_softmax_sink(s[(j, g)], sink_col[g])) for j, g in units}
    o = {(j, g): _dot(pr[(j, g)], _bf(vcat[rows(j, 2), cols(g)])) for j, g in units}
    y_ref[0] = _bf(jnp.concatenate(
        [jnp.concatenate([o[(j, g)][i * blk:(i + 1) * blk] for g in range(2) for i in range(4)], axis=1)
         for j in range(nb)], axis=0))
    k_last, v_last = k[(nb - 1) * blk:], v[(nb - 1) * blk:]
    ko_ref[0] = k_last
    vo_ref[0] = v_last
    kprev_ref[...] = k_last
    vprev_ref[...] = v_last


def _swa_prompt(p, inv_freq, sink, nb=4):
    bsz, t, _ = p.shape
    blk = WINDOW
    rows = nb * blk
    cos, sin_signed = _rope_table(t, inv_freq, rows)
    table = pl.BlockSpec((rows, 128), lambda b, n: (n, 0))
    return pl.pallas_call(
        functools.partial(_swa_prompt_kernel, nb=nb),
        grid=(bsz, t // rows),
        in_specs=[pl.BlockSpec((1, rows, SWA_COLS), lambda b, n: (b, n, 0)), table, table,
                  pl.BlockSpec((1, N_HEADS), lambda b, n: (0, 0))],
        out_specs=[pl.BlockSpec((1, rows, RWKV_WIDTH), lambda b, n: (b, n, 0)),
                   pl.BlockSpec((1, blk, KV_WIDTH), lambda b, n: (b, 0, 0)),
                   pl.BlockSpec((1, blk, KV_WIDTH), lambda b, n: (b, 0, 0))],
        out_shape=[jax.ShapeDtypeStruct((bsz, t, RWKV_WIDTH), BF16),
                   jax.ShapeDtypeStruct((bsz, blk, KV_WIDTH), F32),
                   jax.ShapeDtypeStruct((bsz, blk, KV_WIDTH), F32)],
        scratch_shapes=[pltpu.VMEM((blk, KV_WIDTH), F32), pltpu.VMEM((blk, KV_WIDTH), F32)],
        compiler_params=_params(("arbitrary", "arbitrary"), 48),
        name="swa_prompt",
    )(p, cos, sin_signed, sink)


def _swa_sample_kernel(p_ref, ck_ref, cv_ref, invf_ref, sinkc_ref, y_ref, nk_ref, nv_ref, *, bt):
    p = p_ref[...]
    cos, sin_signed = _rope_tables(jnp.full((1, 1), float(PAST_LEN), F32), invf_ref[...])
    q = _rope(p[:, :RWKV_WIDTH], cos, sin_signed) * (HEAD ** -0.5)
    k = _rope(p[:, RWKV_WIDTH:RWKV_WIDTH + KV_WIDTH], cos, sin_signed)
    v = p[:, RWKV_WIDTH + KV_WIDTH:]
    k_t, v_t = k.T, v.T
    last = _iota((KV_WIDTH, WINDOW), 1) == WINDOW - 1
    head_mask = _iota((N_HEADS, RWKV_WIDTH), 0) == (_iota((N_HEADS, RWKV_WIDTH), 1) >> 6)
    tile_lanes = lambda x: jnp.concatenate([x[:, :HEAD]] * 4 + [x[:, HEAD:]] * 4, axis=1)
    tile_rows = lambda x: jnp.concatenate([x[:HEAD]] * 4 + [x[HEAD:]] * 4, axis=0)
    seqs = range(bt)
    k_new = [jnp.where(last, k_t[:, bi:bi + 1], pltpu.roll(ck_ref[bi], WINDOW - 1, axis=1)) for bi in seqs]
    v_new = [jnp.where(last, v_t[:, bi:bi + 1], pltpu.roll(cv_ref[bi], WINDOW - 1, axis=1)) for bi in seqs]
    for bi in seqs:
        nk_ref[bi] = k_new[bi]
        nv_ref[bi] = v_new[bi]
    q_bd = [jnp.where(head_mask, q[bi:bi + 1, :], 0.0) for bi in seqs]
    s = [_dot3(q_bd[bi], tile_rows(k_new[bi])) for bi in seqs]
    pr = [_softmax_sink(s[bi], sinkc_ref[...]) for bi in seqs]
    o = [_dot3_nt(pr[bi], v_new[bi]) for bi in seqs]
    y_ref[...] = jnp.concatenate(
        [jnp.sum(jnp.where(head_mask, tile_lanes(o[bi]), 0.0), axis=0, keepdims=True) for bi in seqs], axis=0)


def _swa_sample(p, ck, cv, inv_freq, sink_col, bt=8):
    n = p.shape[0]
    cache_spec = pl.BlockSpec((bt, KV_WIDTH, WINDOW), lambda i: (i, 0, 0))
    return pl.pallas_call(
        functools.partial(_swa_sample_kernel, bt=bt),
        grid=(n // bt,),
        in_specs=[pl.BlockSpec((bt, SWA_COLS), lambda i: (i, 0)), cache_spec, cache_spec,
                  pl.BlockSpec((1, 128), lambda i: (0, 0)), pl.BlockSpec((N_HEADS, 1), lambda i: (0, 0))],
        out_specs=[pl.BlockSpec((bt, RWKV_WIDTH), lambda i: (i, 0)), cache_spec, cache_spec],
        out_shape=[jax.ShapeDtypeStruct((n, RWKV_WIDTH), F32), jax.ShapeDtypeStruct(ck.shape, F32),
                   jax.ShapeDtypeStruct(cv.shape, F32)],
        compiler_params=_params(("parallel",), 32),
        name="swa_sample",
    )(p, ck, cv, inv_freq, sink_col)


def _mix_out(yr, ys, x, wo1, wo2):
    return x + _dot(yr, wo1) + _dot(ys, wo2)


def _out_cross_kernel(yr_ref, ys_ref, x_ref, wo1_ref, wo2_ref, gc_ref, wq_ref, mk_ref, mv_ref, wco_ref, o_ref):
    x1 = _mix_out(yr_ref[0], ys_ref[0], x_ref[0], wo1_ref[...], wo2_ref[...])
    q = _bf(_dot(_bf(_rmsnorm(x1, gc_ref[...])), wq_ref[...]))
    mk, mv = mk_ref[0], mv_ref[0]
    outs = []
    for h in range(X_HEADS):
        sl = slice(h * X_HEAD, (h + 1) * X_HEAD)
        s = _dot_nt(q[:, sl], mk[:, sl]) * (X_HEAD ** -0.5)
        e = jnp.exp(s - jnp.max(s, axis=-1, keepdims=True))
        pr = e / jnp.sum(e, axis=-1, keepdims=True)
        outs.append(_dot(_bf(pr), mv[:, sl]))
    o_ref[0] = x1 + _dot(_bf(jnp.concatenate(outs, axis=1)), wco_ref[...])


def _out_cross(yr, ys, x, wo1, wo2, gain, wq, mk, mv, wco, tm=512):
    bsz, t, d = x.shape
    tile = lambda w: pl.BlockSpec((1, tm, w), lambda b, i: (b, i, 0))
    const = lambda a: pl.BlockSpec(a.shape, lambda b, i: (0,) * a.ndim)
    mem = pl.BlockSpec((1, MEM_TOKENS, X_WIDTH), lambda b, i: (b, 0, 0))
    return pl.pallas_call(
        _out_cross_kernel,
        grid=(bsz, t // tm),
        in_specs=[tile(RWKV_WIDTH), tile(RWKV_WIDTH), tile(d), const(wo1), const(wo2), const(gain), const(wq),
                  mem, mem, const(wco)],
        out_specs=tile(d),
        out_shape=jax.ShapeDtypeStruct((bsz, t, d), F32),
        compiler_params=_params(("parallel", "parallel"), 48),
        name="out_cross",
    )(yr, ys, x, wo1, wo2, gain, wq, mk, mv, wco)


def _out_q_kernel(yt_ref, r_ref, k_ref, v_ref, g_ref, rk_ref, lnw_ref, lnb_ref, ys_ref, x_ref, wo1_ref, wo2_ref,
                  gc_ref, wq_ref, x1_ref, q_ref):
    yr = _rwkv_post(yt_ref[...].T, r_ref[...], k_ref[...], v_ref[...], g_ref[...], rk_ref[...], lnw_ref[...],
                    lnb_ref[...], _block_ones(RWKV_WIDTH, 6))
    x1 = x_ref[...] + _dot3(yr, wo1_ref[...]) + _dot3(ys_ref[...], wo2_ref[...])
    x1_ref[...] = x1
    q_ref[...] = _dot3(_rmsnorm(x1, gc_ref[...]), wq_ref[...])


def _cross_sample_kernel(q_ref, mk_ref, mv_ref, o_ref, *, bt):
    seqs = range(bt)
    s = [jnp.sum(mk_ref[bi] * q_ref[bi][None], axis=-1, keepdims=True) * (X_HEAD ** -0.5) for bi in seqs]
    e = [jnp.exp(s[bi] - jnp.max(s[bi], axis=0, keepdims=True)) for bi in seqs]
    pr = [e[bi] / jnp.sum(e[bi], axis=0, keepdims=True) for bi in seqs]
    for bi in seqs:
        o_ref[bi] = jnp.sum(pr[bi] * mv_ref[bi], axis=0)


def _lin_res_kernel(x_ref, a_ref, w_ref, o_ref):
    o_ref[...] = x_ref[...] + _dot3(a_ref[...], w_ref[...])


def _single_step(kernel_fn, out_shape, name, *args):
    full = lambda a: pl.BlockSpec(a.shape, lambda i: (0,) * len(a.shape))
    outs = out_shape if isinstance(out_shape, (list, tuple)) else [out_shape]
    out_specs = [full(o) for o in outs]
    return pl.pallas_call(
        kernel_fn, grid=(1,), in_specs=[full(a) for a in args],
        out_specs=out_specs if isinstance(out_shape, (list, tuple)) else out_specs[0],
        out_shape=out_shape, compiler_params=_params(("arbitrary",), 32), name=name,
    )(*args)


def _out_cross_sample(rwkv_parts, ys, x, wo1, wo2, gain, wq, mk, mv, wco, bt=4):
    n, d = x.shape
    x1, q = _single_step(_out_q_kernel, [jax.ShapeDtypeStruct((n, d), F32), jax.ShapeDtypeStruct((n, X_WIDTH), F32)],
                         "out_q_sample", *rwkv_parts, ys, x, wo1, wo2, gain, wq)
    mem = pl.BlockSpec((bt, MEM_TOKENS, X_HEADS, X_HEAD), lambda i: (i, 0, 0, 0))
    vec = pl.BlockSpec((bt, X_HEADS, X_HEAD), lambda i: (i, 0, 0))
    o = pl.pallas_call(
        functools.partial(_cross_sample_kernel, bt=bt),
        grid=(n // bt,),
        in_specs=[vec, mem, mem],
        out_specs=vec,
        out_shape=jax.ShapeDtypeStruct((n, X_HEADS, X_HEAD), F32),
        compiler_params=_params(("parallel",), 48),
        name="cross_sample",
    )(q.reshape(n, X_HEADS, X_HEAD), mk, mv)
    return _single_step(_lin_res_kernel, jax.ShapeDtypeStruct((n, d), F32), "cross_out_sample", x1,
                        o.reshape(n, X_WIDTH), wco)


def _route(logits):
    lane_i = _iota(logits.shape, 1)
    lane = lane_i.astype(F32)
    big = 1024.0
    n_exp = float(N_GROUPS * EXPERTS_PER_GROUP)
    lg = jnp.where((lane >= n_exp) & (lane < n_exp + N_GROUPS), logits, NEG)
    g_max = jnp.max(lg, axis=-1, keepdims=True)
    g_idx = jnp.min(jnp.where(lg == g_max, lane, big), axis=-1, keepdims=True) - n_exp
    p_sel = 1.0 / jnp.sum(jnp.exp(lg - g_max), axis=-1, keepdims=True)
    le = jnp.where((lane_i >> 3).astype(F32) == g_idx, logits, NEG)
    t1 = jnp.max(le, axis=-1, keepdims=True)
    i1 = jnp.min(jnp.where(le == t1, lane, big), axis=-1, keepdims=True)
    le2 = jnp.where(lane == i1, NEG, le)
    t2 = jnp.max(le2, axis=-1, keepdims=True)
    i2 = jnp.min(jnp.where(le2 == t2, lane, big), axis=-1, keepdims=True)
    e2 = jnp.exp(t2 - t1)
    w1 = 1.0 / (1.0 + e2)
    return p_sel * (jnp.where(lane == i1, w1, 0.0) + jnp.where(lane == i2, e2 * w1, 0.0))


def _moe_kernel(x_ref, gf_ref, wr_ref, br_ref, wg_ref, wu_ref, wd_ref, gfin_ref, o_ref, h_ref, gate_ref, acc_ref):
    g = pl.program_id(1)

    @pl.when(g == 0)
    def _init():
        h = _rmsnorm(x_ref[...], gf_ref[...])
        h_ref[...] = _bf(h)
        gate_ref[...] = _route(_dot3(h, wr_ref[...]) + br_ref[...])
        acc_ref[...] = jnp.zeros_like(acc_ref)

    hb = h_ref[...]
    hg = _dot(hb, wg_ref[0])
    hu = _dot(hb, wu_ref[0])
    width = EXPERTS_PER_GROUP * EXPERT_HIDDEN
    expand = _mask_bf(_iota((128, width), 0) == g * EXPERTS_PER_GROUP + (_iota((128, width), 1) >> 7))
    act = hg * _sigmoid(hg) * hu * _dot_sel_rhs(gate_ref[...], expand)
    acc_ref[...] += _dot(_bf(act), wd_ref[0])

    @pl.when(g == N_GROUPS - 1)
    def _fin():
        o_ref[...] = _rmsnorm(x_ref[...] + acc_ref[...], gfin_ref[...])


def _moe(x, gain, w_router, b_router, wg, wu, wd, gain_final, tm):
    n, d = x.shape
    width = EXPERTS_PER_GROUP * EXPERT_HIDDEN
    const = lambda a: pl.BlockSpec(a.shape, lambda i, g: (0,) * a.ndim)
    return pl.pallas_call(
        _moe_kernel,
        grid=(n // tm, N_GROUPS),
        in_specs=[pl.BlockSpec((tm, d), lambda i, g: (i, 0)), const(gain), const(w_router), const(b_router),
                  pl.BlockSpec((1, d, width), lambda i, g: (g, 0, 0)),
                  pl.BlockSpec((1, d, width), lambda i, g: (g, 0, 0)),
                  pl.BlockSpec((1, width, d), lambda i, g: (g, 0, 0)), const(gain_final)],
        out_specs=pl.BlockSpec((tm, d), lambda i, g: (i, 0)),
        out_shape=jax.ShapeDtypeStruct((n, d), F32),
        scratch_shapes=[pltpu.VMEM((tm, d), BF16), pltpu.VMEM((tm, 128), F32), pltpu.VMEM((tm, d), F32)],
        compiler_params=_params(("parallel", "arbitrary"), 56),
        name="moe",
    )(x, gain, w_router, b_router, wg, wu, wd, gain_final)


def kernel(x_prompt, x_sample, mem_prompt, state_rwkv_shift, state_rwkv_wkv, cache_swa_k, cache_swa_v, cache_mem_k, cache_mem_v, norm_mix, w_in, mu_shift, w_decay0, w_decay_up, a0, w_a_up, w_g_up, k_k, k_a, r_k, ln_x_w, ln_x_b, attn_sink, w_out, norm_cross, norm_mem, w_cq, w_ck, w_cv, w_co, norm_ffn, w_group_router, b_group_router, w_expert_router, b_expert_router, w_exp_gate, w_exp_up, w_exp_down, norm_final):
    bsz, t, d = x_prompt.shape
    n_s = x_sample.shape[0]
    wb = cache_swa_k.shape[2]
    width = EXPERTS_PER_GROUP * EXPERT_HIDDEN

    w_in_bf = w_in[0].astype(BF16)
    wo1, wo2 = w_out[0, :RWKV_WIDTH].astype(BF16), w_out[0, RWKV_WIDTH:].astype(BF16)
    wq, wco = w_cq[0].astype(BF16), w_co[0].astype(BF16)
    w_ckv = jnp.concatenate([w_ck[0], w_cv[0]], axis=1).astype(BF16)
    w_router = jnp.pad(jnp.concatenate([w_expert_router[0], w_group_router[0]], axis=1), ((0, 0), (0, 128 - 36)))
    b_router = jnp.pad(jnp.concatenate([b_expert_router[0], b_group_router[0]]), (0, 128 - 36)).reshape(1, 128)
    wg = jnp.transpose(w_exp_gate[0], (0, 2, 1, 3)).reshape(N_GROUPS, d, width).astype(BF16)
    wu = jnp.transpose(w_exp_up[0], (0, 2, 1, 3)).reshape(N_GROUPS, d, width).astype(BF16)
    wd = w_exp_down[0].reshape(N_GROUPS, width, d).astype(BF16)
    row = lambda a: a.reshape(1, -1)
    lp = dict(mu_shift=mu_shift, w_decay0=w_decay0, w_decay_up=w_decay_up[0], a0=a0, w_a_up=w_a_up[0],
              w_g_up=w_g_up[0], k_k=k_k, k_a=k_a, r_k=row(r_k), ln_x_w=ln_x_w, ln_x_b=ln_x_b)
    half = HEAD // 2
    inv_freq = ROPE_THETA ** (-jnp.arange(half, dtype=F32) * 2.0 / HEAD)
    inv_freq = jnp.tile(inv_freq, 128 // half).reshape(1, 128)

    p_rwkv, p_swa = _norm_matmul(x_prompt.reshape(bsz * t, d), norm_mix, w_in_bf, RWKV_COLS, 512)
    p_rwkv = p_rwkv.reshape(bsz, t, RWKV_COLS)
    y_rwkv, wkv_p = _rwkv_prompt(p_rwkv, lp)
    y_swa, k_p, v_p = _swa_prompt(p_swa.reshape(bsz, t, SWA_COLS), inv_freq, attn_sink)
    mk_p, mv_p = _norm_matmul(mem_prompt.reshape(bsz * MEM_TOKENS, d), norm_mem, w_ckv, X_WIDTH, 256)
    mk_b = mk_p.astype(BF16).reshape(bsz, MEM_TOKENS, X_WIDTH)
    mv_b = mv_p.astype(BF16).reshape(bsz, MEM_TOKENS, X_WIDTH)
    x2 = _out_cross(y_rwkv, y_swa, x_prompt, wo1, wo2, norm_cross, wq, mk_b, mv_b, wco)
    y_prompt = _moe(x2.reshape(bsz * t, d), norm_ffn, w_router, b_router, wg, wu, wd, row(norm_final), 512)

    xs = x_sample.reshape(n_s, d)
    ps_rwkv, ps_swa = _norm_matmul(xs, norm_mix, w_in[0], RWKV_COLS, n_s)
    (r_s, k2_s, v_s_tok, g_s), yt_s, wkv_s = _rwkv_sample(ps_rwkv, state_rwkv_shift[0],
                                                           jnp.transpose(state_rwkv_wkv[0], (1, 2, 3, 0)), lp)
    cache_t = lambda c: jnp.transpose(c[0].reshape(n_s, wb, KV_WIDTH), (0, 2, 1))
    ys_swa, k_s, v_s = _swa_sample(ps_swa, cache_t(cache_swa_k), cache_t(cache_swa_v), inv_freq,
                                   attn_sink.reshape(N_HEADS, 1))
    x2s = _out_cross_sample((yt_s, r_s, k2_s, v_s_tok, g_s, lp["r_k"], lp["ln_x_w"], lp["ln_x_b"]), ys_swa, xs,
                            w_out[0, :RWKV_WIDTH], w_out[0, RWKV_WIDTH:], norm_cross, w_cq[0],
                            cache_mem_k[0], cache_mem_v[0], w_co[0])
    y_sample = _moe(x2s, norm_ffn, w_router, b_router, wg, wu, wd, row(norm_final), n_s)

    kv_shape = (1, -1, wb, KV_WIDTH // HEAD, HEAD)
    mem_shape = (1, bsz, MEM_TOKENS, X_HEADS, X_HEAD)
    cache_back = lambda c: jnp.transpose(c, (0, 2, 1)).reshape(kv_shape)
    return (y_prompt.reshape(bsz, t, d), y_sample.reshape(n_s, 1, d),
            p_rwkv[:, -1][None], wkv_p[None], k_p.reshape(kv_shape), v_p.reshape(kv_shape),
            mk_p.reshape(mem_shape), mv_p.reshape(mem_shape),
            ps_rwkv[None], jnp.transpose(wkv_s, (3, 0, 1, 2))[None], cache_back(k_s), cache_back(v_s))
```

```python
import functools

import jax
import jax.numpy as jnp
from jax import lax
from jax.experimental import pallas as pl
from jax.experimental.pallas import tpu as pltpu

F32 = jnp.float32
BF16 = jnp.bfloat16

D_MODEL = 1024
RWKV_WIDTH = 512
HEAD = 64
N_HEADS = 8
RWKV_COLS = 1792
SWA_COLS = 768
KV_WIDTH = 128
WINDOW = 128
PAST_LEN = 8192
ROPE_THETA = 10000.0
MEM_TOKENS = 256
X_HEADS = 4
X_HEAD = 128
X_WIDTH = 512
N_GROUPS = 4
EXPERTS_PER_GROUP = 8
EXPERT_HIDDEN = 128
NORM_EPS = 1e-5
GN_EPS = 64e-5
L2_EPS = 1e-12
CHUNK = 64
NEG = -1e30
MIB = 1024 * 1024


def _params(semantics, vmem_mib):
    return pltpu.CompilerParams(dimension_semantics=semantics, vmem_limit_bytes=vmem_mib * MIB)


def _bf(x):
    return x.astype(BF16)


def _dot(a, b):
    return jnp.dot(a, b, preferred_element_type=F32)


def _dot_nt(a, b):
    return lax.dot_general(a, b, (((1,), (1,)), ((), ())), preferred_element_type=F32)


def _dot_tn(a, b):
    return lax.dot_general(a, b, (((0,), (0,)), ((), ())), preferred_element_type=F32)


def _split(x):
    hi = x.astype(BF16)
    lo = (x - hi.astype(F32)).astype(BF16)
    return hi, lo


def _dot3(a, b, dot=_dot):
    ah, al = _split(a)
    bh, bl = _split(b)
    return dot(ah, bh) + (dot(ah, bl) + dot(al, bh))


def _dot3_nt(a, b):
    return _dot3(a, b, _dot_nt)


def _dot_sel_rhs(a, sel):
    ah, al = _split(a)
    return _dot(ah, sel) + _dot(al, sel)


def _dot_sel_lhs(sel, b):
    bh, bl = _split(b)
    return _dot(sel, bh) + _dot(sel, bl)


def _rmsnorm(x, gain):
    return x * lax.rsqrt(jnp.mean(x * x, axis=-1, keepdims=True) + NORM_EPS) * gain


def _sigmoid(x):
    return 1.0 / (1.0 + jnp.exp(-x))


def _iota(shape, dim):
    return lax.broadcasted_iota(jnp.int32, shape, dim)


def _mask_bf(cond):
    return jnp.where(cond, 1.0, 0.0).astype(BF16)


def _block_ones(n, blk_shift):
    return _mask_bf((_iota((n, n), 0) >> blk_shift) == (_iota((n, n), 1) >> blk_shift))


def _softmax_sink(s, sink_col):
    m = jnp.maximum(jnp.max(s, axis=-1, keepdims=True), sink_col)
    e = jnp.exp(s - m)
    den = jnp.sum(e, axis=-1, keepdims=True) + jnp.exp(sink_col - m)
    return e / den


def _rope(x, cos, sin_signed):
    width = x.shape[1]
    reps = width // 128
    if reps > 1:
        cos = jnp.concatenate([cos] * reps, axis=1)
        sin_signed = jnp.concatenate([sin_signed] * reps, axis=1)
    first_half = (_iota(x.shape, 1) & (HEAD - 1)) < HEAD // 2
    rot = jnp.where(first_half, pltpu.roll(x, width - HEAD // 2, axis=1), pltpu.roll(x, HEAD // 2, axis=1))
    return x * cos + rot * sin_signed


def _rope_tables(pos, inv_freq):
    ang = pos * inv_freq
    cos, sin = jnp.cos(ang), jnp.sin(ang)
    first_half = (_iota(ang.shape, 1) & (HEAD - 1)) < HEAD // 2
    return cos, jnp.where(first_half, -sin, sin)


def _norm_matmul_kernel(x_ref, g_ref, w_ref, o1_ref, o2_ref, *, split):
    h = _rmsnorm(x_ref[...], g_ref[...])
    w = w_ref[...]
    p = _dot(_bf(h), w) if w.dtype == BF16 else _dot3(h, w)
    o1_ref[...] = p[:, :split]
    o2_ref[...] = p[:, split:]


def _norm_matmul(x, gain, w, split, tm):
    n, d = x.shape
    cols = w.shape[1]
    return pl.pallas_call(
        functools.partial(_norm_matmul_kernel, split=split),
        grid=(n // tm,),
        in_specs=[pl.BlockSpec((tm, d), lambda i: (i, 0)),
                  pl.BlockSpec((1, d), lambda i: (0, 0)),
                  pl.BlockSpec((d, cols), lambda i: (0, 0))],
        out_specs=[pl.BlockSpec((tm, split), lambda i: (i, 0)),
                   pl.BlockSpec((tm, cols - split), lambda i: (i, 0))],
        out_shape=[jax.ShapeDtypeStruct((n, split), F32), jax.ShapeDtypeStruct((n, cols - split), F32)],
        compiler_params=_params(("parallel",), 48),
        name="norm_matmul",
    )(x, gain, w)


def _rwkv_prep(p, prev, mu, w0, wd, a0, wa, wg, k_k, k_a, ones_bd):
    c = RWKV_WIDTH
    xm = p + (prev - p) * mu
    r, k, v = xm[:, :c], xm[:, c:2 * c], xm[:, 2 * c:3 * c]
    xw, xa, xg = xm[:, 3 * c:3 * c + 64], xm[:, 3 * c + 64:3 * c + 128], xm[:, 3 * c + 128:]
    z = -(w0 + _dot3(jnp.tanh(xw), wd))
    w = -(jnp.maximum(z, 0.0) + jnp.log(1.0 + jnp.exp(-jnp.abs(z)))) - 0.5
    logw = -jnp.exp(w)
    rate = _sigmoid(a0 + _dot3(xa, wa))
    gate = _dot3(_sigmoid(xg), wg)
    kk = k * k_k
    norm = jnp.sqrt(_dot_sel_rhs(kk * kk, ones_bd))
    kk = kk / jnp.maximum(norm, L2_EPS)
    k2 = k * (1.0 + (rate - 1.0) * k_a)
    return r, logw, k2, v, -kk, kk * rate, gate


def _rwkv_post(y, r, k2, v, gate, r_k, ln_w, ln_b, ones_bd):
    mean = _dot_sel_rhs(y, ones_bd) * (1.0 / HEAD)
    d = y - mean
    var = _dot_sel_rhs(d * d, ones_bd) * (1.0 / HEAD)
    yn = d * lax.rsqrt(var + GN_EPS) * ln_w + ln_b
    bonus = _dot_sel_rhs(r * k2 * r_k, ones_bd) * v
    return (yn + bonus) * gate


def _rwkv_prompt_kernel(p_ref, mu_ref, w0_ref, wd_ref, a0_ref, wa_ref, wg_ref, kk_ref, ka_ref, rk_ref, lnw_ref,
                        lnb_ref, y_ref, s_ref, carry_ref, st_ref, ar_ref, bk_ref, bp_ref, kp_ref, v_ref, yacc_ref,
                        *, tc):
    c = pl.program_id(1)

    @pl.when(c == 0)
    def _init():
        carry_ref[...] = jnp.zeros_like(carry_ref)
        st_ref[...] = jnp.zeros_like(st_ref)

    p = p_ref[0]
    first_row = _iota((tc, 1), 0) == 0
    prev = jnp.where(first_row, carry_ref[0:1, :], pltpu.roll(p, 1, axis=0))
    carry_ref[0:1, :] = p[tc - 1:tc, :]
    ones_bd = _block_ones(RWKV_WIDTH, 6)
    r, logw, k2, v, a, b, gate = _rwkv_prep(p, prev, mu_ref[...], w0_ref[...], wd_ref[...], a0_ref[...],
                                             wa_ref[...], wg_ref[...], kk_ref[...], ka_ref[...], ones_bd)
    ri, ci = _iota((tc, tc), 0), _iota((tc, tc), 1)
    same_chunk = (ri >> 6) == (ci >> 6)
    cum = _dot_sel_lhs(_mask_bf(same_chunk & (ci <= ri)), logw)
    n_ch = tc // CHUNK
    p_end = [jnp.exp(cum[(ci_ + 1) * CHUNK - 1:(ci_ + 1) * CHUNK, :]) for ci_ in range(n_ch)]
    e_inv = jnp.exp(-cum)
    e_end = e_inv * jnp.concatenate([jnp.broadcast_to(pe, (CHUNK, RWKV_WIDTH)) for pe in p_end], axis=0)
    ar_ref[0], ar_ref[1] = _bf(a * jnp.exp(cum - logw)), _bf(r * jnp.exp(cum))
    bk_ref[0], bk_ref[1] = _bf(b * e_inv), _bf(k2 * e_inv)
    bp_ref[...] = _bf(b * e_end)
    kp_ref[...] = _bf(k2 * e_end)
    v_ref[...] = _bf(v)

    rr, cc = _iota((CHUNK, 2 * HEAD), 0), _iota((CHUNK, 2 * HEAD), 1) & (HEAD - 1)
    strict, incl, eye = cc < rr, cc <= rr, cc == rr
    lane = _iota((CHUNK, 2 * HEAD), 1)
    left, right = lane < HEAD, lane >= HEAD
    zero = jnp.zeros((), BF16)

    def stack(x):
        return jnp.concatenate([jnp.where(left, x, zero.astype(x.dtype)), jnp.where(right, x, zero.astype(x.dtype))],
                               axis=0)

    def diag_blocks(x):
        return jnp.where(left, x[:HEAD], x[HEAD:])

    n_ch, n_pairs = tc // CHUNK, N_HEADS // 2
    ids = [(ci_, pr) for ci_ in range(n_ch) for pr in range(n_pairs)]
    rows = lambda ci_: slice(ci_ * CHUNK, (ci_ + 1) * CHUNK)
    cols = lambda pr: slice(pr * 2 * HEAD, (pr + 1) * 2 * HEAD)
    at = {i: ar_ref[0, rows(i[0]), cols(i[1])] for i in ids}
    rt = {i: ar_ref[1, rows(i[0]), cols(i[1])] for i in ids}
    vv = {i: v_ref[rows(i[0]), cols(i[1])] for i in ids}
    bp = {i: bp_ref[rows(i[0]), cols(i[1])] for i in ids}
    g = {i: _dot_nt(jnp.concatenate([at[i], rt[i]], axis=0),
                    jnp.concatenate([stack(bk_ref[0, rows(i[0]), cols(i[1])]),
                                     stack(bk_ref[1, rows(i[0]), cols(i[1])])], axis=0)) for i in ids}
    vk = {i: diag_blocks(_dot_tn(vv[i], kp_ref[rows(i[0]), cols(i[1])])) for i in ids}
    a_ab = {i: jnp.where(strict, g[i][:CHUNK, :2 * HEAD], 0.0) for i in ids}
    a_rb = {i: _bf(jnp.where(incl, g[i][CHUNK:, :2 * HEAD], 0.0)) for i in ids}
    av = {i: _dot(_bf(jnp.concatenate([jnp.where(strict, g[i][:CHUNK, 2 * HEAD:], 0.0),
                                       jnp.where(incl, g[i][CHUNK:, 2 * HEAD:], 0.0)], axis=0)), stack(vv[i]))
          for i in ids}
    t = {i: jnp.where(eye, 1.0, a_ab[i]) for i in ids}
    apow = {i: _bf(a_ab[i]) for i in ids}
    for step in range(5):
        apow = {i: _bf(_dot(apow[i], stack(apow[i]))) for i in ids}
        t = {i: t[i] + _dot(_bf(t[i]), stack(apow[i])) for i in ids}
    wu = {i: _bf(_dot(_bf(t[i]), jnp.concatenate([stack(at[i]), stack(_bf(av[i][:CHUNK]))], axis=1)))
          for i in ids}
    wy_y0 = {i: jnp.concatenate([rt[i].astype(F32), av[i][CHUNK:]], axis=1)
             + _dot(a_rb[i], jnp.concatenate([stack(wu[i][:, :2 * HEAD]), stack(wu[i][:, 2 * HEAD:])], axis=1))
             for i in ids}
    mbn = {i: _dot_tn(wu[i], bp[i]) for i in ids}
    rb, cb = _iota((2 * HEAD, 2 * HEAD), 0) < HEAD, _iota((2 * HEAD, 2 * HEAD), 1) < HEAD
    mb = {i: _bf(jnp.where(rb == cb, mbn[i][:2 * HEAD], 0.0)) for i in ids}
    n0 = {i: vk[i] + diag_blocks(mbn[i][2 * HEAD:]) for i in ids}
    state = [st_ref[:, cols(pr)] for pr in range(n_pairs)]
    for ci_ in range(n_ch):
        sb = [_bf(s) for s in state]
        ys = [wy_y0[(ci_, pr)][:, 2 * HEAD:] + _dot_nt(_bf(wy_y0[(ci_, pr)][:, :2 * HEAD]), stack(sb[pr]))
              for pr in range(n_pairs)]
        state = [state[pr] * p_end[ci_][:, cols(pr)] + _dot(sb[pr], mb[(ci_, pr)])
                 + n0[(ci_, pr)] for pr in range(n_pairs)]
        for pr in range(n_pairs):
            yacc_ref[rows(ci_), cols(pr)] = ys[pr]
    for pr in range(n_pairs):
        st_ref[:, cols(pr)] = state[pr]
    y_ref[0] = _bf(_rwkv_post(yacc_ref[...], r, k2, v, gate, rk_ref[...], lnw_ref[...], lnb_ref[...], ones_bd))
    for h in range(N_HEADS):
        s_ref[0, h] = state[h // 2][:, (h % 2) * HEAD:(h % 2 + 1) * HEAD]


def _rwkv_prompt(p, lp, tc=256):
    bsz, t, _ = p.shape
    row = lambda n: pl.BlockSpec((1, n), lambda b, c: (0, 0))
    mat = lambda m, n: pl.BlockSpec((m, n), lambda b, c: (0, 0))
    tile = lambda dt: pltpu.VMEM((tc, RWKV_WIDTH), dt)
    two = pltpu.VMEM((2, tc, RWKV_WIDTH), BF16)
    return pl.pallas_call(
        functools.partial(_rwkv_prompt_kernel, tc=tc),
        grid=(bsz, t // tc),
        in_specs=[pl.BlockSpec((1, tc, RWKV_COLS), lambda b, c: (b, c, 0)),
                  row(RWKV_COLS), row(512), mat(64, 512), row(512), mat(64, 512), mat(128, 512),
                  row(512), row(512), row(512), row(512), row(512)],
        out_specs=[pl.BlockSpec((1, tc, RWKV_WIDTH), lambda b, c: (b, c, 0)),
                   pl.BlockSpec((1, N_HEADS, HEAD, HEAD), lambda b, c: (b, 0, 0, 0))],
        out_shape=[jax.ShapeDtypeStruct((bsz, t, RWKV_WIDTH), BF16),
                   jax.ShapeDtypeStruct((bsz, N_HEADS, HEAD, HEAD), F32)],
        scratch_shapes=[pltpu.VMEM((8, RWKV_COLS), F32), pltpu.VMEM((HEAD, RWKV_WIDTH), F32), two, two,
                        tile(BF16), tile(BF16), tile(BF16), tile(F32)],
        compiler_params=_params(("arbitrary", "arbitrary"), 48),
        name="rwkv_prompt",
    )(p, lp["mu_shift"], lp["w_decay0"], lp["w_decay_up"], lp["a0"], lp["w_a_up"], lp["w_g_up"], lp["k_k"],
      lp["k_a"], lp["r_k"], lp["ln_x_w"], lp["ln_x_b"])


def _rwkv_sample_prep_kernel(p_ref, prev_ref, mu_ref, w0_ref, wd_ref, a0_ref, wa_ref, wg_ref, kk_ref, ka_ref,
                             r_ref, k_ref, v_ref, g_ref, *t_refs):
    ones_bd = _block_ones(RWKV_WIDTH, 6)
    r, logw, k2, v, a, b, gate = _rwkv_prep(p_ref[...], prev_ref[...], mu_ref[...], w0_ref[...], wd_ref[...],
                                             a0_ref[...], wa_ref[...], wg_ref[...], kk_ref[...], ka_ref[...], ones_bd)
    r_ref[...] = r
    k_ref[...] = k2
    v_ref[...] = v
    g_ref[...] = gate
    for ref, x in zip(t_refs, (r, jnp.exp(logw), k2, v, a, b)):
        ref[...] = x.T


def _rwkv_sample_step_kernel(s_ref, r_ref, w_ref, k_ref, v_ref, a_ref, b_ref, so_ref, y_ref, *, unroll):
    a, w, b, k, r = a_ref[...], w_ref[...], b_ref[...], k_ref[...], r_ref[...]

    def rows(i0, carry):
        base = pl.multiple_of(i0 * unroll, unroll)
        v8 = v_ref[pl.ds(base, unroll), :]
        s = [s_ref[0, base + u] for u in range(unroll)]
        sa = [jnp.sum(s[u] * a, axis=0, keepdims=True) for u in range(unroll)]
        s = [s[u] * w + sa[u] * b + v8[u:u + 1, :] * k for u in range(unroll)]
        for u in range(unroll):
            so_ref[0, base + u] = s[u]
        y_ref[pl.ds(base, unroll), :] = jnp.concatenate(
            [jnp.sum(s[u] * r, axis=0, keepdims=True) for u in range(unroll)], axis=0)
        return carry

    lax.fori_loop(0, HEAD // unroll, rows, 0)


def _rwkv_sample(p, prev, wkv_t, lp, unroll=8):
    n = p.shape[0]
    full = lambda a: pl.BlockSpec(a.shape, lambda i: (0,) * a.ndim)
    names = ("mu_shift", "w_decay0", "w_decay_up", "a0", "w_a_up", "w_g_up", "k_k", "k_a")
    prep_in = [p, prev] + [lp[k] for k in names]
    vecs = pl.pallas_call(
        _rwkv_sample_prep_kernel,
        grid=(1,),
        in_specs=[full(a) for a in prep_in],
        out_specs=[pl.BlockSpec((n, RWKV_WIDTH), lambda i: (0, 0))] * 4
        + [pl.BlockSpec((RWKV_WIDTH, n), lambda i: (0, 0))] * 6,
        out_shape=[jax.ShapeDtypeStruct((n, RWKV_WIDTH), F32)] * 4
        + [jax.ShapeDtypeStruct((RWKV_WIDTH, n), F32)] * 6,
        compiler_params=_params(("arbitrary",), 32),
        name="rwkv_sample_prep",
    )(*prep_in)
    vec_t = pl.BlockSpec((HEAD, n), lambda h: (h, 0))
    st_spec = pl.BlockSpec((1, HEAD, HEAD, n), lambda h: (h, 0, 0, 0))
    wkv_new, y_t = pl.pallas_call(
        functools.partial(_rwkv_sample_step_kernel, unroll=unroll),
        grid=(N_HEADS,),
        in_specs=[st_spec] + [vec_t] * 6,
        out_specs=[st_spec, vec_t],
        out_shape=[jax.ShapeDtypeStruct(wkv_t.shape, F32), jax.ShapeDtypeStruct((RWKV_WIDTH, n), F32)],
        compiler_params=_params(("parallel",), 32),
        name="rwkv_sample_step",
    )(wkv_t, *vecs[4:])
    return vecs[:4], y_t, wkv_new


def _rope_table_kernel(invf_ref, cos_ref, sin_ref, *, rows):
    pos = (pl.program_id(0) * rows + _iota((rows, 1), 0)).astype(F32)
    cos_ref[...], sin_ref[...] = _rope_tables(pos, invf_ref[...])


def _rope_table(t, inv_freq, rows=512):
    spec = pl.BlockSpec((rows, 128), lambda i: (i, 0))
    return pl.pallas_call(
        functools.partial(_rope_table_kernel, rows=rows),
        grid=(t // rows,),
        in_specs=[pl.BlockSpec((1, 128), lambda i: (0, 0))],
        out_specs=[spec, spec],
        out_shape=[jax.ShapeDtypeStruct((t, 128), F32)] * 2,
        compiler_params=_params(("parallel",), 32),
        name="rope_table",
    )(inv_freq)


def _swa_prompt_kernel(p_ref, cos_ref, sin_ref, sink_ref, y_ref, ko_ref, vo_ref, kprev_ref, vprev_ref, *, nb):
    n = pl.program_id(1)
    blk = WINDOW

    @pl.when(n == 0)
    def _init():
        kprev_ref[...] = jnp.zeros_like(kprev_ref)
        vprev_ref[...] = jnp.zeros_like(vprev_ref)

    p = p_ref[0]
    cos, sin_signed = cos_ref[...], sin_ref[...]
    q = _rope(p[:, :RWKV_WIDTH], cos, sin_signed)
    k = _rope(p[:, RWKV_WIDTH:RWKV_WIDTH + KV_WIDTH], cos, sin_signed)
    v = p[:, RWKV_WIDTH + KV_WIDTH:]
    kcat = jnp.concatenate([kprev_ref[...], k], axis=0)
    vcat = jnp.concatenate([vprev_ref[...], v], axis=0)
    ii, jj = _iota((blk, 2 * blk), 0), _iota((blk, 2 * blk), 1)
    in_window = (jj > ii) & (jj <= ii + blk)
    bias = jnp.concatenate([jnp.where(in_window, 0.0, NEG)] * 4, axis=0)
    lo = jnp.where(n == 0, blk, 0)
    bias_first = jnp.concatenate([jnp.where(in_window & (jj >= lo), 0.0, NEG)] * 4, axis=0)
    units = [(j, g) for j in range(nb) for g in range(2)]
    rows = lambda j, m: slice(j * blk, (j + m) * blk)
    heads = lambda g: range(4 * g, 4 * g + 4)
    cols = lambda g: slice(g * HEAD, (g + 1) * HEAD)
    sink_col = [jnp.concatenate([jnp.broadcast_to(sink_ref[0:1, h:h + 1], (blk, 1)) for h in heads(g)], axis=0)
                for g in range(2)]
    qg = {(j, g): _bf(jnp.concatenate([q[rows(j, 1), h * HEAD:(h + 1) * HEAD] for h in heads(g)], axis=0))
          for j, g in units}
    s = {(j, g): _dot_nt(qg[(j, g)], _bf(kcat[rows(j, 2), cols(g)])) * (HEAD ** -0.5)
         + (bias_first if j == 0 else bias) for j, g in units}
    pr = {(j, g): _bf(_softmax_sink(s[(j, g)], sink_col[g])) for j, g in units}
    o = {(j, g): _dot(pr[(j, g)], _bf(vcat[rows(j, 2), cols(g)])) for j, g in units}
    y_ref[0] = _bf(jnp.concatenate(
        [jnp.concatenate([o[(j, g)][i * blk:(i + 1) * blk] for g in range(2) for i in range(4)], axis=1)
         for j in range(nb)], axis=0))
    k_last, v_last = k[(nb - 1) * blk:], v[(nb - 1) * blk:]
    ko_ref[0] = k_last
    vo_ref[0] = v_last
    kprev_ref[...] = k_last
    vprev_ref[...] = v_last


def _swa_prompt(p, inv_freq, sink, nb=4):
    bsz, t, _ = p.shape
    blk = WINDOW
    rows = nb * blk
    cos, sin_signed = _rope_table(t, inv_freq, rows)
    table = pl.BlockSpec((rows, 128), lambda b, n: (n, 0))
    return pl.pallas_call(
        functools.partial(_swa_prompt_kernel, nb=nb),
        grid=(bsz, t // rows),
        in_specs=[pl.BlockSpec((1, rows, SWA_COLS), lambda b, n: (b, n, 0)), table, table,
                  pl.BlockSpec((1, N_HEADS), lambda b, n: (0, 0))],
        out_specs=[pl.BlockSpec((1, rows, RWKV_WIDTH), lambda b, n: (b, n, 0)),
                   pl.BlockSpec((1, blk, KV_WIDTH), lambda b, n: (b, 0, 0)),
                   pl.BlockSpec((1, blk, KV_WIDTH), lambda b, n: (b, 0, 0))],
        out_shape=[jax.ShapeDtypeStruct((bsz, t, RWKV_WIDTH), BF16),
                   jax.ShapeDtypeStruct((bsz, blk, KV_WIDTH), F32),
                   jax.ShapeDtypeStruct((bsz, blk, KV_WIDTH), F32)],
        scratch_shapes=[pltpu.VMEM((blk, KV_WIDTH), F32), pltpu.VMEM((blk, KV_WIDTH), F32)],
        compiler_params=_params(("arbitrary", "arbitrary"), 48),
        name="swa_prompt",
    )(p, cos, sin_signed, sink)


def _swa_sample_kernel(p_ref, ck_ref, cv_ref, invf_ref, sinkc_ref, y_ref, nk_ref, nv_ref, *, bt):
    p = p_ref[...]
    cos, sin_signed = _rope_tables(jnp.full((1, 1), float(PAST_LEN), F32), invf_ref[...])
    q = _rope(p[:, :RWKV_WIDTH], cos, sin_signed) * (HEAD ** -0.5)
    k = _rope(p[:, RWKV_WIDTH:RWKV_WIDTH + KV_WIDTH], cos, sin_signed)
    v = p[:, RWKV_WIDTH + KV_WIDTH:]
    k_t, v_t = k.T, v.T
    last = _iota((KV_WIDTH, WINDOW), 1) == WINDOW - 1
    head_mask = _iota((N_HEADS, RWKV_WIDTH), 0) == (_iota((N_HEADS, RWKV_WIDTH), 1) >> 6)
    tile_lanes = lambda x: jnp.concatenate([x[:, :HEAD]] * 4 + [x[:, HEAD:]] * 4, axis=1)
    tile_rows = lambda x: jnp.concatenate([x[:HEAD]] * 4 + [x[HEAD:]] * 4, axis=0)
    seqs = range(bt)
    k_new = [jnp.where(last, k_t[:, bi:bi + 1], pltpu.roll(ck_ref[bi], WINDOW - 1, axis=1)) for bi in seqs]
    v_new = [jnp.where(last, v_t[:, bi:bi + 1], pltpu.roll(cv_ref[bi], WINDOW - 1, axis=1)) for bi in seqs]
    for bi in seqs:
        nk_ref[bi] = k_new[bi]
        nv_ref[bi] = v_new[bi]
    q_bd = [jnp.where(head_mask, q[bi:bi + 1, :], 0.0) for bi in seqs]
    s = [_dot3(q_bd[bi], tile_rows(k_new[bi])) for bi in seqs]
    pr = [_softmax_sink(s[bi], sinkc_ref[...]) for bi in seqs]
    o = [_dot3_nt(pr[bi], v_new[bi]) for bi in seqs]
    y_ref[...] = jnp.concatenate(
        [jnp.sum(jnp.where(head_mask, tile_lanes(o[bi]), 0.0), axis=0, keepdims=True) for bi in seqs], axis=0)


def _swa_sample(p, ck, cv, inv_freq, sink_col, bt=8):
    n = p.shape[0]
    cache_spec = pl.BlockSpec((bt, KV_WIDTH, WINDOW), lambda i: (i, 0, 0))
    return pl.pallas_call(
        functools.partial(_swa_sample_kernel, bt=bt),
        grid=(n // bt,),
        in_specs=[pl.BlockSpec((bt, SWA_COLS), lambda i: (i, 0)), cache_spec, cache_spec,
                  pl.BlockSpec((1, 128), lambda i: (0, 0)), pl.BlockSpec((N_HEADS, 1), lambda i: (0, 0))],
        out_specs=[pl.BlockSpec((bt, RWKV_WIDTH), lambda i: (i, 0)), cache_spec, cache_spec],
        out_shape=[jax.ShapeDtypeStruct((n, RWKV_WIDTH), F32), jax.ShapeDtypeStruct(ck.shape, F32),
                   jax.ShapeDtypeStruct(cv.shape, F32)],
        compiler_params=_params(("parallel",), 32),
        name="swa_sample",
    )(p, ck, cv, inv_freq, sink_col)


def _mix_out(yr, ys, x, wo1, wo2):
    return x + _dot(yr, wo1) + _dot(ys, wo2)


def _out_cross_kernel(yr_ref, ys_ref, x_ref, wo1_ref, wo2_ref, gc_ref, wq_ref, mk_ref, mv_ref, wco_ref, o_ref):
    x1 = _mix_out(yr_ref[0], ys_ref[0], x_ref[0], wo1_ref[...], wo2_ref[...])
    q = _bf(_dot(_bf(_rmsnorm(x1, gc_ref[...])), wq_ref[...]))
    mk, mv = mk_ref[0], mv_ref[0]
    outs = []
    for h in range(X_HEADS):
        sl = slice(h * X_HEAD, (h + 1) * X_HEAD)
        s = _dot_nt(q[:, sl], mk[:, sl]) * (X_HEAD ** -0.5)
        e = jnp.exp(s - jnp.max(s, axis=-1, keepdims=True))
        pr = e / jnp.sum(e, axis=-1, keepdims=True)
        outs.append(_dot(_bf(pr), mv[:, sl]))
    o_ref[0] = x1 + _dot(_bf(jnp.concatenate(outs, axis=1)), wco_ref[...])


def _out_cross(yr, ys, x, wo1, wo2, gain, wq, mk, mv, wco, tm=512):
    bsz, t, d = x.shape
    tile = lambda w: pl.BlockSpec((1, tm, w), lambda b, i: (b, i, 0))
    const = lambda a: pl.BlockSpec(a.shape, lambda b, i: (0,) * a.ndim)
    mem = pl.BlockSpec((1, MEM_TOKENS, X_WIDTH), lambda b, i: (b, 0, 0))
    return pl.pallas_call(
        _out_cross_kernel,
        grid=(bsz, t // tm),
        in_specs=[tile(RWKV_WIDTH), tile(RWKV_WIDTH), tile(d), const(wo1), const(wo2), const(gain), const(wq),
                  mem, mem, const(wco)],
        out_specs=tile(d),
        out_shape=jax.ShapeDtypeStruct((bsz, t, d), F32),
        compiler_params=_params(("parallel", "parallel"), 48),
        name="out_cross",
    )(yr, ys, x, wo1, wo2, gain, wq, mk, mv, wco)


def _out_q_kernel(yt_ref, r_ref, k_ref, v_ref, g_ref, rk_ref, lnw_ref, lnb_ref, ys_ref, x_ref, wo1_ref, wo2_ref,
                  gc_ref, wq_ref, x1_ref, q_ref):
    yr = _rwkv_post(yt_ref[...].T, r_ref[...], k_ref[...], v_ref[...], g_ref[...], rk_ref[...], lnw_ref[...],
                    lnb_ref[...], _block_ones(RWKV_WIDTH, 6))
    x1 = x_ref[...] + _dot3(yr, wo1_ref[...]) + _dot3(ys_ref[...], wo2_ref[...])
    x1_ref[...] = x1
    q_ref[...] = _dot3(_rmsnorm(x1, gc_ref[...]), wq_ref[...])


def _cross_sample_kernel(q_ref, mk_ref, mv_ref, o_ref, *, bt):
    seqs = range(bt)
    both = lambda x: x + pltpu.roll(x, X_HEADS, axis=0)
    q8 = [jnp.concatenate([q_ref[bi]] * 2, axis=0) for bi in seqs]
    s = [jnp.sum(mk_ref[bi] * q8[bi][None], axis=-1, keepdims=True) * (X_HEAD ** -0.5) for bi in seqs]
    mx = [jnp.max(s[bi], axis=0) for bi in seqs]
    mx = [jnp.maximum(m, pltpu.roll(m, X_HEADS, axis=0)) for m in mx]
    e = [jnp.exp(s[bi] - mx[bi][None]) for bi in seqs]
    den = [both(jnp.sum(e[bi], axis=0)) for bi in seqs]
    o8 = [jnp.sum(e[bi] * mv_ref[bi], axis=0) / den[bi] for bi in seqs]
    for bi in seqs:
        o_ref[bi] = both(o8[bi])[:X_HEADS]


def _lin_res_kernel(x_ref, a_ref, w_ref, o_ref):
    o_ref[...] = x_ref[...] + _dot3(a_ref[...], w_ref[...])


def _single_step(kernel_fn, out_shape, name, *args):
    full = lambda a: pl.BlockSpec(a.shape, lambda i: (0,) * len(a.shape))
    outs = out_shape if isinstance(out_shape, (list, tuple)) else [out_shape]
    out_specs = [full(o) for o in outs]
    return pl.pallas_call(
        kernel_fn, grid=(1,), in_specs=[full(a) for a in args],
        out_specs=out_specs if isinstance(out_shape, (list, tuple)) else out_specs[0],
        out_shape=out_shape, compiler_params=_params(("arbitrary",), 32), name=name,
    )(*args)


def _out_cross_sample(rwkv_parts, ys, x, wo1, wo2, gain, wq, mk, mv, wco, bt=4):
    n, d = x.shape
    x1, q = _single_step(_out_q_kernel, [jax.ShapeDtypeStruct((n, d), F32), jax.ShapeDtypeStruct((n, X_WIDTH), F32)],
                         "out_q_sample", *rwkv_parts, ys, x, wo1, wo2, gain, wq)
    mem_shape = (n, MEM_TOKENS // 2, 2 * X_HEADS, X_HEAD)
    mem = pl.BlockSpec((bt,) + mem_shape[1:], lambda i: (i, 0, 0, 0))
    vec = pl.BlockSpec((bt, X_HEADS, X_HEAD), lambda i: (i, 0, 0))
    o = pl.pallas_call(
        functools.partial(_cross_sample_kernel, bt=bt),
        grid=(n // bt,),
        in_specs=[vec, mem, mem],
        out_specs=vec,
        out_shape=jax.ShapeDtypeStruct((n, X_HEADS, X_HEAD), F32),
        compiler_params=_params(("parallel",), 48),
        name="cross_sample",
    )(q.reshape(n, X_HEADS, X_HEAD), mk.reshape(mem_shape), mv.reshape(mem_shape))
    return _single_step(_lin_res_kernel, jax.ShapeDtypeStruct((n, d), F32), "cross_out_sample", x1,
                        o.reshape(n, X_WIDTH), wco)


def _route(logits):
    lane_i = _iota(logits.shape, 1)
    lane = lane_i.astype(F32)
    big = 1024.0
    n_exp = float(N_GROUPS * EXPERTS_PER_GROUP)
    lg = jnp.where((lane >= n_exp) & (lane < n_exp + N_GROUPS), logits, NEG)
    g_max = jnp.max(lg, axis=-1, keepdims=True)
    g_idx = jnp.min(jnp.where(lg == g_max, lane, big), axis=-1, keepdims=True) - n_exp
    p_sel = 1.0 / jnp.sum(jnp.exp(lg - g_max), axis=-1, keepdims=True)
    le = jnp.where((lane_i >> 3).astype(F32) == g_idx, logits, NEG)
    t1 = jnp.max(le, axis=-1, keepdims=True)
    i1 = jnp.min(jnp.where(le == t1, lane, big), axis=-1, keepdims=True)
    le2 = jnp.where(lane == i1, NEG, le)
    t2 = jnp.max(le2, axis=-1, keepdims=True)
    i2 = jnp.min(jnp.where(le2 == t2, lane, big), axis=-1, keepdims=True)
    e2 = jnp.exp(t2 - t1)
    w1 = 1.0 / (1.0 + e2)
    return p_sel * (jnp.where(lane == i1, w1, 0.0) + jnp.where(lane == i2, e2 * w1, 0.0))


def _moe_kernel(x_ref, gf_ref, wr_ref, br_ref, wg_ref, wu_ref, wd_ref, gfin_ref, o_ref, h_ref, gate_ref, acc_ref):
    g = pl.program_id(1)

    @pl.when(g == 0)
    def _init():
        h = _rmsnorm(x_ref[...], gf_ref[...])
        h_ref[...] = _bf(h)
        gate_ref[...] = _route(_dot3(h, wr_ref[...]) + br_ref[...])
        acc_ref[...] = jnp.zeros_like(acc_ref)

    hb = h_ref[...]
    hg = _dot(hb, wg_ref[0])
    hu = _dot(hb, wu_ref[0])
    width = EXPERTS_PER_GROUP * EXPERT_HIDDEN
    expand = _mask_bf(_iota((128, width), 0) == g * EXPERTS_PER_GROUP + (_iota((128, width), 1) >> 7))
    act = hg * _sigmoid(hg) * hu * _dot_sel_rhs(gate_ref[...], expand)
    acc_ref[...] += _dot(_bf(act), wd_ref[0])

    @pl.when(g == N_GROUPS - 1)
    def _fin():
        o_ref[...] = _rmsnorm(x_ref[...] + acc_ref[...], gfin_ref[...])


def _moe(x, gain, w_router, b_router, wg, wu, wd, gain_final, tm):
    n, d = x.shape
    width = EXPERTS_PER_GROUP * EXPERT_HIDDEN
    const = lambda a: pl.BlockSpec(a.shape, lambda i, g: (0,) * a.ndim)
    return pl.pallas_call(
        _moe_kernel,
        grid=(n // tm, N_GROUPS),
        in_specs=[pl.BlockSpec((tm, d), lambda i, g: (i, 0)), const(gain), const(w_router), const(b_router),
                  pl.BlockSpec((1, d, width), lambda i, g: (g, 0, 0)),
                  pl.BlockSpec((1, d, width), lambda i, g: (g, 0, 0)),
                  pl.BlockSpec((1, width, d), lambda i, g: (g, 0, 0)), const(gain_final)],
        out_specs=pl.BlockSpec((tm, d), lambda i, g: (i, 0)),
        out_shape=jax.ShapeDtypeStruct((n, d), F32),
        scratch_shapes=[pltpu.VMEM((tm, d), BF16), pltpu.VMEM((tm, 128), F32), pltpu.VMEM((tm, d), F32)],
        compiler_params=_params(("parallel", "arbitrary"), 56),
        name="moe",
    )(x, gain, w_router, b_router, wg, wu, wd, gain_final)


def kernel(x_prompt, x_sample, mem_prompt, state_rwkv_shift, state_rwkv_wkv, cache_swa_k, cache_swa_v, cache_mem_k, cache_mem_v, norm_mix, w_in, mu_shift, w_decay0, w_decay_up, a0, w_a_up, w_g_up, k_k, k_a, r_k, ln_x_w, ln_x_b, attn_sink, w_out, norm_cross, norm_mem, w_cq, w_ck, w_cv, w_co, norm_ffn, w_group_router, b_group_router, w_expert_router, b_expert_router, w_exp_gate, w_exp_up, w_exp_down, norm_final):
    bsz, t, d = x_prompt.shape
    n_s = x_sample.shape[0]
    wb = cache_swa_k.shape[2]
    width = EXPERTS_PER_GROUP * EXPERT_HIDDEN

    w_in_bf = w_in[0].astype(BF16)
    wo1, wo2 = w_out[0, :RWKV_WIDTH].astype(BF16), w_out[0, RWKV_WIDTH:].astype(BF16)
    wq, wco = w_cq[0].astype(BF16), w_co[0].astype(BF16)
    w_ckv = jnp.concatenate([w_ck[0], w_cv[0]], axis=1).astype(BF16)
    w_router = jnp.pad(jnp.concatenate([w_expert_router[0], w_group_router[0]], axis=1), ((0, 0), (0, 128 - 36)))
    b_router = jnp.pad(jnp.concatenate([b_expert_router[0], b_group_router[0]]), (0, 128 - 36)).reshape(1, 128)
    wg = jnp.transpose(w_exp_gate[0], (0, 2, 1, 3)).reshape(N_GROUPS, d, width).astype(BF16)
    wu = jnp.transpose(w_exp_up[0], (0, 2, 1, 3)).reshape(N_GROUPS, d, width).astype(BF16)
    wd = w_exp_down[0].reshape(N_GROUPS, width, d).astype(BF16)
    row = lambda a: a.reshape(1, -1)
    lp = dict(mu_shift=mu_shift, w_decay0=w_decay0, w_decay_up=w_decay_up[0], a0=a0, w_a_up=w_a_up[0],
              w_g_up=w_g_up[0], k_k=k_k, k_a=k_a, r_k=row(r_k), ln_x_w=ln_x_w, ln_x_b=ln_x_b)
    half = HEAD // 2
    inv_freq = ROPE_THETA ** (-jnp.arange(half, dtype=F32) * 2.0 / HEAD)
    inv_freq = jnp.tile(inv_freq, 128 // half).reshape(1, 128)

    p_rwkv, p_swa = _norm_matmul(x_prompt.reshape(bsz * t, d), norm_mix, w_in_bf, RWKV_COLS, 512)
    p_rwkv = p_rwkv.reshape(bsz, t, RWKV_COLS)
    y_rwkv, wkv_p = _rwkv_prompt(p_rwkv, lp)
    y_swa, k_p, v_p = _swa_prompt(p_swa.reshape(bsz, t, SWA_COLS), inv_freq, attn_sink)
    mk_p, mv_p = _norm_matmul(mem_prompt.reshape(bsz * MEM_TOKENS, d), norm_mem, w_ckv, X_WIDTH, 256)
    mk_b = mk_p.astype(BF16).reshape(bsz, MEM_TOKENS, X_WIDTH)
    mv_b = mv_p.astype(BF16).reshape(bsz, MEM_TOKENS, X_WIDTH)
    x2 = _out_cross(y_rwkv, y_swa, x_prompt, wo1, wo2, norm_cross, wq, mk_b, mv_b, wco)
    y_prompt = _moe(x2.reshape(bsz * t, d), norm_ffn, w_router, b_router, wg, wu, wd, row(norm_final), 512)

    xs = x_sample.reshape(n_s, d)
    ps_rwkv, ps_swa = _norm_matmul(xs, norm_mix, w_in[0], RWKV_COLS, n_s)
    (r_s, k2_s, v_s_tok, g_s), yt_s, wkv_s = _rwkv_sample(ps_rwkv, state_rwkv_shift[0],
                                                           jnp.transpose(state_rwkv_wkv[0], (1, 2, 3, 0)), lp)
    cache_t = lambda c: jnp.transpose(c[0].reshape(n_s, wb, KV_WIDTH), (0, 2, 1))
    ys_swa, k_s, v_s = _swa_sample(ps_swa, cache_t(cache_swa_k), cache_t(cache_swa_v), inv_freq,
                                   attn_sink.reshape(N_HEADS, 1))
    x2s = _out_cross_sample((yt_s, r_s, k2_s, v_s_tok, g_s, lp["r_k"], lp["ln_x_w"], lp["ln_x_b"]), ys_swa, xs,
                            w_out[0, :RWKV_WIDTH], w_out[0, RWKV_WIDTH:], norm_cross, w_cq[0],
                            cache_mem_k[0], cache_mem_v[0], w_co[0])
    y_sample = _moe(x2s, norm_ffn, w_router, b_router, wg, wu, wd, row(norm_final), n_s)

    kv_shape = (1, -1, wb, KV_WIDTH // HEAD, HEAD)
    mem_shape = (1, bsz, MEM_TOKENS, X_HEADS, X_HEAD)
    cache_back = lambda c: jnp.transpose(c, (0, 2, 1)).reshape(kv_shape)
    return (y_prompt.reshape(bsz, t, d), y_sample.reshape(n_s, 1, d),
            p_rwkv[:, -1][None], wkv_p[None], k_p.reshape(kv_shape), v_p.reshape(kv_shape),
            mk_p.reshape(mem_shape), mv_p.reshape(mem_shape),
            ps_rwkv[None], jnp.transpose(wkv_s, (3, 0, 1, 2))[None], cache_back(k_s), cache_back(v_s))
```

```python
import functools

import jax
import jax.numpy as jnp
from jax import lax
from jax.experimental import pallas as pl
from jax.experimental.pallas import tpu as pltpu

F32 = jnp.float32
BF16 = jnp.bfloat16

D_MODEL = 1024
RWKV_WIDTH = 512
HEAD = 64
N_HEADS = 8
RWKV_COLS = 1792
SWA_COLS = 768
KV_WIDTH = 128
WINDOW = 128
PAST_LEN = 8192
ROPE_THETA = 10000.0
MEM_TOKENS = 256
X_HEADS = 4
X_HEAD = 128
X_WIDTH = 512
N_GROUPS = 4
EXPERTS_PER_GROUP = 8
EXPERT_HIDDEN = 128
NORM_EPS = 1e-5
GN_EPS = 64e-5
L2_EPS = 1e-12
CHUNK = 64
NEG = -1e30
MIB = 1024 * 1024


def _params(semantics, vmem_mib):
    return pltpu.CompilerParams(dimension_semantics=semantics, vmem_limit_bytes=vmem_mib * MIB)


def _bf(x):
    return x.astype(BF16)


def _dot(a, b):
    return jnp.dot(a, b, preferred_element_type=F32)


def _dot_nt(a, b):
    return lax.dot_general(a, b, (((1,), (1,)), ((), ())), preferred_element_type=F32)


def _dot_tn(a, b):
    return lax.dot_general(a, b, (((0,), (0,)), ((), ())), preferred_element_type=F32)


def _split(x):
    hi = x.astype(BF16)
    lo = (x - hi.astype(F32)).astype(BF16)
    return hi, lo


def _dot3(a, b, dot=_dot):
    ah, al = _split(a)
    bh, bl = _split(b)
    return dot(ah, bh) + (dot(ah, bl) + dot(al, bh))


def _dot3_nt(a, b):
    return _dot3(a, b, _dot_nt)


def _dot_sel_rhs(a, sel):
    ah, al = _split(a)
    return _dot(ah, sel) + _dot(al, sel)


def _dot_sel_lhs(sel, b):
    bh, bl = _split(b)
    return _dot(sel, bh) + _dot(sel, bl)


def _rmsnorm(x, gain):
    return x * lax.rsqrt(jnp.mean(x * x, axis=-1, keepdims=True) + NORM_EPS) * gain


def _sigmoid(x):
    return 1.0 / (1.0 + jnp.exp(-x))


def _iota(shape, dim):
    return lax.broadcasted_iota(jnp.int32, shape, dim)


def _mask_bf(cond):
    return jnp.where(cond, 1.0, 0.0).astype(BF16)


def _block_ones(n, blk_shift):
    return _mask_bf((_iota((n, n), 0) >> blk_shift) == (_iota((n, n), 1) >> blk_shift))


def _softmax_sink(s, sink_col):
    m = jnp.maximum(jnp.max(s, axis=-1, keepdims=True), sink_col)
    e = jnp.exp(s - m)
    den = jnp.sum(e, axis=-1, keepdims=True) + jnp.exp(sink_col - m)
    return e / den


def _rope(x, cos, sin_signed):
    width = x.shape[1]
    reps = width // 128
    if reps > 1:
        cos = jnp.concatenate([cos] * reps, axis=1)
        sin_signed = jnp.concatenate([sin_signed] * reps, axis=1)
    first_half = (_iota(x.shape, 1) & (HEAD - 1)) < HEAD // 2
    rot = jnp.where(first_half, pltpu.roll(x, width - HEAD // 2, axis=1), pltpu.roll(x, HEAD // 2, axis=1))
    return x * cos + rot * sin_signed


def _rope_tables(pos, inv_freq):
    ang = pos * inv_freq
    cos, sin = jnp.cos(ang), jnp.sin(ang)
    first_half = (_iota(ang.shape, 1) & (HEAD - 1)) < HEAD // 2
    return cos, jnp.where(first_half, -sin, sin)


def _norm_matmul_kernel(x_ref, g_ref, w_ref, o1_ref, o2_ref, *, split):
    h = _rmsnorm(x_ref[...], g_ref[...])
    w = w_ref[...]
    p = _dot(_bf(h), w) if w.dtype == BF16 else _dot3(h, w)
    o1_ref[...] = p[:, :split]
    o2_ref[...] = p[:, split:]


def _norm_matmul(x, gain, w, split, tm):
    n, d = x.shape
    cols = w.shape[1]
    return pl.pallas_call(
        functools.partial(_norm_matmul_kernel, split=split),
        grid=(n // tm,),
        in_specs=[pl.BlockSpec((tm, d), lambda i: (i, 0)),
                  pl.BlockSpec((1, d), lambda i: (0, 0)),
                  pl.BlockSpec((d, cols), lambda i: (0, 0))],
        out_specs=[pl.BlockSpec((tm, split), lambda i: (i, 0)),
                   pl.BlockSpec((tm, cols - split), lambda i: (i, 0))],
        out_shape=[jax.ShapeDtypeStruct((n, split), F32), jax.ShapeDtypeStruct((n, cols - split), F32)],
        compiler_params=_params(("parallel",), 48),
        name="norm_matmul",
    )(x, gain, w)


def _dot1(a, b):
    return _dot(_bf(a), _bf(b))


def _rwkv_prep(p, prev, mu, w0, wd, a0, wa, wg, k_k, k_a, ones_bd, precise):
    mm, head_sum = (_dot3, _dot_sel_rhs) if precise else (_dot1, _dot1)
    c = RWKV_WIDTH
    xm = p + (prev - p) * mu
    r, k, v = xm[:, :c], xm[:, c:2 * c], xm[:, 2 * c:3 * c]
    xw, xa, xg = xm[:, 3 * c:3 * c + 64], xm[:, 3 * c + 64:3 * c + 128], xm[:, 3 * c + 128:]
    z = -(w0 + _dot3(jnp.tanh(xw), wd))
    w = -(jnp.maximum(z, 0.0) + jnp.log(1.0 + jnp.exp(-jnp.abs(z)))) - 0.5
    logw = -jnp.exp(w)
    rate = _sigmoid(a0 + mm(xa, wa))
    gate = mm(_sigmoid(xg), wg)
    kk = k * k_k
    kk = kk * lax.rsqrt(jnp.maximum(head_sum(kk * kk, ones_bd), L2_EPS * L2_EPS))
    k2 = k * (1.0 + (rate - 1.0) * k_a)
    return r, logw, k2, v, -kk, kk * rate, gate


def _rwkv_post(y, r, k2, v, gate, r_k, ln_w, ln_b, ones_bd, precise):
    head_sum = _dot_sel_rhs if precise else _dot1
    mean = head_sum(y, ones_bd) * (1.0 / HEAD)
    d = y - mean
    var = head_sum(d * d, ones_bd) * (1.0 / HEAD)
    yn = d * lax.rsqrt(var + GN_EPS) * ln_w + ln_b
    bonus = head_sum(r * k2 * r_k, ones_bd) * v
    return (yn + bonus) * gate


def _rwkv_prompt_kernel(p_ref, mu_ref, w0_ref, wd_ref, a0_ref, wa_ref, wg_ref, kk_ref, ka_ref, rk_ref, lnw_ref,
                        lnb_ref, y_ref, s_ref, carry_ref, st_ref, ar_ref, bk_ref, bp_ref, kp_ref, v_ref, yacc_ref,
                        *, tc):
    c = pl.program_id(1)

    @pl.when(c == 0)
    def _init():
        carry_ref[...] = jnp.zeros_like(carry_ref)
        st_ref[...] = jnp.zeros_like(st_ref)

    p = p_ref[0]
    first_row = _iota((tc, 1), 0) == 0
    prev = jnp.where(first_row, carry_ref[0:1, :], pltpu.roll(p, 1, axis=0))
    carry_ref[0:1, :] = p[tc - 1:tc, :]
    ones_bd = _block_ones(RWKV_WIDTH, 6)
    r, logw, k2, v, a, b, gate = _rwkv_prep(p, prev, mu_ref[...], w0_ref[...], wd_ref[...], a0_ref[...],
                                             wa_ref[...], wg_ref[...], kk_ref[...], ka_ref[...], ones_bd, False)
    ri, ci = _iota((tc, tc), 0), _iota((tc, tc), 1)
    same_chunk = (ri >> 6) == (ci >> 6)
    cum = _dot_sel_lhs(_mask_bf(same_chunk & (ci <= ri)), logw)
    n_ch = tc // CHUNK
    p_end = [jnp.exp(cum[(ci_ + 1) * CHUNK - 1:(ci_ + 1) * CHUNK, :]) for ci_ in range(n_ch)]
    e_inv = jnp.exp(-cum)
    e_end = e_inv * jnp.concatenate([jnp.broadcast_to(pe, (CHUNK, RWKV_WIDTH)) for pe in p_end], axis=0)
    ar_ref[0], ar_ref[1] = _bf(a * jnp.exp(cum - logw)), _bf(r * jnp.exp(cum))
    bk_ref[0], bk_ref[1] = _bf(b * e_inv), _bf(k2 * e_inv)
    bp_ref[...] = _bf(b * e_end)
    kp_ref[...] = _bf(k2 * e_end)
    v_ref[...] = _bf(v)

    rr, cc = _iota((CHUNK, 2 * HEAD), 0), _iota((CHUNK, 2 * HEAD), 1) & (HEAD - 1)
    strict, incl, eye = cc < rr, cc <= rr, cc == rr
    lane = _iota((CHUNK, 2 * HEAD), 1)
    left, right = lane < HEAD, lane >= HEAD
    zero = jnp.zeros((), BF16)

    def stack(x):
        return jnp.concatenate([jnp.where(left, x, zero.astype(x.dtype)), jnp.where(right, x, zero.astype(x.dtype))],
                               axis=0)

    def diag_blocks(x):
        return jnp.where(left, x[:HEAD], x[HEAD:])

    n_ch, n_pairs = tc // CHUNK, N_HEADS // 2
    ids = [(ci_, pr) for ci_ in range(n_ch) for pr in range(n_pairs)]
    rows = lambda ci_: slice(ci_ * CHUNK, (ci_ + 1) * CHUNK)
    cols = lambda pr: slice(pr * 2 * HEAD, (pr + 1) * 2 * HEAD)
    at = {i: ar_ref[0, rows(i[0]), cols(i[1])] for i in ids}
    rt = {i: ar_ref[1, rows(i[0]), cols(i[1])] for i in ids}
    vv = {i: v_ref[rows(i[0]), cols(i[1])] for i in ids}
    bp = {i: bp_ref[rows(i[0]), cols(i[1])] for i in ids}
    g = {i: _dot_nt(jnp.concatenate([at[i], rt[i]], axis=0),
                    jnp.concatenate([stack(bk_ref[0, rows(i[0]), cols(i[1])]),
                                     stack(bk_ref[1, rows(i[0]), cols(i[1])])], axis=0)) for i in ids}
    vk = {i: diag_blocks(_dot_tn(vv[i], kp_ref[rows(i[0]), cols(i[1])])) for i in ids}
    a_ab = {i: jnp.where(strict, g[i][:CHUNK, :2 * HEAD], 0.0) for i in ids}
    a_rb = {i: _bf(jnp.where(incl, g[i][CHUNK:, :2 * HEAD], 0.0)) for i in ids}
    av = {i: _dot(_bf(jnp.concatenate([jnp.where(strict, g[i][:CHUNK, 2 * HEAD:], 0.0),
                                       jnp.where(incl, g[i][CHUNK:, 2 * HEAD:], 0.0)], axis=0)), stack(vv[i]))
          for i in ids}
    t = {i: jnp.where(eye, 1.0, a_ab[i]) for i in ids}
    apow = {i: _bf(a_ab[i]) for i in ids}
    for step in range(5):
        apow = {i: _bf(_dot(apow[i], stack(apow[i]))) for i in ids}
        t = {i: t[i] + _dot(_bf(t[i]), stack(apow[i])) for i in ids}
    wu = {i: _bf(_dot(_bf(t[i]), jnp.concatenate([stack(at[i]), stack(_bf(av[i][:CHUNK]))], axis=1)))
          for i in ids}
    wy_y0 = {i: jnp.concatenate([rt[i].astype(F32), av[i][CHUNK:]], axis=1)
             + _dot(a_rb[i], jnp.concatenate([stack(wu[i][:, :2 * HEAD]), stack(wu[i][:, 2 * HEAD:])], axis=1))
             for i in ids}
    mbn = {i: _dot_tn(wu[i], bp[i]) for i in ids}
    rb, cb = _iota((2 * HEAD, 2 * HEAD), 0) < HEAD, _iota((2 * HEAD, 2 * HEAD), 1) < HEAD
    mb = {i: _bf(jnp.where(rb == cb, mbn[i][:2 * HEAD], 0.0)) for i in ids}
    n0 = {i: vk[i] + diag_blocks(mbn[i][2 * HEAD:]) for i in ids}
    state = [st_ref[:, cols(pr)] for pr in range(n_pairs)]
    for ci_ in range(n_ch):
        sb = [_bf(s) for s in state]
        ys = [wy_y0[(ci_, pr)][:, 2 * HEAD:] + _dot_nt(_bf(wy_y0[(ci_, pr)][:, :2 * HEAD]), stack(sb[pr]))
              for pr in range(n_pairs)]
        state = [state[pr] * p_end[ci_][:, cols(pr)] + _dot(sb[pr], mb[(ci_, pr)])
                 + n0[(ci_, pr)] for pr in range(n_pairs)]
        for pr in range(n_pairs):
            yacc_ref[rows(ci_), cols(pr)] = ys[pr]
    for pr in range(n_pairs):
        st_ref[:, cols(pr)] = state[pr]
    y_ref[0] = _bf(_rwkv_post(yacc_ref[...], r, k2, v, gate, rk_ref[...], lnw_ref[...], lnb_ref[...], ones_bd,
                              False))
    for h in range(N_HEADS):
        s_ref[0, h] = state[h // 2][:, (h % 2) * HEAD:(h % 2 + 1) * HEAD]


def _rwkv_prompt(p, lp, tc=256):
    bsz, t, _ = p.shape
    row = lambda n: pl.BlockSpec((1, n), lambda b, c: (0, 0))
    mat = lambda m, n: pl.BlockSpec((m, n), lambda b, c: (0, 0))
    tile = lambda dt: pltpu.VMEM((tc, RWKV_WIDTH), dt)
    two = pltpu.VMEM((2, tc, RWKV_WIDTH), BF16)
    return pl.pallas_call(
        functools.partial(_rwkv_prompt_kernel, tc=tc),
        grid=(bsz, t // tc),
        in_specs=[pl.BlockSpec((1, tc, RWKV_COLS), lambda b, c: (b, c, 0)),
                  row(RWKV_COLS), row(512), mat(64, 512), row(512), mat(64, 512), mat(128, 512),
                  row(512), row(512), row(512), row(512), row(512)],
        out_specs=[pl.BlockSpec((1, tc, RWKV_WIDTH), lambda b, c: (b, c, 0)),
                   pl.BlockSpec((1, N_HEADS, HEAD, HEAD), lambda b, c: (b, 0, 0, 0))],
        out_shape=[jax.ShapeDtypeStruct((bsz, t, RWKV_WIDTH), BF16),
                   jax.ShapeDtypeStruct((bsz, N_HEADS, HEAD, HEAD), F32)],
        scratch_shapes=[pltpu.VMEM((8, RWKV_COLS), F32), pltpu.VMEM((HEAD, RWKV_WIDTH), F32), two, two,
                        tile(BF16), tile(BF16), tile(BF16), tile(F32)],
        compiler_params=_params(("arbitrary", "arbitrary"), 48),
        name="rwkv_prompt",
    )(p, lp["mu_shift"], lp["w_decay0"], lp["w_decay_up"], lp["a0"], lp["w_a_up"], lp["w_g_up"], lp["k_k"],
      lp["k_a"], lp["r_k"], lp["ln_x_w"], lp["ln_x_b"])


def _rwkv_sample_prep_kernel(p_ref, prev_ref, mu_ref, w0_ref, wd_ref, a0_ref, wa_ref, wg_ref, kk_ref, ka_ref,
                             r_ref, k_ref, v_ref, g_ref, *t_refs):
    ones_bd = _block_ones(RWKV_WIDTH, 6)
    r, logw, k2, v, a, b, gate = _rwkv_prep(p_ref[...], prev_ref[...], mu_ref[...], w0_ref[...], wd_ref[...],
                                             a0_ref[...], wa_ref[...], wg_ref[...], kk_ref[...], ka_ref[...], ones_bd,
                                             True)
    r_ref[...] = r
    k_ref[...] = k2
    v_ref[...] = v
    g_ref[...] = gate
    for ref, x in zip(t_refs, (r, jnp.exp(logw), k2, v, a, b)):
        ref[...] = x.T


def _rwkv_sample_step_kernel(s_ref, r_ref, w_ref, k_ref, v_ref, a_ref, b_ref, so_ref, y_ref, *, unroll):
    a, w, b, k, r = a_ref[...], w_ref[...], b_ref[...], k_ref[...], r_ref[...]

    def rows(i0, carry):
        base = pl.multiple_of(i0 * unroll, unroll)
        v8 = v_ref[pl.ds(base, unroll), :]
        s = [s_ref[0, base + u] for u in range(unroll)]
        sa = [jnp.sum(s[u] * a, axis=0, keepdims=True) for u in range(unroll)]
        s = [s[u] * w + sa[u] * b + v8[u:u + 1, :] * k for u in range(unroll)]
        for u in range(unroll):
            so_ref[0, base + u] = s[u]
        y_ref[pl.ds(base, unroll), :] = jnp.concatenate(
            [jnp.sum(s[u] * r, axis=0, keepdims=True) for u in range(unroll)], axis=0)
        return carry

    lax.fori_loop(0, HEAD // unroll, rows, 0)


def _rwkv_sample(p, prev, wkv_t, lp, unroll=8):
    n = p.shape[0]
    full = lambda a: pl.BlockSpec(a.shape, lambda i: (0,) * a.ndim)
    names = ("mu_shift", "w_decay0", "w_decay_up", "a0", "w_a_up", "w_g_up", "k_k", "k_a")
    prep_in = [p, prev] + [lp[k] for k in names]
    vecs = pl.pallas_call(
        _rwkv_sample_prep_kernel,
        grid=(1,),
        in_specs=[full(a) for a in prep_in],
        out_specs=[pl.BlockSpec((n, RWKV_WIDTH), lambda i: (0, 0))] * 4
        + [pl.BlockSpec((RWKV_WIDTH, n), lambda i: (0, 0))] * 6,
        out_shape=[jax.ShapeDtypeStruct((n, RWKV_WIDTH), F32)] * 4
        + [jax.ShapeDtypeStruct((RWKV_WIDTH, n), F32)] * 6,
        compiler_params=_params(("arbitrary",), 32),
        name="rwkv_sample_prep",
    )(*prep_in)
    vec_t = pl.BlockSpec((HEAD, n), lambda h: (h, 0))
    st_spec = pl.BlockSpec((1, HEAD, HEAD, n), lambda h: (h, 0, 0, 0))
    wkv_new, y_t = pl.pallas_call(
        functools.partial(_rwkv_sample_step_kernel, unroll=unroll),
        grid=(N_HEADS,),
        in_specs=[st_spec] + [vec_t] * 6,
        out_specs=[st_spec, vec_t],
        out_shape=[jax.ShapeDtypeStruct(wkv_t.shape, F32), jax.ShapeDtypeStruct((RWKV_WIDTH, n), F32)],
        compiler_params=_params(("parallel",), 32),
        name="rwkv_sample_step",
    )(wkv_t, *vecs[4:])
    return vecs[:4], y_t, wkv_new


def _rope_table_kernel(invf_ref, cos_ref, sin_ref, *, rows):
    pos = (pl.program_id(0) * rows + _iota((rows, 1), 0)).astype(F32)
    cos_ref[...], sin_ref[...] = _rope_tables(pos, invf_ref[...])


def _rope_table(t, inv_freq, rows=512):
    spec = pl.BlockSpec((rows, 128), lambda i: (i, 0))
    return pl.pallas_call(
        functools.partial(_rope_table_kernel, rows=rows),
        grid=(t // rows,),
        in_specs=[pl.BlockSpec((1, 128), lambda i: (0, 0))],
        out_specs=[spec, spec],
        out_shape=[jax.ShapeDtypeStruct((t, 128), F32)] * 2,
        compiler_params=_params(("parallel",), 32),
        name="rope_table",
    )(inv_freq)


def _swa_prompt_kernel(p_ref, cos_ref, sin_ref, sink_ref, y_ref, ko_ref, vo_ref, kprev_ref, vprev_ref, *, nb):
    n = pl.program_id(1)
    blk = WINDOW

    @pl.when(n == 0)
    def _init():
        kprev_ref[...] = jnp.zeros_like(kprev_ref)
        vprev_ref[...] = jnp.zeros_like(vprev_ref)

    p = p_ref[0]
    cos, sin_signed = cos_ref[...], sin_ref[...]
    q = _rope(p[:, :RWKV_WIDTH], cos, sin_signed)
    k = _rope(p[:, RWKV_WIDTH:RWKV_WIDTH + KV_WIDTH], cos, sin_signed)
    v = p[:, RWKV_WIDTH + KV_WIDTH:]
    kcat = jnp.concatenate([kprev_ref[...], k], axis=0)
    vcat = jnp.concatenate([vprev_ref[...], v], axis=0)
    ii, jj = _iota((blk, 2 * blk), 0), _iota((blk, 2 * blk), 1)
    in_window = (jj > ii) & (jj <= ii + blk)
    bias = jnp.concatenate([jnp.where(in_window, 0.0, NEG)] * 4, axis=0)
    lo = jnp.where(n == 0, blk, 0)
    bias_first = jnp.concatenate([jnp.where(in_window & (jj >= lo), 0.0, NEG)] * 4, axis=0)
    units = [(j, g) for j in range(nb) for g in range(2)]
    rows = lambda j, m: slice(j * blk, (j + m) * blk)
    heads = lambda g: range(4 * g, 4 * g + 4)
    cols = lambda g: slice(g * HEAD, (g + 1) * HEAD)
    sink_col = [jnp.concatenate([jnp.broadcast_to(sink_ref[0:1, h:h + 1], (blk, 1)) for h in heads(g)], axis=0)
                for g in range(2)]
    qg = {(j, g): _bf(jnp.concatenate([q[rows(j, 1), h * HEAD:(h + 1) * HEAD] for h in heads(g)], axis=0))
          for j, g in units}
    s = {(j, g): _dot_nt(qg[(j, g)], _bf(kcat[rows(j, 2), cols(g)])) * (HEAD ** -0.5)
         + (bias_first if j == 0 else bias) for j, g in units}
    pr = {(j, g): _bf(_softmax_sink(s[(j, g)], sink_col[g])) for j, g in units}
    o = {(j, g): _dot(pr[(j, g)], _bf(vcat[rows(j, 2), cols(g)])) for j, g in units}
    y_ref[0] = _bf(jnp.concatenate(
        [jnp.concatenate([o[(j, g)][i * blk:(i + 1) * blk] for g in range(2) for i in range(4)], axis=1)
         for j in range(nb)], axis=0))
    k_last, v_last = k[(nb - 1) * blk:], v[(nb - 1) * blk:]
    ko_ref[0] = k_last
    vo_ref[0] = v_last
    kprev_ref[...] = k_last
    vprev_ref[...] = v_last


def _swa_prompt(p, inv_freq, sink, nb=4):
    bsz, t, _ = p.shape
    blk = WINDOW
    rows = nb * blk
    cos, sin_signed = _rope_table(t, inv_freq, rows)
    table = pl.BlockSpec((rows, 128), lambda b, n: (n, 0))
    return pl.pallas_call(
        functools.partial(_swa_prompt_kernel, nb=nb),
        grid=(bsz, t // rows),
        in_specs=[pl.BlockSpec((1, rows, SWA_COLS), lambda b, n: (b, n, 0)), table, table,
                  pl.BlockSpec((1, N_HEADS), lambda b, n: (0, 0))],
        out_specs=[pl.BlockSpec((1, rows, RWKV_WIDTH), lambda b, n: (b, n, 0)),
                   pl.BlockSpec((1, blk, KV_WIDTH), lambda b, n: (b, 0, 0)),
                   pl.BlockSpec((1, blk, KV_WIDTH), lambda b, n: (b, 0, 0))],
        out_shape=[jax.ShapeDtypeStruct((bsz, t, RWKV_WIDTH), BF16),
                   jax.ShapeDtypeStruct((bsz, blk, KV_WIDTH), F32),
                   jax.ShapeDtypeStruct((bsz, blk, KV_WIDTH), F32)],
        scratch_shapes=[pltpu.VMEM((blk, KV_WIDTH), F32), pltpu.VMEM((blk, KV_WIDTH), F32)],
        compiler_params=_params(("arbitrary", "arbitrary"), 48),
        name="swa_prompt",
    )(p, cos, sin_signed, sink)


def _swa_sample_kernel(p_ref, ck_ref, cv_ref, invf_ref, sinkc_ref, y_ref, nk_ref, nv_ref, *, bt):
    p = p_ref[...]
    cos, sin_signed = _rope_tables(jnp.full((1, 1), float(PAST_LEN), F32), invf_ref[...])
    q = _rope(p[:, :RWKV_WIDTH], cos, sin_signed) * (HEAD ** -0.5)
    k = _rope(p[:, RWKV_WIDTH:RWKV_WIDTH + KV_WIDTH], cos, sin_signed)
    v = p[:, RWKV_WIDTH + KV_WIDTH:]
    k_t, v_t = k.T, v.T
    last = _iota((KV_WIDTH, WINDOW), 1) == WINDOW - 1
    head_mask = _iota((N_HEADS, RWKV_WIDTH), 0) == (_iota((N_HEADS, RWKV_WIDTH), 1) >> 6)
    tile_lanes = lambda x: jnp.concatenate([x[:, :HEAD]] * 4 + [x[:, HEAD:]] * 4, axis=1)
    tile_rows = lambda x: jnp.concatenate([x[:HEAD]] * 4 + [x[HEAD:]] * 4, axis=0)
    seqs = range(bt)
    k_new = [jnp.where(last, k_t[:, bi:bi + 1], pltpu.roll(ck_ref[bi], WINDOW - 1, axis=1)) for bi in seqs]
    v_new = [jnp.where(last, v_t[:, bi:bi + 1], pltpu.roll(cv_ref[bi], WINDOW - 1, axis=1)) for bi in seqs]
    for bi in seqs:
        nk_ref[bi] = k_new[bi]
        nv_ref[bi] = v_new[bi]
    q_bd = [jnp.where(head_mask, q[bi:bi + 1, :], 0.0) for bi in seqs]
    s = [_dot3(q_bd[bi], tile_rows(k_new[bi])) for bi in seqs]
    pr = [_softmax_sink(s[bi], sinkc_ref[...]) for bi in seqs]
    o = [_dot3_nt(pr[bi], v_new[bi]) for bi in seqs]
    y_ref[...] = jnp.concatenate(
        [jnp.sum(jnp.where(head_mask, tile_lanes(o[bi]), 0.0), axis=0, keepdims=True) for bi in seqs], axis=0)


def _swa_sample(p, ck, cv, inv_freq, sink_col, bt=8):
    n = p.shape[0]
    cache_spec = pl.BlockSpec((bt, KV_WIDTH, WINDOW), lambda i: (i, 0, 0))
    return pl.pallas_call(
        functools.partial(_swa_sample_kernel, bt=bt),
        grid=(n // bt,),
        in_specs=[pl.BlockSpec((bt, SWA_COLS), lambda i: (i, 0)), cache_spec, cache_spec,
                  pl.BlockSpec((1, 128), lambda i: (0, 0)), pl.BlockSpec((N_HEADS, 1), lambda i: (0, 0))],
        out_specs=[pl.BlockSpec((bt, RWKV_WIDTH), lambda i: (i, 0)), cache_spec, cache_spec],
        out_shape=[jax.ShapeDtypeStruct((n, RWKV_WIDTH), F32), jax.ShapeDtypeStruct(ck.shape, F32),
                   jax.ShapeDtypeStruct(cv.shape, F32)],
        compiler_params=_params(("parallel",), 32),
        name="swa_sample",
    )(p, ck, cv, inv_freq, sink_col)


def _mix_out(yr, ys, x, wo1, wo2):
    return x + _dot(yr, wo1) + _dot(ys, wo2)


def _out_cross_kernel(yr_ref, ys_ref, x_ref, wo1_ref, wo2_ref, gc_ref, wq_ref, mk_ref, mv_ref, wco_ref, o_ref):
    x1 = _mix_out(yr_ref[0], ys_ref[0], x_ref[0], wo1_ref[...], wo2_ref[...])
    q = _bf(_dot(_bf(_rmsnorm(x1, gc_ref[...])), wq_ref[...]))
    mk, mv = mk_ref[0], mv_ref[0]
    outs = []
    for h in range(X_HEADS):
        sl = slice(h * X_HEAD, (h + 1) * X_HEAD)
        s = _dot_nt(q[:, sl], mk[:, sl]) * (X_HEAD ** -0.5)
        e = jnp.exp(s - jnp.max(s, axis=-1, keepdims=True))
        pr = e / jnp.sum(e, axis=-1, keepdims=True)
        outs.append(_dot(_bf(pr), mv[:, sl]))
    o_ref[0] = x1 + _dot(_bf(jnp.concatenate(outs, axis=1)), wco_ref[...])


def _out_cross(yr, ys, x, wo1, wo2, gain, wq, mk, mv, wco, tm=512):
    bsz, t, d = x.shape
    tile = lambda w: pl.BlockSpec((1, tm, w), lambda b, i: (b, i, 0))
    const = lambda a: pl.BlockSpec(a.shape, lambda b, i: (0,) * a.ndim)
    mem = pl.BlockSpec((1, MEM_TOKENS, X_WIDTH), lambda b, i: (b, 0, 0))
    return pl.pallas_call(
        _out_cross_kernel,
        grid=(bsz, t // tm),
        in_specs=[tile(RWKV_WIDTH), tile(RWKV_WIDTH), tile(d), const(wo1), const(wo2), const(gain), const(wq),
                  mem, mem, const(wco)],
        out_specs=tile(d),
        out_shape=jax.ShapeDtypeStruct((bsz, t, d), F32),
        compiler_params=_params(("parallel", "parallel"), 48),
        name="out_cross",
    )(yr, ys, x, wo1, wo2, gain, wq, mk, mv, wco)


def _out_q_kernel(yt_ref, r_ref, k_ref, v_ref, g_ref, rk_ref, lnw_ref, lnb_ref, ys_ref, x_ref, wo1_ref, wo2_ref,
                  gc_ref, wq_ref, x1_ref, q_ref):
    yr = _rwkv_post(yt_ref[...].T, r_ref[...], k_ref[...], v_ref[...], g_ref[...], rk_ref[...], lnw_ref[...],
                    lnb_ref[...], _block_ones(RWKV_WIDTH, 6), True)
    x1 = x_ref[...] + _dot3(yr, wo1_ref[...]) + _dot3(ys_ref[...], wo2_ref[...])
    x1_ref[...] = x1
    q_ref[...] = _dot3(_rmsnorm(x1, gc_ref[...]), wq_ref[...])


def _cross_sample_kernel(q_ref, mk_ref, mv_ref, o_ref, *, bt):
    seqs = range(bt)
    both = lambda x: x + pltpu.roll(x, X_HEADS, axis=0)
    q8 = [jnp.concatenate([q_ref[bi]] * 2, axis=0) for bi in seqs]
    s = [jnp.sum(mk_ref[bi] * q8[bi][None], axis=-1, keepdims=True) * (X_HEAD ** -0.5) for bi in seqs]
    mx = [jnp.max(s[bi], axis=0) for bi in seqs]
    mx = [jnp.maximum(m, pltpu.roll(m, X_HEADS, axis=0)) for m in mx]
    e = [jnp.exp(s[bi] - mx[bi][None]) for bi in seqs]
    den = [both(jnp.sum(e[bi], axis=0)) for bi in seqs]
    o8 = [jnp.sum(e[bi] * mv_ref[bi], axis=0) / den[bi] for bi in seqs]
    for bi in seqs:
        o_ref[bi] = both(o8[bi])[:X_HEADS]


def _lin_res_kernel(x_ref, a_ref, w_ref, o_ref):
    o_ref[...] = x_ref[...] + _dot3(a_ref[...], w_ref[...])


def _single_step(kernel_fn, out_shape, name, *args):
    full = lambda a: pl.BlockSpec(a.shape, lambda i: (0,) * len(a.shape))
    outs = out_shape if isinstance(out_shape, (list, tuple)) else [out_shape]
    out_specs = [full(o) for o in outs]
    return pl.pallas_call(
        kernel_fn, grid=(1,), in_specs=[full(a) for a in args],
        out_specs=out_specs if isinstance(out_shape, (list, tuple)) else out_specs[0],
        out_shape=out_shape, compiler_params=_params(("arbitrary",), 32), name=name,
    )(*args)


def _out_cross_sample(rwkv_parts, ys, x, wo1, wo2, gain, wq, mk, mv, wco, bt=4):
    n, d = x.shape
    x1, q = _single_step(_out_q_kernel, [jax.ShapeDtypeStruct((n, d), F32), jax.ShapeDtypeStruct((n, X_WIDTH), F32)],
                         "out_q_sample", *rwkv_parts, ys, x, wo1, wo2, gain, wq)
    mem_shape = (n, MEM_TOKENS // 2, 2 * X_HEADS, X_HEAD)
    mem = pl.BlockSpec((bt,) + mem_shape[1:], lambda i: (i, 0, 0, 0))
    vec = pl.BlockSpec((bt, X_HEADS, X_HEAD), lambda i: (i, 0, 0))
    o = pl.pallas_call(
        functools.partial(_cross_sample_kernel, bt=bt),
        grid=(n // bt,),
        in_specs=[vec, mem, mem],
        out_specs=vec,
        out_shape=jax.ShapeDtypeStruct((n, X_HEADS, X_HEAD), F32),
        compiler_params=_params(("parallel",), 48),
        name="cross_sample",
    )(q.reshape(n, X_HEADS, X_HEAD), mk.reshape(mem_shape), mv.reshape(mem_shape))
    return _single_step(_lin_res_kernel, jax.ShapeDtypeStruct((n, d), F32), "cross_out_sample", x1,
                        o.reshape(n, X_WIDTH), wco)


def _route(logits):
    lane_i = _iota(logits.shape, 1)
    lane = lane_i.astype(F32)
    big = 1024.0
    n_exp = float(N_GROUPS * EXPERTS_PER_GROUP)
    lg = jnp.where((lane >= n_exp) & (lane < n_exp + N_GROUPS), logits, NEG)
    g_max = jnp.max(lg, axis=-1, keepdims=True)
    g_idx = jnp.min(jnp.where(lg == g_max, lane, big), axis=-1, keepdims=True) - n_exp
    p_sel = 1.0 / jnp.sum(jnp.exp(lg - g_max), axis=-1, keepdims=True)
    le = jnp.where((lane_i >> 3).astype(F32) == g_idx, logits, NEG)
    t1 = jnp.max(le, axis=-1, keepdims=True)
    i1 = jnp.min(jnp.where(le == t1, lane, big), axis=-1, keepdims=True)
    le2 = jnp.where(lane == i1, NEG, le)
    t2 = jnp.max(le2, axis=-1, keepdims=True)
    i2 = jnp.min(jnp.where(le2 == t2, lane, big), axis=-1, keepdims=True)
    e2 = jnp.exp(t2 - t1)
    w1 = 1.0 / (1.0 + e2)
    return p_sel * (jnp.where(lane == i1, w1, 0.0) + jnp.where(lane == i2, e2 * w1, 0.0)), g_idx


def _moe_kernel(x_ref, gf_ref, wr_ref, br_ref, wg_ref, wu_ref, wd_ref, gfin_ref, o_ref, h_ref, gate_ref, pos_ref,
                acc_ref, *, tm, blocks):
    g = pl.program_id(1)

    @pl.when(g == 0)
    def _init():
        h = _rmsnorm(x_ref[...], gf_ref[...])
        h_ref[...] = _bf(h)
        gate, g_idx = _route(_dot3(h, wr_ref[...]) + br_ref[...])
        gate_ref[...] = gate
        lane = _iota((tm, 128), 1).astype(F32)
        earlier = _mask_bf(_iota((tm, tm), 1) < _iota((tm, tm), 0))
        own = lane == g_idx
        count = _dot(earlier, _mask_bf(own))
        pos_ref[...] = jnp.where(own, count, -1.0)
        acc_ref[...] = jnp.zeros_like(acc_ref)

    lane = _iota((tm, 128), 1)
    pos = jnp.sum(jnp.where(lane == g, pos_ref[...], 0.0), axis=-1, keepdims=True)
    n_g = jnp.sum(jnp.where(pos >= 0.0, 1.0, 0.0))
    width = EXPERTS_PER_GROUP * EXPERT_HIDDEN
    expand = _mask_bf(_iota((128, width), 0) == g * EXPERTS_PER_GROUP + (_iota((128, width), 1) >> 7))
    start = 0
    for bm in blocks:
        @pl.when(n_g > start)
        def _block(start=start, bm=bm):
            sel = _mask_bf(pos - float(start) == _iota((tm, bm), 1).astype(F32))
            hs = _bf(_dot_tn(sel, h_ref[...]))
            gh, gl = _split(gate_ref[...])
            gs = _dot_tn(sel, gh) + _dot_tn(sel, gl)
            hg = _dot(hs, wg_ref[0])
            hu = _dot(hs, wu_ref[0])
            act = hg * _sigmoid(hg) * hu * _dot_sel_rhs(gs, expand)
            acc_ref[...] += _dot(sel, _bf(_dot(_bf(act), wd_ref[0])))

        start += bm

    @pl.when(g == N_GROUPS - 1)
    def _fin():
        o_ref[...] = _rmsnorm(x_ref[...] + acc_ref[...], gfin_ref[...])


def _moe(x, gain, w_router, b_router, wg, wu, wd, gain_final, tm, blocks):
    n, d = x.shape
    assert sum(blocks) == tm
    width = EXPERTS_PER_GROUP * EXPERT_HIDDEN
    const = lambda a: pl.BlockSpec(a.shape, lambda i, g: (0,) * a.ndim)
    return pl.pallas_call(
        functools.partial(_moe_kernel, tm=tm, blocks=blocks),
        grid=(n // tm, N_GROUPS),
        in_specs=[pl.BlockSpec((tm, d), lambda i, g: (i, 0)), const(gain), const(w_router), const(b_router),
                  pl.BlockSpec((1, d, width), lambda i, g: (g, 0, 0)),
                  pl.BlockSpec((1, d, width), lambda i, g: (g, 0, 0)),
                  pl.BlockSpec((1, width, d), lambda i, g: (g, 0, 0)), const(gain_final)],
        out_specs=pl.BlockSpec((tm, d), lambda i, g: (i, 0)),
        out_shape=jax.ShapeDtypeStruct((n, d), F32),
        scratch_shapes=[pltpu.VMEM((tm, d), BF16), pltpu.VMEM((tm, 128), F32), pltpu.VMEM((tm, 128), F32),
                        pltpu.VMEM((tm, d), F32)],
        compiler_params=_params(("parallel", "arbitrary"), 60),
        name="moe",
    )(x, gain, w_router, b_router, wg, wu, wd, gain_final)


def kernel(x_prompt, x_sample, mem_prompt, state_rwkv_shift, state_rwkv_wkv, cache_swa_k, cache_swa_v, cache_mem_k, cache_mem_v, norm_mix, w_in, mu_shift, w_decay0, w_decay_up, a0, w_a_up, w_g_up, k_k, k_a, r_k, ln_x_w, ln_x_b, attn_sink, w_out, norm_cross, norm_mem, w_cq, w_ck, w_cv, w_co, norm_ffn, w_group_router, b_group_router, w_expert_router, b_expert_router, w_exp_gate, w_exp_up, w_exp_down, norm_final):
    bsz, t, d = x_prompt.shape
    n_s = x_sample.shape[0]
    wb = cache_swa_k.shape[2]
    width = EXPERTS_PER_GROUP * EXPERT_HIDDEN

    w_in_bf = w_in[0].astype(BF16)
    wo1, wo2 = w_out[0, :RWKV_WIDTH].astype(BF16), w_out[0, RWKV_WIDTH:].astype(BF16)
    wq, wco = w_cq[0].astype(BF16), w_co[0].astype(BF16)
    w_ckv = jnp.concatenate([w_ck[0], w_cv[0]], axis=1).astype(BF16)
    w_router = jnp.pad(jnp.concatenate([w_expert_router[0], w_group_router[0]], axis=1), ((0, 0), (0, 128 - 36)))
    b_router = jnp.pad(jnp.concatenate([b_expert_router[0], b_group_router[0]]), (0, 128 - 36)).reshape(1, 128)
    wg = jnp.transpose(w_exp_gate[0], (0, 2, 1, 3)).reshape(N_GROUPS, d, width).astype(BF16)
    wu = jnp.transpose(w_exp_up[0], (0, 2, 1, 3)).reshape(N_GROUPS, d, width).astype(BF16)
    wd = w_exp_down[0].reshape(N_GROUPS, width, d).astype(BF16)
    row = lambda a: a.reshape(1, -1)
    lp = dict(mu_shift=mu_shift, w_decay0=w_decay0, w_decay_up=w_decay_up[0], a0=a0, w_a_up=w_a_up[0],
              w_g_up=w_g_up[0], k_k=k_k, k_a=k_a, r_k=row(r_k), ln_x_w=ln_x_w, ln_x_b=ln_x_b)
    half = HEAD // 2
    inv_freq = ROPE_THETA ** (-jnp.arange(half, dtype=F32) * 2.0 / HEAD)
    inv_freq = jnp.tile(inv_freq, 128 // half).reshape(1, 128)

    p_rwkv, p_swa = _norm_matmul(x_prompt.reshape(bsz * t, d), norm_mix, w_in_bf, RWKV_COLS, 512)
    p_rwkv = p_rwkv.reshape(bsz, t, RWKV_COLS)
    y_rwkv, wkv_p = _rwkv_prompt(p_rwkv, lp)
    y_swa, k_p, v_p = _swa_prompt(p_swa.reshape(bsz, t, SWA_COLS), inv_freq, attn_sink)
    mk_p, mv_p = _norm_matmul(mem_prompt.reshape(bsz * MEM_TOKENS, d), norm_mem, w_ckv, X_WIDTH, 256)
    mk_b = mk_p.astype(BF16).reshape(bsz, MEM_TOKENS, X_WIDTH)
    mv_b = mv_p.astype(BF16).reshape(bsz, MEM_TOKENS, X_WIDTH)
    x2 = _out_cross(y_rwkv, y_swa, x_prompt, wo1, wo2, norm_cross, wq, mk_b, mv_b, wco)
    y_prompt = _moe(x2.reshape(bsz * t, d), norm_ffn, w_router, b_router, wg, wu, wd, row(norm_final), 1024,
                    (256, 128, 128, 256, 256))

    xs = x_sample.reshape(n_s, d)
    ps_rwkv, ps_swa = _norm_matmul(xs, norm_mix, w_in[0], RWKV_COLS, n_s)
    (r_s, k2_s, v_s_tok, g_s), yt_s, wkv_s = _rwkv_sample(ps_rwkv, state_rwkv_shift[0],
                                                           jnp.transpose(state_rwkv_wkv[0], (1, 2, 3, 0)), lp)
    cache_t = lambda c: jnp.transpose(c[0].reshape(n_s, wb, KV_WIDTH), (0, 2, 1))
    ys_swa, k_s, v_s = _swa_sample(ps_swa, cache_t(cache_swa_k), cache_t(cache_swa_v), inv_freq,
                                   attn_sink.reshape(N_HEADS, 1))
    x2s = _out_cross_sample((yt_s, r_s, k2_s, v_s_tok, g_s, lp["r_k"], lp["ln_x_w"], lp["ln_x_b"]), ys_swa, xs,
                            w_out[0, :RWKV_WIDTH], w_out[0, RWKV_WIDTH:], norm_cross, w_cq[0],
                            cache_mem_k[0], cache_mem_v[0], w_co[0])
    y_sample = _moe(x2s, norm_ffn, w_router, b_router, wg, wu, wd, row(norm_final), n_s, (n_s,))

    kv_shape = (1, -1, wb, KV_WIDTH // HEAD, HEAD)
    mem_shape = (1, bsz, MEM_TOKENS, X_HEADS, X_HEAD)
    cache_back = lambda c: jnp.transpose(c, (0, 2, 1)).reshape(kv_shape)
    return (y_prompt.reshape(bsz, t, d), y_sample.reshape(n_s, 1, d),
            p_rwkv[:, -1][None], wkv_p[None], k_p.reshape(kv_shape), v_p.reshape(kv_shape),
            mk_p.reshape(mem_shape), mv_p.reshape(mem_shape),
            ps_rwkv[None], jnp.transpose(wkv_s, (3, 0, 1, 2))[None], cache_back(k_s), cache_back(v_s))
```

```python
import functools

import jax
import jax.numpy as jnp
from jax import lax
from jax.experimental import pallas as pl
from jax.experimental.pallas import tpu as pltpu

F32 = jnp.float32
BF16 = jnp.bfloat16

D_MODEL = 1024
RWKV_WIDTH = 512
HEAD = 64
N_HEADS = 8
RWKV_COLS = 1792
SWA_COLS = 768
KV_WIDTH = 128
WINDOW = 128
PAST_LEN = 8192
ROPE_THETA = 10000.0
MEM_TOKENS = 256
X_HEADS = 4
X_HEAD = 128
X_WIDTH = 512
N_GROUPS = 4
EXPERTS_PER_GROUP = 8
EXPERT_HIDDEN = 128
NORM_EPS = 1e-5
GN_EPS = 64e-5
L2_EPS = 1e-12
CHUNK = 64
NEG = -1e30
MIB = 1024 * 1024


def _params(semantics, vmem_mib):
    return pltpu.CompilerParams(dimension_semantics=semantics, vmem_limit_bytes=vmem_mib * MIB)


def _bf(x):
    return x.astype(BF16)


def _dot(a, b):
    return jnp.dot(a, b, preferred_element_type=F32)


def _dot_nt(a, b):
    return lax.dot_general(a, b, (((1,), (1,)), ((), ())), preferred_element_type=F32)


def _dot_tn(a, b):
    return lax.dot_general(a, b, (((0,), (0,)), ((), ())), preferred_element_type=F32)


def _split(x):
    hi = x.astype(BF16)
    lo = (x - hi.astype(F32)).astype(BF16)
    return hi, lo


def _dot3(a, b, dot=_dot):
    ah, al = _split(a)
    bh, bl = _split(b)
    return dot(ah, bh) + (dot(ah, bl) + dot(al, bh))


def _dot3_nt(a, b):
    return _dot3(a, b, _dot_nt)


def _dot_sel_rhs(a, sel):
    ah, al = _split(a)
    return _dot(ah, sel) + _dot(al, sel)


def _dot_sel_lhs(sel, b):
    bh, bl = _split(b)
    return _dot(sel, bh) + _dot(sel, bl)


def _rmsnorm(x, gain):
    return x * lax.rsqrt(jnp.mean(x * x, axis=-1, keepdims=True) + NORM_EPS) * gain


def _sigmoid(x):
    return 1.0 / (1.0 + jnp.exp(-x))


def _iota(shape, dim):
    return lax.broadcasted_iota(jnp.int32, shape, dim)


def _mask_bf(cond):
    return jnp.where(cond, 1.0, 0.0).astype(BF16)


def _block_ones(n, blk_shift):
    return _mask_bf((_iota((n, n), 0) >> blk_shift) == (_iota((n, n), 1) >> blk_shift))


def _softmax_sink(s, sink_col):
    m = jnp.maximum(jnp.max(s, axis=-1, keepdims=True), sink_col)
    e = jnp.exp(s - m)
    den = jnp.sum(e, axis=-1, keepdims=True) + jnp.exp(sink_col - m)
    return e / den


def _rope(x, cos, sin_signed):
    width = x.shape[1]
    reps = width // 128
    if reps > 1:
        cos = jnp.concatenate([cos] * reps, axis=1)
        sin_signed = jnp.concatenate([sin_signed] * reps, axis=1)
    first_half = (_iota(x.shape, 1) & (HEAD - 1)) < HEAD // 2
    rot = jnp.where(first_half, pltpu.roll(x, width - HEAD // 2, axis=1), pltpu.roll(x, HEAD // 2, axis=1))
    return x * cos + rot * sin_signed


def _rope_tables(pos, inv_freq):
    ang = pos * inv_freq
    cos, sin = jnp.cos(ang), jnp.sin(ang)
    first_half = (_iota(ang.shape, 1) & (HEAD - 1)) < HEAD // 2
    return cos, jnp.where(first_half, -sin, sin)


def _norm_matmul_kernel(x_ref, g_ref, w_ref, o1_ref, o2_ref, *, split):
    h = _rmsnorm(x_ref[...], g_ref[...])
    w = w_ref[...]
    p = _dot(_bf(h), w) if w.dtype == BF16 else _dot3(h, w)
    o1_ref[...] = p[:, :split]
    o2_ref[...] = p[:, split:]


def _norm_matmul(x, gain, w, split, tm):
    n, d = x.shape
    cols = w.shape[1]
    return pl.pallas_call(
        functools.partial(_norm_matmul_kernel, split=split),
        grid=(n // tm,),
        in_specs=[pl.BlockSpec((tm, d), lambda i: (i, 0)),
                  pl.BlockSpec((1, d), lambda i: (0, 0)),
                  pl.BlockSpec((d, cols), lambda i: (0, 0))],
        out_specs=[pl.BlockSpec((tm, split), lambda i: (i, 0)),
                   pl.BlockSpec((tm, cols - split), lambda i: (i, 0))],
        out_shape=[jax.ShapeDtypeStruct((n, split), F32), jax.ShapeDtypeStruct((n, cols - split), F32)],
        compiler_params=_params(("parallel",), 48),
        name="norm_matmul",
    )(x, gain, w)


def _dot1(a, b):
    return _dot(_bf(a), _bf(b))


def _rwkv_prep(p, prev, mu, w0, wd, a0, wa, wg, k_k, k_a, ones_bd, precise):
    mm, head_sum = (_dot3, _dot_sel_rhs) if precise else (_dot1, _dot1)
    c = RWKV_WIDTH
    xm = p + (prev - p) * mu
    r, k, v = xm[:, :c], xm[:, c:2 * c], xm[:, 2 * c:3 * c]
    xw, xa, xg = xm[:, 3 * c:3 * c + 64], xm[:, 3 * c + 64:3 * c + 128], xm[:, 3 * c + 128:]
    z = -(w0 + _dot3(jnp.tanh(xw), wd))
    w = -(jnp.maximum(z, 0.0) + jnp.log(1.0 + jnp.exp(-jnp.abs(z)))) - 0.5
    logw = -jnp.exp(w)
    rate = _sigmoid(a0 + mm(xa, wa))
    gate = mm(_sigmoid(xg), wg)
    kk = k * k_k
    kk = kk * lax.rsqrt(jnp.maximum(head_sum(kk * kk, ones_bd), L2_EPS * L2_EPS))
    k2 = k * (1.0 + (rate - 1.0) * k_a)
    return r, logw, k2, v, -kk, kk * rate, gate


def _rwkv_post(y, r, k2, v, gate, r_k, ln_w, ln_b, ones_bd, precise):
    head_sum = _dot_sel_rhs if precise else _dot1
    mean = head_sum(y, ones_bd) * (1.0 / HEAD)
    d = y - mean
    var = head_sum(d * d, ones_bd) * (1.0 / HEAD)
    yn = d * lax.rsqrt(var + GN_EPS) * ln_w + ln_b
    bonus = head_sum(r * k2 * r_k, ones_bd) * v
    return (yn + bonus) * gate


def _rwkv_prompt_kernel(p_ref, mu_ref, w0_ref, wd_ref, a0_ref, wa_ref, wg_ref, kk_ref, ka_ref, rk_ref, lnw_ref,
                        lnb_ref, y_ref, s_ref, carry_ref, st_ref, ar_ref, bk_ref, bp_ref, kp_ref, v_ref, yacc_ref,
                        *, tc):
    c = pl.program_id(1)

    @pl.when(c == 0)
    def _init():
        carry_ref[...] = jnp.zeros_like(carry_ref)
        st_ref[...] = jnp.zeros_like(st_ref)

    p = p_ref[0]
    first_row = _iota((tc, 1), 0) == 0
    prev = jnp.where(first_row, carry_ref[0:1, :], pltpu.roll(p, 1, axis=0))
    carry_ref[0:1, :] = p[tc - 1:tc, :]
    ones_bd = _block_ones(RWKV_WIDTH, 6)
    r, logw, k2, v, a, b, gate = _rwkv_prep(p, prev, mu_ref[...], w0_ref[...], wd_ref[...], a0_ref[...],
                                             wa_ref[...], wg_ref[...], kk_ref[...], ka_ref[...], ones_bd, False)
    ri, ci = _iota((tc, tc), 0), _iota((tc, tc), 1)
    same_chunk = (ri >> 6) == (ci >> 6)
    cum = _dot_sel_lhs(_mask_bf(same_chunk & (ci <= ri)), logw)
    n_ch = tc // CHUNK
    p_end = [jnp.exp(cum[(ci_ + 1) * CHUNK - 1:(ci_ + 1) * CHUNK, :]) for ci_ in range(n_ch)]
    e_inv = jnp.exp(-cum)
    e_end = e_inv * jnp.concatenate([jnp.broadcast_to(pe, (CHUNK, RWKV_WIDTH)) for pe in p_end], axis=0)
    ar_ref[0], ar_ref[1] = _bf(a * jnp.exp(cum - logw)), _bf(r * jnp.exp(cum))
    bk_ref[0], bk_ref[1] = _bf(b * e_inv), _bf(k2 * e_inv)
    bp_ref[...] = _bf(b * e_end)
    kp_ref[...] = _bf(k2 * e_end)
    v_ref[...] = _bf(v)

    rr, cc = _iota((CHUNK, 2 * HEAD), 0), _iota((CHUNK, 2 * HEAD), 1) & (HEAD - 1)
    strict, incl, eye = cc < rr, cc <= rr, cc == rr
    lane = _iota((CHUNK, 2 * HEAD), 1)
    left, right = lane < HEAD, lane >= HEAD
    zero = jnp.zeros((), BF16)

    def stack(x):
        return jnp.concatenate([jnp.where(left, x, zero.astype(x.dtype)), jnp.where(right, x, zero.astype(x.dtype))],
                               axis=0)

    def diag_blocks(x):
        return jnp.where(left, x[:HEAD], x[HEAD:])

    n_ch, n_pairs = tc // CHUNK, N_HEADS // 2
    ids = [(ci_, pr) for ci_ in range(n_ch) for pr in range(n_pairs)]
    rows = lambda ci_: slice(ci_ * CHUNK, (ci_ + 1) * CHUNK)
    cols = lambda pr: slice(pr * 2 * HEAD, (pr + 1) * 2 * HEAD)
    at = {i: ar_ref[0, rows(i[0]), cols(i[1])] for i in ids}
    rt = {i: ar_ref[1, rows(i[0]), cols(i[1])] for i in ids}
    vv = {i: v_ref[rows(i[0]), cols(i[1])] for i in ids}
    bp = {i: bp_ref[rows(i[0]), cols(i[1])] for i in ids}
    g = {i: _dot_nt(jnp.concatenate([at[i], rt[i]], axis=0),
                    jnp.concatenate([stack(bk_ref[0, rows(i[0]), cols(i[1])]),
                                     stack(bk_ref[1, rows(i[0]), cols(i[1])])], axis=0)) for i in ids}
    vk = {i: diag_blocks(_dot_tn(vv[i], kp_ref[rows(i[0]), cols(i[1])])) for i in ids}
    a_ab = {i: jnp.where(strict, g[i][:CHUNK, :2 * HEAD], 0.0) for i in ids}
    a_rb = {i: _bf(jnp.where(incl, g[i][CHUNK:, :2 * HEAD], 0.0)) for i in ids}
    av = {i: _dot(_bf(jnp.concatenate([jnp.where(strict, g[i][:CHUNK, 2 * HEAD:], 0.0),
                                       jnp.where(incl, g[i][CHUNK:, 2 * HEAD:], 0.0)], axis=0)), stack(vv[i]))
          for i in ids}
    t = {i: jnp.where(eye, 1.0, a_ab[i]) for i in ids}
    apow = {i: _bf(a_ab[i]) for i in ids}
    apow = {i: _bf(_dot(apow[i], stack(apow[i]))) for i in ids}
    for step in range(4):
        both = {i: _dot(jnp.concatenate([_bf(t[i]), apow[i]], axis=0), stack(apow[i])) for i in ids}
        t = {i: t[i] + both[i][:CHUNK] for i in ids}
        apow = {i: _bf(both[i][CHUNK:]) for i in ids}
    t = {i: t[i] + _dot(_bf(t[i]), stack(apow[i])) for i in ids}
    wu = {i: _bf(_dot(_bf(t[i]), jnp.concatenate([stack(at[i]), stack(_bf(av[i][:CHUNK]))], axis=1)))
          for i in ids}
    wy_y0 = {i: jnp.concatenate([rt[i].astype(F32), av[i][CHUNK:]], axis=1)
             + _dot(a_rb[i], jnp.concatenate([stack(wu[i][:, :2 * HEAD]), stack(wu[i][:, 2 * HEAD:])], axis=1))
             for i in ids}
    mbn = {i: _dot_tn(wu[i], bp[i]) for i in ids}
    rb, cb = _iota((2 * HEAD, 2 * HEAD), 0) < HEAD, _iota((2 * HEAD, 2 * HEAD), 1) < HEAD
    mb = {i: _bf(jnp.where(rb == cb, mbn[i][:2 * HEAD], 0.0)) for i in ids}
    n0 = {i: vk[i] + diag_blocks(mbn[i][2 * HEAD:]) for i in ids}
    state = [st_ref[:, cols(pr)] for pr in range(n_pairs)]
    for ci_ in range(n_ch):
        sb = [_bf(s) for s in state]
        ys = [wy_y0[(ci_, pr)][:, 2 * HEAD:] + _dot_nt(_bf(wy_y0[(ci_, pr)][:, :2 * HEAD]), stack(sb[pr]))
              for pr in range(n_pairs)]
        state = [state[pr] * p_end[ci_][:, cols(pr)] + _dot(sb[pr], mb[(ci_, pr)])
                 + n0[(ci_, pr)] for pr in range(n_pairs)]
        for pr in range(n_pairs):
            yacc_ref[rows(ci_), cols(pr)] = ys[pr]
    for pr in range(n_pairs):
        st_ref[:, cols(pr)] = state[pr]
    y_ref[0] = _bf(_rwkv_post(yacc_ref[...], r, k2, v, gate, rk_ref[...], lnw_ref[...], lnb_ref[...], ones_bd,
                              False))
    for h in range(N_HEADS):
        s_ref[0, h] = state[h // 2][:, (h % 2) * HEAD:(h % 2 + 1) * HEAD]


def _rwkv_prompt(p, lp, tc=256):
    bsz, t, _ = p.shape
    row = lambda n: pl.BlockSpec((1, n), lambda b, c: (0, 0))
    mat = lambda m, n: pl.BlockSpec((m, n), lambda b, c: (0, 0))
    tile = lambda dt: pltpu.VMEM((tc, RWKV_WIDTH), dt)
    two = pltpu.VMEM((2, tc, RWKV_WIDTH), BF16)
    return pl.pallas_call(
        functools.partial(_rwkv_prompt_kernel, tc=tc),
        grid=(bsz, t // tc),
        in_specs=[pl.BlockSpec((1, tc, RWKV_COLS), lambda b, c: (b, c, 0)),
                  row(RWKV_COLS), row(512), mat(64, 512), row(512), mat(64, 512), mat(128, 512),
                  row(512), row(512), row(512), row(512), row(512)],
        out_specs=[pl.BlockSpec((1, tc, RWKV_WIDTH), lambda b, c: (b, c, 0)),
                   pl.BlockSpec((1, N_HEADS, HEAD, HEAD), lambda b, c: (b, 0, 0, 0))],
        out_shape=[jax.ShapeDtypeStruct((bsz, t, RWKV_WIDTH), BF16),
                   jax.ShapeDtypeStruct((bsz, N_HEADS, HEAD, HEAD), F32)],
        scratch_shapes=[pltpu.VMEM((8, RWKV_COLS), F32), pltpu.VMEM((HEAD, RWKV_WIDTH), F32), two, two,
                        tile(BF16), tile(BF16), tile(BF16), tile(F32)],
        compiler_params=_params(("arbitrary", "arbitrary"), 48),
        name="rwkv_prompt",
    )(p, lp["mu_shift"], lp["w_decay0"], lp["w_decay_up"], lp["a0"], lp["w_a_up"], lp["w_g_up"], lp["k_k"],
      lp["k_a"], lp["r_k"], lp["ln_x_w"], lp["ln_x_b"])


def _rwkv_sample_prep_kernel(p_ref, prev_ref, mu_ref, w0_ref, wd_ref, a0_ref, wa_ref, wg_ref, kk_ref, ka_ref,
                             r_ref, k_ref, v_ref, g_ref, *t_refs):
    ones_bd = _block_ones(RWKV_WIDTH, 6)
    r, logw, k2, v, a, b, gate = _rwkv_prep(p_ref[...], prev_ref[...], mu_ref[...], w0_ref[...], wd_ref[...],
                                             a0_ref[...], wa_ref[...], wg_ref[...], kk_ref[...], ka_ref[...], ones_bd,
                                             True)
    r_ref[...] = r
    k_ref[...] = k2
    v_ref[...] = v
    g_ref[...] = gate
    for ref, x in zip(t_refs, (r, jnp.exp(logw), k2, v, a, b)):
        ref[...] = x.T


def _rwkv_sample_step_kernel(s_ref, r_ref, w_ref, k_ref, v_ref, a_ref, b_ref, so_ref, y_ref, *, unroll):
    a, w, b, k, r = a_ref[...], w_ref[...], b_ref[...], k_ref[...], r_ref[...]

    def rows(i0, carry):
        base = pl.multiple_of(i0 * unroll, unroll)
        v8 = v_ref[pl.ds(base, unroll), :]
        s = [s_ref[0, base + u] for u in range(unroll)]
        sa = [jnp.sum(s[u] * a, axis=0, keepdims=True) for u in range(unroll)]
        s = [s[u] * w + sa[u] * b + v8[u:u + 1, :] * k for u in range(unroll)]
        for u in range(unroll):
            so_ref[0, base + u] = s[u]
        y_ref[pl.ds(base, unroll), :] = jnp.concatenate(
            [jnp.sum(s[u] * r, axis=0, keepdims=True) for u in range(unroll)], axis=0)
        return carry

    lax.fori_loop(0, HEAD // unroll, rows, 0)


def _rwkv_sample(p, prev, wkv_t, lp, unroll=8):
    n = p.shape[0]
    full = lambda a: pl.BlockSpec(a.shape, lambda i: (0,) * a.ndim)
    names = ("mu_shift", "w_decay0", "w_decay_up", "a0", "w_a_up", "w_g_up", "k_k", "k_a")
    prep_in = [p, prev] + [lp[k] for k in names]
    vecs = pl.pallas_call(
        _rwkv_sample_prep_kernel,
        grid=(1,),
        in_specs=[full(a) for a in prep_in],
        out_specs=[pl.BlockSpec((n, RWKV_WIDTH), lambda i: (0, 0))] * 4
        + [pl.BlockSpec((RWKV_WIDTH, n), lambda i: (0, 0))] * 6,
        out_shape=[jax.ShapeDtypeStruct((n, RWKV_WIDTH), F32)] * 4
        + [jax.ShapeDtypeStruct((RWKV_WIDTH, n), F32)] * 6,
        compiler_params=_params(("arbitrary",), 32),
        name="rwkv_sample_prep",
    )(*prep_in)
    vec_t = pl.BlockSpec((HEAD, n), lambda h: (h, 0))
    st_spec = pl.BlockSpec((1, HEAD, HEAD, n), lambda h: (h, 0, 0, 0))
    wkv_new, y_t = pl.pallas_call(
        functools.partial(_rwkv_sample_step_kernel, unroll=unroll),
        grid=(N_HEADS,),
        in_specs=[st_spec] + [vec_t] * 6,
        out_specs=[st_spec, vec_t],
        out_shape=[jax.ShapeDtypeStruct(wkv_t.shape, F32), jax.ShapeDtypeStruct((RWKV_WIDTH, n), F32)],
        compiler_params=_params(("parallel",), 32),
        name="rwkv_sample_step",
    )(wkv_t, *vecs[4:])
    return vecs[:4], y_t, wkv_new


def _rope_table_kernel(invf_ref, cos_ref, sin_ref, *, rows):
    pos = (pl.program_id(0) * rows + _iota((rows, 1), 0)).astype(F32)
    cos_ref[...], sin_ref[...] = _rope_tables(pos, invf_ref[...])


def _rope_table(t, inv_freq, rows=512):
    spec = pl.BlockSpec((rows, 128), lambda i: (i, 0))
    return pl.pallas_call(
        functools.partial(_rope_table_kernel, rows=rows),
        grid=(t // rows,),
        in_specs=[pl.BlockSpec((1, 128), lambda i: (0, 0))],
        out_specs=[spec, spec],
        out_shape=[jax.ShapeDtypeStruct((t, 128), F32)] * 2,
        compiler_params=_params(("parallel",), 32),
        name="rope_table",
    )(inv_freq)


def _swa_prompt_kernel(p_ref, cos_ref, sin_ref, sink_ref, y_ref, ko_ref, vo_ref, kprev_ref, vprev_ref, *, nb):
    n = pl.program_id(1)
    blk = WINDOW

    @pl.when(n == 0)
    def _init():
        kprev_ref[...] = jnp.zeros_like(kprev_ref)
        vprev_ref[...] = jnp.zeros_like(vprev_ref)

    p = p_ref[0]
    cos, sin_signed = cos_ref[...], sin_ref[...]
    q = _rope(p[:, :RWKV_WIDTH], cos, sin_signed) * (HEAD ** -0.5)
    k = _rope(p[:, RWKV_WIDTH:RWKV_WIDTH + KV_WIDTH], cos, sin_signed)
    v = p[:, RWKV_WIDTH + KV_WIDTH:]
    kcat = jnp.concatenate([kprev_ref[...], k], axis=0)
    vcat = jnp.concatenate([vprev_ref[...], v], axis=0)
    ii, jj = _iota((blk, 2 * blk), 0), _iota((blk, 2 * blk), 1)
    in_window = (jj > ii) & (jj <= ii + blk)
    bias = jnp.concatenate([jnp.where(in_window, 0.0, NEG)] * 4, axis=0)
    lo = jnp.where(n == 0, blk, 0)
    bias_first = jnp.concatenate([jnp.where(in_window & (jj >= lo), 0.0, NEG)] * 4, axis=0)
    units = [(j, g) for j in range(nb) for g in range(2)]
    rows = lambda j, m: slice(j * blk, (j + m) * blk)
    heads = lambda g: range(4 * g, 4 * g + 4)
    cols = lambda g: slice(g * HEAD, (g + 1) * HEAD)
    sink_col = [jnp.concatenate([jnp.broadcast_to(sink_ref[0:1, h:h + 1], (blk, 1)) for h in heads(g)], axis=0)
                for g in range(2)]
    qg = {(j, g): _bf(jnp.concatenate([q[rows(j, 1), h * HEAD:(h + 1) * HEAD] for h in heads(g)], axis=0))
          for j, g in units}
    s = {(j, g): _dot_nt(qg[(j, g)], _bf(kcat[rows(j, 2), cols(g)])) + (bias_first if j == 0 else bias)
         for j, g in units}
    m = {u: jnp.maximum(jnp.max(s[u], axis=-1, keepdims=True), sink_col[u[1]]) for u in units}
    e = {u: jnp.exp(s[u] - m[u]) for u in units}
    inv = {u: 1.0 / (jnp.sum(e[u], axis=-1, keepdims=True) + jnp.exp(sink_col[u[1]] - m[u])) for u in units}
    o = {(j, g): _dot(_bf(e[(j, g)]), _bf(vcat[rows(j, 2), cols(g)])) * inv[(j, g)] for j, g in units}
    y_ref[0] = _bf(jnp.concatenate(
        [jnp.concatenate([o[(j, g)][i * blk:(i + 1) * blk] for g in range(2) for i in range(4)], axis=1)
         for j in range(nb)], axis=0))
    k_last, v_last = k[(nb - 1) * blk:], v[(nb - 1) * blk:]
    ko_ref[0] = k_last
    vo_ref[0] = v_last
    kprev_ref[...] = k_last
    vprev_ref[...] = v_last


def _swa_prompt(p, inv_freq, sink, nb=8):
    bsz, t, _ = p.shape
    blk = WINDOW
    rows = nb * blk
    cos, sin_signed = _rope_table(t, inv_freq, rows)
    table = pl.BlockSpec((rows, 128), lambda b, n: (n, 0))
    return pl.pallas_call(
        functools.partial(_swa_prompt_kernel, nb=nb),
        grid=(bsz, t // rows),
        in_specs=[pl.BlockSpec((1, rows, SWA_COLS), lambda b, n: (b, n, 0)), table, table,
                  pl.BlockSpec((1, N_HEADS), lambda b, n: (0, 0))],
        out_specs=[pl.BlockSpec((1, rows, RWKV_WIDTH), lambda b, n: (b, n, 0)),
                   pl.BlockSpec((1, blk, KV_WIDTH), lambda b, n: (b, 0, 0)),
                   pl.BlockSpec((1, blk, KV_WIDTH), lambda b, n: (b, 0, 0))],
        out_shape=[jax.ShapeDtypeStruct((bsz, t, RWKV_WIDTH), BF16),
                   jax.ShapeDtypeStruct((bsz, blk, KV_WIDTH), F32),
                   jax.ShapeDtypeStruct((bsz, blk, KV_WIDTH), F32)],
        scratch_shapes=[pltpu.VMEM((blk, KV_WIDTH), F32), pltpu.VMEM((blk, KV_WIDTH), F32)],
        compiler_params=_params(("arbitrary", "arbitrary"), 48),
        name="swa_prompt",
    )(p, cos, sin_signed, sink)


def _swa_sample_kernel(p_ref, ck_ref, cv_ref, invf_ref, sinkc_ref, y_ref, nk_ref, nv_ref, *, bt):
    p = p_ref[...]
    cos, sin_signed = _rope_tables(jnp.full((1, 1), float(PAST_LEN), F32), invf_ref[...])
    q = _rope(p[:, :RWKV_WIDTH], cos, sin_signed) * (HEAD ** -0.5)
    k = _rope(p[:, RWKV_WIDTH:RWKV_WIDTH + KV_WIDTH], cos, sin_signed)
    v = p[:, RWKV_WIDTH + KV_WIDTH:]
    k_t, v_t = k.T, v.T
    last = _iota((KV_WIDTH, WINDOW), 1) == WINDOW - 1
    head_mask = _iota((N_HEADS, RWKV_WIDTH), 0) == (_iota((N_HEADS, RWKV_WIDTH), 1) >> 6)
    tile_lanes = lambda x: jnp.concatenate([x[:, :HEAD]] * 4 + [x[:, HEAD:]] * 4, axis=1)
    tile_rows = lambda x: jnp.concatenate([x[:HEAD]] * 4 + [x[HEAD:]] * 4, axis=0)
    seqs = range(bt)
    k_new = [jnp.where(last, k_t[:, bi:bi + 1], pltpu.roll(ck_ref[bi], WINDOW - 1, axis=1)) for bi in seqs]
    v_new = [jnp.where(last, v_t[:, bi:bi + 1], pltpu.roll(cv_ref[bi], WINDOW - 1, axis=1)) for bi in seqs]
    for bi in seqs:
        nk_ref[bi] = k_new[bi]
        nv_ref[bi] = v_new[bi]
    q_bd = [jnp.where(head_mask, q[bi:bi + 1, :], 0.0) for bi in seqs]
    s = [_dot3(q_bd[bi], tile_rows(k_new[bi])) for bi in seqs]
    pr = [_softmax_sink(s[bi], sinkc_ref[...]) for bi in seqs]
    o = [_dot3_nt(pr[bi], v_new[bi]) for bi in seqs]
    y_ref[...] = jnp.concatenate(
        [jnp.sum(jnp.where(head_mask, tile_lanes(o[bi]), 0.0), axis=0, keepdims=True) for bi in seqs], axis=0)


def _swa_sample(p, ck, cv, inv_freq, sink_col, bt=8):
    n = p.shape[0]
    cache_spec = pl.BlockSpec((bt, KV_WIDTH, WINDOW), lambda i: (i, 0, 0))
    return pl.pallas_call(
        functools.partial(_swa_sample_kernel, bt=bt),
        grid=(n // bt,),
        in_specs=[pl.BlockSpec((bt, SWA_COLS), lambda i: (i, 0)), cache_spec, cache_spec,
                  pl.BlockSpec((1, 128), lambda i: (0, 0)), pl.BlockSpec((N_HEADS, 1), lambda i: (0, 0))],
        out_specs=[pl.BlockSpec((bt, RWKV_WIDTH), lambda i: (i, 0)), cache_spec, cache_spec],
        out_shape=[jax.ShapeDtypeStruct((n, RWKV_WIDTH), F32), jax.ShapeDtypeStruct(ck.shape, F32),
                   jax.ShapeDtypeStruct(cv.shape, F32)],
        compiler_params=_params(("parallel",), 32),
        name="swa_sample",
    )(p, ck, cv, inv_freq, sink_col)


def _mix_out(yr, ys, x, wo1, wo2):
    return x + _dot(yr, wo1) + _dot(ys, wo2)


def _out_cross_kernel(yr_ref, ys_ref, x_ref, wo1_ref, wo2_ref, gc_ref, wq_ref, mk_ref, mv_ref, wco_ref, o_ref):
    x1 = _mix_out(yr_ref[0], ys_ref[0], x_ref[0], wo1_ref[...], wo2_ref[...])
    q = _bf(_dot(_bf(_rmsnorm(x1, gc_ref[...])), wq_ref[...]))
    mk, mv = mk_ref[0], mv_ref[0]
    heads = range(X_HEADS)
    sl = lambda h: slice(h * X_HEAD, (h + 1) * X_HEAD)
    s = [_dot_nt(q[:, sl(h)], mk[:, sl(h)]) * (X_HEAD ** -0.5) for h in heads]
    e = [jnp.exp(s[h] - jnp.max(s[h], axis=-1, keepdims=True)) for h in heads]
    outs = [_dot(_bf(e[h]), mv[:, sl(h)]) / jnp.sum(e[h], axis=-1, keepdims=True) for h in heads]
    o_ref[0] = x1 + _dot(_bf(jnp.concatenate(outs, axis=1)), wco_ref[...])


def _out_cross(yr, ys, x, wo1, wo2, gain, wq, mk, mv, wco, tm=1024):
    bsz, t, d = x.shape
    tile = lambda w: pl.BlockSpec((1, tm, w), lambda b, i: (b, i, 0))
    const = lambda a: pl.BlockSpec(a.shape, lambda b, i: (0,) * a.ndim)
    mem = pl.BlockSpec((1, MEM_TOKENS, X_WIDTH), lambda b, i: (b, 0, 0))
    return pl.pallas_call(
        _out_cross_kernel,
        grid=(bsz, t // tm),
        in_specs=[tile(RWKV_WIDTH), tile(RWKV_WIDTH), tile(d), const(wo1), const(wo2), const(gain), const(wq),
                  mem, mem, const(wco)],
        out_specs=tile(d),
        out_shape=jax.ShapeDtypeStruct((bsz, t, d), F32),
        compiler_params=_params(("parallel", "parallel"), 48),
        name="out_cross",
    )(yr, ys, x, wo1, wo2, gain, wq, mk, mv, wco)


def _out_q_kernel(yt_ref, r_ref, k_ref, v_ref, g_ref, rk_ref, lnw_ref, lnb_ref, ys_ref, x_ref, wo1_ref, wo2_ref,
                  gc_ref, wq_ref, x1_ref, q_ref):
    yr = _rwkv_post(yt_ref[...].T, r_ref[...], k_ref[...], v_ref[...], g_ref[...], rk_ref[...], lnw_ref[...],
                    lnb_ref[...], _block_ones(RWKV_WIDTH, 6), True)
    x1 = x_ref[...] + _dot3(yr, wo1_ref[...]) + _dot3(ys_ref[...], wo2_ref[...])
    x1_ref[...] = x1
    q_ref[...] = _dot3(_rmsnorm(x1, gc_ref[...]), wq_ref[...])


def _cross_sample_kernel(q_ref, mk_ref, mv_ref, o_ref, *, bt):
    seqs = range(bt)
    both = lambda x: x + pltpu.roll(x, X_HEADS, axis=0)
    q8 = [jnp.concatenate([q_ref[bi]] * 2, axis=0) for bi in seqs]
    s = [jnp.sum(mk_ref[bi] * q8[bi][None], axis=-1, keepdims=True) * (X_HEAD ** -0.5) for bi in seqs]
    mx = [jnp.max(s[bi], axis=0) for bi in seqs]
    mx = [jnp.maximum(m, pltpu.roll(m, X_HEADS, axis=0)) for m in mx]
    e = [jnp.exp(s[bi] - mx[bi][None]) for bi in seqs]
    den = [both(jnp.sum(e[bi], axis=0)) for bi in seqs]
    o8 = [jnp.sum(e[bi] * mv_ref[bi], axis=0) / den[bi] for bi in seqs]
    for bi in seqs:
        o_ref[bi] = both(o8[bi])[:X_HEADS]


def _lin_res_kernel(x_ref, a_ref, w_ref, o_ref):
    o_ref[...] = x_ref[...] + _dot3(a_ref[...], w_ref[...])


def _single_step(kernel_fn, out_shape, name, *args):
    full = lambda a: pl.BlockSpec(a.shape, lambda i: (0,) * len(a.shape))
    outs = out_shape if isinstance(out_shape, (list, tuple)) else [out_shape]
    out_specs = [full(o) for o in outs]
    return pl.pallas_call(
        kernel_fn, grid=(1,), in_specs=[full(a) for a in args],
        out_specs=out_specs if isinstance(out_shape, (list, tuple)) else out_specs[0],
        out_shape=out_shape, compiler_params=_params(("arbitrary",), 32), name=name,
    )(*args)


def _out_cross_sample(rwkv_parts, ys, x, wo1, wo2, gain, wq, mk, mv, wco, bt=4):
    n, d = x.shape
    x1, q = _single_step(_out_q_kernel, [jax.ShapeDtypeStruct((n, d), F32), jax.ShapeDtypeStruct((n, X_WIDTH), F32)],
                         "out_q_sample", *rwkv_parts, ys, x, wo1, wo2, gain, wq)
    mem_shape = (n, MEM_TOKENS // 2, 2 * X_HEADS, X_HEAD)
    mem = pl.BlockSpec((bt,) + mem_shape[1:], lambda i: (i, 0, 0, 0))
    vec = pl.BlockSpec((bt, X_HEADS, X_HEAD), lambda i: (i, 0, 0))
    o = pl.pallas_call(
        functools.partial(_cross_sample_kernel, bt=bt),
        grid=(n // bt,),
        in_specs=[vec, mem, mem],
        out_specs=vec,
        out_shape=jax.ShapeDtypeStruct((n, X_HEADS, X_HEAD), F32),
        compiler_params=_params(("parallel",), 48),
        name="cross_sample",
    )(q.reshape(n, X_HEADS, X_HEAD), mk.reshape(mem_shape), mv.reshape(mem_shape))
    return _single_step(_lin_res_kernel, jax.ShapeDtypeStruct((n, d), F32), "cross_out_sample", x1,
                        o.reshape(n, X_WIDTH), wco)


def _route(logits):
    lane_i = _iota(logits.shape, 1)
    lane = lane_i.astype(F32)
    big = 1024.0
    n_exp = float(N_GROUPS * EXPERTS_PER_GROUP)
    lg = jnp.where((lane >= n_exp) & (lane < n_exp + N_GROUPS), logits, NEG)
    g_max = jnp.max(lg, axis=-1, keepdims=True)
    g_idx = jnp.min(jnp.where(lg == g_max, lane, big), axis=-1, keepdims=True) - n_exp
    p_sel = 1.0 / jnp.sum(jnp.exp(lg - g_max), axis=-1, keepdims=True)
    le = jnp.where((lane_i >> 3).astype(F32) == g_idx, logits, NEG)
    t1 = jnp.max(le, axis=-1, keepdims=True)
    i1 = jnp.min(jnp.where(le == t1, lane, big), axis=-1, keepdims=True)
    le2 = jnp.where(lane == i1, NEG, le)
    t2 = jnp.max(le2, axis=-1, keepdims=True)
    i2 = jnp.min(jnp.where(le2 == t2, lane, big), axis=-1, keepdims=True)
    e2 = jnp.exp(t2 - t1)
    w1 = 1.0 / (1.0 + e2)
    return p_sel * (jnp.where(lane == i1, w1, 0.0) + jnp.where(lane == i2, e2 * w1, 0.0)), g_idx


def _moe_kernel(x_ref, gf_ref, wr_ref, br_ref, wg_ref, wu_ref, wd_ref, gfin_ref, tri_ref, o_ref, h_ref, gate_ref,
                pos_ref, acc_ref, *, tm, blocks, precise):
    g = pl.program_id(1)

    @pl.when(g == 0)
    def _init():
        h = _rmsnorm(x_ref[...], gf_ref[...])
        h_ref[...] = _bf(h)
        gate, g_idx = _route((_dot3 if precise else _dot1)(h, wr_ref[...]) + br_ref[...])
        gate_ref[...] = gate
        lane = _iota((tm, 128), 1).astype(F32)
        own = lane == g_idx
        count = _dot(tri_ref[...], _mask_bf(own))
        pos_ref[...] = jnp.where(own, count, -1.0)
        acc_ref[...] = jnp.zeros_like(acc_ref)

    lane = _iota((tm, 128), 1)
    pos = jnp.sum(jnp.where(lane == g, pos_ref[...], 0.0), axis=-1, keepdims=True)
    n_g = jnp.sum(jnp.where(pos >= 0.0, 1.0, 0.0))
    width = EXPERTS_PER_GROUP * EXPERT_HIDDEN
    expand = _mask_bf(_iota((128, width), 0) == g * EXPERTS_PER_GROUP + (_iota((128, width), 1) >> 7))
    start = 0
    for bm in blocks:
        @pl.when(n_g > start)
        def _block(start=start, bm=bm):
            sel = _mask_bf(pos - float(start) == _iota((tm, bm), 1).astype(F32))
            hs = _bf(_dot_tn(sel, h_ref[...]))
            gh, gl = _split(gate_ref[...])
            gs = _dot_tn(sel, gh) + _dot_tn(sel, gl) if precise else _dot_tn(sel, gh)
            hg = _dot(hs, wg_ref[0])
            hu = _dot(hs, wu_ref[0])
            act = hg * _sigmoid(hg) * hu * (_dot_sel_rhs if precise else _dot1)(gs, expand)
            acc_ref[...] += _dot(sel, _bf(_dot(_bf(act), wd_ref[0])))

        start += bm

    @pl.when(g == N_GROUPS - 1)
    def _fin():
        o_ref[...] = _rmsnorm(x_ref[...] + acc_ref[...], gfin_ref[...])


def _moe(x, gain, w_router, b_router, wg, wu, wd, gain_final, tm, blocks, precise):
    n, d = x.shape
    assert sum(blocks) == tm
    width = EXPERTS_PER_GROUP * EXPERT_HIDDEN
    const = lambda a: pl.BlockSpec(a.shape, lambda i, g: (0,) * a.ndim)
    earlier = jnp.tril(jnp.ones((tm, tm), BF16), -1)
    return pl.pallas_call(
        functools.partial(_moe_kernel, tm=tm, blocks=blocks, precise=precise),
        grid=(n // tm, N_GROUPS),
        in_specs=[pl.BlockSpec((tm, d), lambda i, g: (i, 0)), const(gain), const(w_router), const(b_router),
                  pl.BlockSpec((1, d, width), lambda i, g: (g, 0, 0)),
                  pl.BlockSpec((1, d, width), lambda i, g: (g, 0, 0)),
                  pl.BlockSpec((1, width, d), lambda i, g: (g, 0, 0)), const(gain_final), const(earlier)],
        out_specs=pl.BlockSpec((tm, d), lambda i, g: (i, 0)),
        out_shape=jax.ShapeDtypeStruct((n, d), F32),
        scratch_shapes=[pltpu.VMEM((tm, d), BF16), pltpu.VMEM((tm, 128), F32), pltpu.VMEM((tm, 128), F32),
                        pltpu.VMEM((tm, d), F32)],
        compiler_params=_params(("parallel", "arbitrary"), 60),
        name="moe",
    )(x, gain, w_router, b_router, wg, wu, wd, gain_final, earlier)


def kernel(x_prompt, x_sample, mem_prompt, state_rwkv_shift, state_rwkv_wkv, cache_swa_k, cache_swa_v, cache_mem_k, cache_mem_v, norm_mix, w_in, mu_shift, w_decay0, w_decay_up, a0, w_a_up, w_g_up, k_k, k_a, r_k, ln_x_w, ln_x_b, attn_sink, w_out, norm_cross, norm_mem, w_cq, w_ck, w_cv, w_co, norm_ffn, w_group_router, b_group_router, w_expert_router, b_expert_router, w_exp_gate, w_exp_up, w_exp_down, norm_final):
    bsz, t, d = x_prompt.shape
    n_s = x_sample.shape[0]
    wb = cache_swa_k.shape[2]
    width = EXPERTS_PER_GROUP * EXPERT_HIDDEN

    w_in_bf = w_in[0].astype(BF16)
    wo1, wo2 = w_out[0, :RWKV_WIDTH].astype(BF16), w_out[0, RWKV_WIDTH:].astype(BF16)
    wq, wco = w_cq[0].astype(BF16), w_co[0].astype(BF16)
    w_ckv = jnp.concatenate([w_ck[0], w_cv[0]], axis=1).astype(BF16)
    w_router = jnp.pad(jnp.concatenate([w_expert_router[0], w_group_router[0]], axis=1), ((0, 0), (0, 128 - 36)))
    b_router = jnp.pad(jnp.concatenate([b_expert_router[0], b_group_router[0]]), (0, 128 - 36)).reshape(1, 128)
    wg = jnp.transpose(w_exp_gate[0], (0, 2, 1, 3)).reshape(N_GROUPS, d, width).astype(BF16)
    wu = jnp.transpose(w_exp_up[0], (0, 2, 1, 3)).reshape(N_GROUPS, d, width).astype(BF16)
    wd = w_exp_down[0].reshape(N_GROUPS, width, d).astype(BF16)
    row = lambda a: a.reshape(1, -1)
    lp = dict(mu_shift=mu_shift, w_decay0=w_decay0, w_decay_up=w_decay_up[0], a0=a0, w_a_up=w_a_up[0],
              w_g_up=w_g_up[0], k_k=k_k, k_a=k_a, r_k=row(r_k), ln_x_w=ln_x_w, ln_x_b=ln_x_b)
    half = HEAD // 2
    inv_freq = ROPE_THETA ** (-jnp.arange(half, dtype=F32) * 2.0 / HEAD)
    inv_freq = jnp.tile(inv_freq, 128 // half).reshape(1, 128)

    p_rwkv, p_swa = _norm_matmul(x_prompt.reshape(bsz * t, d), norm_mix, w_in_bf, RWKV_COLS, 1024)
    p_rwkv = p_rwkv.reshape(bsz, t, RWKV_COLS)
    y_rwkv, wkv_p = _rwkv_prompt(p_rwkv, lp)
    y_swa, k_p, v_p = _swa_prompt(p_swa.reshape(bsz, t, SWA_COLS), inv_freq, attn_sink)
    mk_p, mv_p = _norm_matmul(mem_prompt.reshape(bsz * MEM_TOKENS, d), norm_mem, w_ckv, X_WIDTH, 256)
    mk_b = mk_p.astype(BF16).reshape(bsz, MEM_TOKENS, X_WIDTH)
    mv_b = mv_p.astype(BF16).reshape(bsz, MEM_TOKENS, X_WIDTH)
    x2 = _out_cross(y_rwkv, y_swa, x_prompt, wo1, wo2, norm_cross, wq, mk_b, mv_b, wco)
    y_prompt = _moe(x2.reshape(bsz * t, d), norm_ffn, w_router, b_router, wg, wu, wd, row(norm_final), 1024,
                    (256, 128, 128, 256, 256), False)

    xs = x_sample.reshape(n_s, d)
    ps_rwkv, ps_swa = _norm_matmul(xs, norm_mix, w_in[0], RWKV_COLS, n_s)
    (r_s, k2_s, v_s_tok, g_s), yt_s, wkv_s = _rwkv_sample(ps_rwkv, state_rwkv_shift[0],
                                                           jnp.transpose(state_rwkv_wkv[0], (1, 2, 3, 0)), lp)
    cache_t = lambda c: jnp.transpose(c[0].reshape(n_s, wb, KV_WIDTH), (0, 2, 1))
    ys_swa, k_s, v_s = _swa_sample(ps_swa, cache_t(cache_swa_k), cache_t(cache_swa_v), inv_freq,
                                   attn_sink.reshape(N_HEADS, 1))
    x2s = _out_cross_sample((yt_s, r_s, k2_s, v_s_tok, g_s, lp["r_k"], lp["ln_x_w"], lp["ln_x_b"]), ys_swa, xs,
                            w_out[0, :RWKV_WIDTH], w_out[0, RWKV_WIDTH:], norm_cross, w_cq[0],
                            cache_mem_k[0], cache_mem_v[0], w_co[0])
    y_sample = _moe(x2s, norm_ffn, w_router, b_router, wg, wu, wd, row(norm_final), n_s, (n_s,), True)

    kv_shape = (1, -1, wb, KV_WIDTH // HEAD, HEAD)
    mem_shape = (1, bsz, MEM_TOKENS, X_HEADS, X_HEAD)
    cache_back = lambda c: jnp.transpose(c, (0, 2, 1)).reshape(kv_shape)
    return (y_prompt.reshape(bsz, t, d), y_sample.reshape(n_s, 1, d),
            p_rwkv[:, -1][None], wkv_p[None], k_p.reshape(kv_shape), v_p.reshape(kv_shape),
            mk_p.reshape(mem_shape), mv_p.reshape(mem_shape),
            ps_rwkv[None], jnp.transpose(wkv_s, (3, 0, 1, 2))[None], cache_back(k_s), cache_back(v_s))
```

```python
import functools

import jax
import jax.numpy as jnp
from jax import lax
from jax.experimental import pallas as pl
from jax.experimental.pallas import tpu as pltpu

F32 = jnp.float32
BF16 = jnp.bfloat16

D_MODEL = 1024
RWKV_WIDTH = 512
HEAD = 64
N_HEADS = 8
RWKV_COLS = 1792
SWA_COLS = 768
KV_WIDTH = 128
WINDOW = 128
PAST_LEN = 8192
ROPE_THETA = 10000.0
MEM_TOKENS = 256
X_HEADS = 4
X_HEAD = 128
X_WIDTH = 512
N_GROUPS = 4
EXPERTS_PER_GROUP = 8
EXPERT_HIDDEN = 128
NORM_EPS = 1e-5
GN_EPS = 64e-5
L2_EPS = 1e-12
CHUNK = 64
NEG = -1e30
MIB = 1024 * 1024


def _params(semantics, vmem_mib):
    return pltpu.CompilerParams(dimension_semantics=semantics, vmem_limit_bytes=vmem_mib * MIB)


def _bf(x):
    return x.astype(BF16)


def _dot(a, b):
    return jnp.dot(a, b, preferred_element_type=F32)


def _dot_nt(a, b):
    return lax.dot_general(a, b, (((1,), (1,)), ((), ())), preferred_element_type=F32)


def _dot_tn(a, b):
    return lax.dot_general(a, b, (((0,), (0,)), ((), ())), preferred_element_type=F32)


def _split(x):
    hi = x.astype(BF16)
    lo = (x - hi.astype(F32)).astype(BF16)
    return hi, lo


def _dot3(a, b, dot=_dot):
    ah, al = _split(a)
    bh, bl = _split(b)
    return dot(ah, bh) + (dot(ah, bl) + dot(al, bh))


def _dot3_nt(a, b):
    return _dot3(a, b, _dot_nt)


def _dot_sel_rhs(a, sel):
    ah, al = _split(a)
    return _dot(ah, sel) + _dot(al, sel)


def _dot_sel_lhs(sel, b):
    bh, bl = _split(b)
    return _dot(sel, bh) + _dot(sel, bl)


def _rmsnorm(x, gain):
    return x * lax.rsqrt(jnp.mean(x * x, axis=-1, keepdims=True) + NORM_EPS) * gain


def _sigmoid(x):
    return 1.0 / (1.0 + jnp.exp(-x))


def _iota(shape, dim):
    return lax.broadcasted_iota(jnp.int32, shape, dim)


def _mask_bf(cond):
    return jnp.where(cond, 1.0, 0.0).astype(BF16)


def _block_ones(n, blk_shift):
    return _mask_bf((_iota((n, n), 0) >> blk_shift) == (_iota((n, n), 1) >> blk_shift))


def _softmax_sink(s, sink_col):
    m = jnp.maximum(jnp.max(s, axis=-1, keepdims=True), sink_col)
    e = jnp.exp(s - m)
    den = jnp.sum(e, axis=-1, keepdims=True) + jnp.exp(sink_col - m)
    return e / den


def _rope(x, cos, sin_signed, on_mxu=False):
    width = x.shape[1]
    reps = width // 128
    if on_mxu:
        src, dst = _iota((128, 128), 0), _iota((128, 128), 1)
        partner = jnp.where((dst & (HEAD - 1)) < HEAD // 2, dst + HEAD // 2, dst - HEAD // 2)
        swap = _mask_bf(src == partner)
        rot = jnp.concatenate([_dot(_bf(x[:, c * 128:(c + 1) * 128]), swap) for c in range(reps)], axis=1)
    else:
        first_half = (_iota(x.shape, 1) & (HEAD - 1)) < HEAD // 2
        rot = jnp.where(first_half, pltpu.roll(x, width - HEAD // 2, axis=1), pltpu.roll(x, HEAD // 2, axis=1))
    if reps > 1:
        cos = jnp.concatenate([cos] * reps, axis=1)
        sin_signed = jnp.concatenate([sin_signed] * reps, axis=1)
    return x * cos + rot * sin_signed


def _rope_tables(pos, inv_freq):
    ang = pos * inv_freq
    cos, sin = jnp.cos(ang), jnp.sin(ang)
    first_half = (_iota(ang.shape, 1) & (HEAD - 1)) < HEAD // 2
    return cos, jnp.where(first_half, -sin, sin)


def _norm_matmul_kernel(x_ref, g_ref, w_ref, o1_ref, o2_ref, *, split):
    h = _rmsnorm(x_ref[...], g_ref[...])
    w = w_ref[...]
    p = _dot(_bf(h), w) if w.dtype == BF16 else _dot3(h, w)
    o1_ref[...] = p[:, :split]
    o2_ref[...] = p[:, split:]


def _norm_matmul(x, gain, w, split, tm):
    n, d = x.shape
    cols = w.shape[1]
    return pl.pallas_call(
        functools.partial(_norm_matmul_kernel, split=split),
        grid=(n // tm,),
        in_specs=[pl.BlockSpec((tm, d), lambda i: (i, 0)),
                  pl.BlockSpec((1, d), lambda i: (0, 0)),
                  pl.BlockSpec((d, cols), lambda i: (0, 0))],
        out_specs=[pl.BlockSpec((tm, split), lambda i: (i, 0)),
                   pl.BlockSpec((tm, cols - split), lambda i: (i, 0))],
        out_shape=[jax.ShapeDtypeStruct((n, split), F32), jax.ShapeDtypeStruct((n, cols - split), F32)],
        compiler_params=_params(("parallel",), 48),
        name="norm_matmul",
    )(x, gain, w)


def _dot1(a, b):
    return _dot(_bf(a), _bf(b))


def _rwkv_prep(p, prev, mu, w0, wd, a0, wa, wg, k_k, k_a, ones_bd, precise):
    mm, head_sum = (_dot3, _dot_sel_rhs) if precise else (_dot1, _dot1)
    c = RWKV_WIDTH
    xm = p + (prev - p) * mu
    r, k, v = xm[:, :c], xm[:, c:2 * c], xm[:, 2 * c:3 * c]
    xw, xa, xg = xm[:, 3 * c:3 * c + 64], xm[:, 3 * c + 64:3 * c + 128], xm[:, 3 * c + 128:]
    z = -(w0 + _dot3(jnp.tanh(xw), wd))
    w = -(jnp.maximum(z, 0.0) + jnp.log(1.0 + jnp.exp(-jnp.abs(z)))) - 0.5
    logw = -jnp.exp(w)
    rate = _sigmoid(a0 + mm(xa, wa))
    gate = mm(_sigmoid(xg), wg)
    kk = k * k_k
    kk = kk * lax.rsqrt(jnp.maximum(head_sum(kk * kk, ones_bd), L2_EPS * L2_EPS))
    k2 = k * (1.0 + (rate - 1.0) * k_a)
    return r, logw, k2, v, -kk, kk * rate, gate


def _rwkv_post(y, r, k2, v, gate, r_k, ln_w, ln_b, ones_bd, precise):
    head_sum = _dot_sel_rhs if precise else _dot1
    mean = head_sum(y, ones_bd) * (1.0 / HEAD)
    d = y - mean
    var = head_sum(d * d, ones_bd) * (1.0 / HEAD)
    yn = d * lax.rsqrt(var + GN_EPS) * ln_w + ln_b
    bonus = head_sum(r * k2 * r_k, ones_bd) * v
    return (yn + bonus) * gate


def _rwkv_prompt_kernel(p_ref, mu_ref, w0_ref, wd_ref, a0_ref, wa_ref, wg_ref, kk_ref, ka_ref, rk_ref, lnw_ref,
                        lnb_ref, y_ref, s_ref, carry_ref, st_ref, ar_ref, bk_ref, bp_ref, kp_ref, v_ref, yacc_ref,
                        *, tc):
    c = pl.program_id(1)

    @pl.when(c == 0)
    def _init():
        carry_ref[...] = jnp.zeros_like(carry_ref)
        st_ref[...] = jnp.zeros_like(st_ref)

    p = p_ref[0]
    first_row = _iota((tc, 1), 0) == 0
    prev = jnp.where(first_row, carry_ref[0:1, :], pltpu.roll(p, 1, axis=0))
    carry_ref[0:1, :] = p[tc - 1:tc, :]
    ones_bd = _block_ones(RWKV_WIDTH, 6)
    r, logw, k2, v, a, b, gate = _rwkv_prep(p, prev, mu_ref[...], w0_ref[...], wd_ref[...], a0_ref[...],
                                             wa_ref[...], wg_ref[...], kk_ref[...], ka_ref[...], ones_bd, False)
    ri, ci = _iota((tc, tc), 0), _iota((tc, tc), 1)
    same_chunk = (ri >> 6) == (ci >> 6)
    cum = _dot_sel_lhs(_mask_bf(same_chunk & (ci <= ri)), logw)
    n_ch = tc // CHUNK
    p_end = [jnp.exp(cum[(ci_ + 1) * CHUNK - 1:(ci_ + 1) * CHUNK, :]) for ci_ in range(n_ch)]
    e_inv = jnp.exp(-cum)
    e_end = e_inv * jnp.concatenate([jnp.broadcast_to(pe, (CHUNK, RWKV_WIDTH)) for pe in p_end], axis=0)
    ar_ref[0], ar_ref[1] = _bf(a * jnp.exp(cum - logw)), _bf(r * jnp.exp(cum))
    bk_ref[0], bk_ref[1] = _bf(b * e_inv), _bf(k2 * e_inv)
    bp_ref[...] = _bf(b * e_end)
    kp_ref[...] = _bf(k2 * e_end)
    v_ref[...] = _bf(v)

    rr, cc = _iota((CHUNK, 2 * HEAD), 0), _iota((CHUNK, 2 * HEAD), 1) & (HEAD - 1)
    strict, incl, eye = cc < rr, cc <= rr, cc == rr
    lane = _iota((CHUNK, 2 * HEAD), 1)
    left, right = lane < HEAD, lane >= HEAD
    zero = jnp.zeros((), BF16)

    def stack(x):
        return jnp.concatenate([jnp.where(left, x, zero.astype(x.dtype)), jnp.where(right, x, zero.astype(x.dtype))],
                               axis=0)

    def diag_blocks(x):
        return jnp.where(left, x[:HEAD], x[HEAD:])

    n_ch, n_pairs = tc // CHUNK, N_HEADS // 2
    ids = [(ci_, pr) for ci_ in range(n_ch) for pr in range(n_pairs)]
    rows = lambda ci_: slice(ci_ * CHUNK, (ci_ + 1) * CHUNK)
    cols = lambda pr: slice(pr * 2 * HEAD, (pr + 1) * 2 * HEAD)
    at = {i: ar_ref[0, rows(i[0]), cols(i[1])] for i in ids}
    rt = {i: ar_ref[1, rows(i[0]), cols(i[1])] for i in ids}
    vv = {i: v_ref[rows(i[0]), cols(i[1])] for i in ids}
    bp = {i: bp_ref[rows(i[0]), cols(i[1])] for i in ids}
    g = {i: _dot_nt(jnp.concatenate([at[i], rt[i]], axis=0),
                    jnp.concatenate([stack(bk_ref[0, rows(i[0]), cols(i[1])]),
                                     stack(bk_ref[1, rows(i[0]), cols(i[1])])], axis=0)) for i in ids}
    vk = {i: diag_blocks(_dot_tn(vv[i], kp_ref[rows(i[0]), cols(i[1])])) for i in ids}
    a_ab = {i: jnp.where(strict, g[i][:CHUNK, :2 * HEAD], 0.0) for i in ids}
    a_rb = {i: _bf(jnp.where(incl, g[i][CHUNK:, :2 * HEAD], 0.0)) for i in ids}
    av = {i: _dot(_bf(jnp.concatenate([jnp.where(strict, g[i][:CHUNK, 2 * HEAD:], 0.0),
                                       jnp.where(incl, g[i][CHUNK:, 2 * HEAD:], 0.0)], axis=0)), stack(vv[i]))
          for i in ids}
    t = {i: jnp.where(eye, 1.0, a_ab[i]) for i in ids}
    apow = {i: _bf(a_ab[i]) for i in ids}
    apow = {i: _bf(_dot(apow[i], stack(apow[i]))) for i in ids}
    for step in range(4):
        both = {i: _dot(jnp.concatenate([_bf(t[i]), apow[i]], axis=0), stack(apow[i])) for i in ids}
        t = {i: t[i] + both[i][:CHUNK] for i in ids}
        apow = {i: _bf(both[i][CHUNK:]) for i in ids}
    t = {i: t[i] + _dot(_bf(t[i]), stack(apow[i])) for i in ids}
    wu = {i: _bf(_dot(_bf(t[i]), jnp.concatenate([stack(at[i]), stack(_bf(av[i][:CHUNK]))], axis=1)))
          for i in ids}
    wy_y0 = {i: jnp.concatenate([rt[i].astype(F32), av[i][CHUNK:]], axis=1)
             + _dot(a_rb[i], jnp.concatenate([stack(wu[i][:, :2 * HEAD]), stack(wu[i][:, 2 * HEAD:])], axis=1))
             for i in ids}
    mbn = {i: _dot_tn(wu[i], bp[i]) for i in ids}
    rb, cb = _iota((2 * HEAD, 2 * HEAD), 0) < HEAD, _iota((2 * HEAD, 2 * HEAD), 1) < HEAD
    mb = {i: _bf(jnp.where(rb == cb, mbn[i][:2 * HEAD], 0.0)) for i in ids}
    n0 = {i: vk[i] + diag_blocks(mbn[i][2 * HEAD:]) for i in ids}
    state = [st_ref[:, cols(pr)] for pr in range(n_pairs)]
    for ci_ in range(n_ch):
        sb = [_bf(s) for s in state]
        ys = [wy_y0[(ci_, pr)][:, 2 * HEAD:] + _dot_nt(_bf(wy_y0[(ci_, pr)][:, :2 * HEAD]), stack(sb[pr]))
              for pr in range(n_pairs)]
        state = [state[pr] * p_end[ci_][:, cols(pr)] + _dot(sb[pr], mb[(ci_, pr)])
                 + n0[(ci_, pr)] for pr in range(n_pairs)]
        for pr in range(n_pairs):
            yacc_ref[rows(ci_), cols(pr)] = ys[pr]
    for pr in range(n_pairs):
        st_ref[:, cols(pr)] = state[pr]
    y_ref[0] = _bf(_rwkv_post(yacc_ref[...], r, k2, v, gate, rk_ref[...], lnw_ref[...], lnb_ref[...], ones_bd,
                              False))
    for h in range(N_HEADS):
        s_ref[0, h] = state[h // 2][:, (h % 2) * HEAD:(h % 2 + 1) * HEAD]


def _rwkv_prompt(p, lp, tc=256):
    bsz, t, _ = p.shape
    row = lambda n: pl.BlockSpec((1, n), lambda b, c: (0, 0))
    mat = lambda m, n: pl.BlockSpec((m, n), lambda b, c: (0, 0))
    tile = lambda dt: pltpu.VMEM((tc, RWKV_WIDTH), dt)
    two = pltpu.VMEM((2, tc, RWKV_WIDTH), BF16)
    return pl.pallas_call(
        functools.partial(_rwkv_prompt_kernel, tc=tc),
        grid=(bsz, t // tc),
        in_specs=[pl.BlockSpec((1, tc, RWKV_COLS), lambda b, c: (b, c, 0)),
                  row(RWKV_COLS), row(512), mat(64, 512), row(512), mat(64, 512), mat(128, 512),
                  row(512), row(512), row(512), row(512), row(512)],
        out_specs=[pl.BlockSpec((1, tc, RWKV_WIDTH), lambda b, c: (b, c, 0)),
                   pl.BlockSpec((1, N_HEADS, HEAD, HEAD), lambda b, c: (b, 0, 0, 0))],
        out_shape=[jax.ShapeDtypeStruct((bsz, t, RWKV_WIDTH), BF16),
                   jax.ShapeDtypeStruct((bsz, N_HEADS, HEAD, HEAD), F32)],
        scratch_shapes=[pltpu.VMEM((8, RWKV_COLS), F32), pltpu.VMEM((HEAD, RWKV_WIDTH), F32), two, two,
                        tile(BF16), tile(BF16), tile(BF16), tile(F32)],
        compiler_params=_params(("arbitrary", "arbitrary"), 48),
        name="rwkv_prompt",
    )(p, lp["mu_shift"], lp["w_decay0"], lp["w_decay_up"], lp["a0"], lp["w_a_up"], lp["w_g_up"], lp["k_k"],
      lp["k_a"], lp["r_k"], lp["ln_x_w"], lp["ln_x_b"])


def _rwkv_sample_prep_kernel(p_ref, prev_ref, mu_ref, w0_ref, wd_ref, a0_ref, wa_ref, wg_ref, kk_ref, ka_ref,
                             r_ref, k_ref, v_ref, g_ref, *t_refs):
    ones_bd = _block_ones(RWKV_WIDTH, 6)
    r, logw, k2, v, a, b, gate = _rwkv_prep(p_ref[...], prev_ref[...], mu_ref[...], w0_ref[...], wd_ref[...],
                                             a0_ref[...], wa_ref[...], wg_ref[...], kk_ref[...], ka_ref[...], ones_bd,
                                             True)
    r_ref[...] = r
    k_ref[...] = k2
    v_ref[...] = v
    g_ref[...] = gate
    for ref, x in zip(t_refs, (r, jnp.exp(logw), k2, v, a, b)):
        ref[...] = x.T


def _rwkv_sample_step_kernel(s_ref, r_ref, w_ref, k_ref, v_ref, a_ref, b_ref, so_ref, y_ref, *, unroll):
    a, w, b, k, r = a_ref[...], w_ref[...], b_ref[...], k_ref[...], r_ref[...]

    def rows(i0, carry):
        base = pl.multiple_of(i0 * unroll, unroll)
        v8 = v_ref[pl.ds(base, unroll), :]
        s = [s_ref[0, base + u] for u in range(unroll)]
        sa = [jnp.sum(s[u] * a, axis=0, keepdims=True) for u in range(unroll)]
        s = [s[u] * w + sa[u] * b + v8[u:u + 1, :] * k for u in range(unroll)]
        for u in range(unroll):
            so_ref[0, base + u] = s[u]
        y_ref[pl.ds(base, unroll), :] = jnp.concatenate(
            [jnp.sum(s[u] * r, axis=0, keepdims=True) for u in range(unroll)], axis=0)
        return carry

    lax.fori_loop(0, HEAD // unroll, rows, 0)


def _rwkv_sample(p, prev, wkv_t, lp, unroll=8):
    n = p.shape[0]
    full = lambda a: pl.BlockSpec(a.shape, lambda i: (0,) * a.ndim)
    names = ("mu_shift", "w_decay0", "w_decay_up", "a0", "w_a_up", "w_g_up", "k_k", "k_a")
    prep_in = [p, prev] + [lp[k] for k in names]
    vecs = pl.pallas_call(
        _rwkv_sample_prep_kernel,
        grid=(1,),
        in_specs=[full(a) for a in prep_in],
        out_specs=[pl.BlockSpec((n, RWKV_WIDTH), lambda i: (0, 0))] * 4
        + [pl.BlockSpec((RWKV_WIDTH, n), lambda i: (0, 0))] * 6,
        out_shape=[jax.ShapeDtypeStruct((n, RWKV_WIDTH), F32)] * 4
        + [jax.ShapeDtypeStruct((RWKV_WIDTH, n), F32)] * 6,
        compiler_params=_params(("arbitrary",), 32),
        name="rwkv_sample_prep",
    )(*prep_in)
    vec_t = pl.BlockSpec((HEAD, n), lambda h: (h, 0))
    st_spec = pl.BlockSpec((1, HEAD, HEAD, n), lambda h: (h, 0, 0, 0))
    wkv_new, y_t = pl.pallas_call(
        functools.partial(_rwkv_sample_step_kernel, unroll=unroll),
        grid=(N_HEADS,),
        in_specs=[st_spec] + [vec_t] * 6,
        out_specs=[st_spec, vec_t],
        out_shape=[jax.ShapeDtypeStruct(wkv_t.shape, F32), jax.ShapeDtypeStruct((RWKV_WIDTH, n), F32)],
        compiler_params=_params(("parallel",), 32),
        name="rwkv_sample_step",
    )(wkv_t, *vecs[4:])
    return vecs[:4], y_t, wkv_new


def _rope_table_kernel(invf_ref, cos_ref, sin_ref, *, rows):
    pos = (pl.program_id(0) * rows + _iota((rows, 1), 0)).astype(F32)
    cos_ref[...], sin_ref[...] = _rope_tables(pos, invf_ref[...])


def _rope_table(t, inv_freq, rows=512):
    spec = pl.BlockSpec((rows, 128), lambda i: (i, 0))
    return pl.pallas_call(
        functools.partial(_rope_table_kernel, rows=rows),
        grid=(t // rows,),
        in_specs=[pl.BlockSpec((1, 128), lambda i: (0, 0))],
        out_specs=[spec, spec],
        out_shape=[jax.ShapeDtypeStruct((t, 128), F32)] * 2,
        compiler_params=_params(("parallel",), 32),
        name="rope_table",
    )(inv_freq)


def _swa_prompt_kernel(p_ref, cos_ref, sin_ref, sink_ref, y_ref, ko_ref, vo_ref, kprev_ref, vprev_ref, *, nb,
                       group):
    n = pl.program_id(1)
    blk = WINDOW

    @pl.when(n == 0)
    def _init():
        kprev_ref[...] = jnp.zeros_like(kprev_ref)
        vprev_ref[...] = jnp.zeros_like(vprev_ref)

    p = p_ref[0]
    cos, sin_signed = cos_ref[...], sin_ref[...]
    q = _rope(p[:, :RWKV_WIDTH], cos, sin_signed, on_mxu=True) * (HEAD ** -0.5)
    k = _rope(p[:, RWKV_WIDTH:RWKV_WIDTH + KV_WIDTH], cos, sin_signed)
    v = p[:, RWKV_WIDTH + KV_WIDTH:]
    kcat = jnp.concatenate([kprev_ref[...], k], axis=0)
    vcat = jnp.concatenate([vprev_ref[...], v], axis=0)
    own = _iota((4 * blk, blk), 1) <= (_iota((4 * blk, blk), 0) & (blk - 1))
    no_prev = jnp.where(n == 0, NEG, 0.0)
    ones = jnp.ones((blk, blk), BF16)
    zero = jnp.zeros((), BF16)
    rows = lambda j, m: slice(j * blk, (j + m) * blk)
    heads = lambda g: range(4 * g, 4 * g + 4)
    cols = lambda g: slice(g * HEAD, (g + 1) * HEAD)
    sink_col = [jnp.concatenate([jnp.broadcast_to(sink_ref[0:1, h:h + 1], (blk, 1)) for h in heads(g)], axis=0)
                for g in range(2)]
    for j0 in range(0, nb, group):
        units = [(j, g) for j in range(j0, j0 + group) for g in range(2)]
        qg = {(j, g): _bf(jnp.concatenate([q[rows(j, 1), h * HEAD:(h + 1) * HEAD] for h in heads(g)], axis=0))
              for j, g in units}
        s2 = {(j, g): _dot_nt(qg[(j, g)], _bf(kcat[rows(j, 2), cols(g)])) for j, g in units}
        s = {(j, g): jnp.where(own, s2[(j, g)][:, blk:],
                               s2[(j, g)][:, :blk] + no_prev if j == 0 else s2[(j, g)][:, :blk]) for j, g in units}
        m = {u: jnp.maximum(jnp.max(s[u], axis=-1, keepdims=True), sink_col[u[1]]) for u in units}
        e = {u: _bf(jnp.exp(s[u] - m[u])) for u in units}
        inv = {u: 1.0 / (_dot(e[u], ones)[:, :HEAD] + jnp.exp(sink_col[u[1]] - m[u])) for u in units}
        e2 = {u: jnp.concatenate([jnp.where(own, zero, e[u]), jnp.where(own, e[u], zero)], axis=1) for u in units}
        o = {(j, g): _dot(e2[(j, g)], _bf(vcat[rows(j, 2), cols(g)])) * inv[(j, g)] for j, g in units}
        for j in range(j0, j0 + group):
            y_ref[0, rows(j, 1), :] = _bf(jnp.concatenate(
                [o[(j, g)][i * blk:(i + 1) * blk] for g in range(2) for i in range(4)], axis=1))
    k_last, v_last = k[(nb - 1) * blk:], v[(nb - 1) * blk:]
    ko_ref[0] = k_last
    vo_ref[0] = v_last
    kprev_ref[...] = k_last
    vprev_ref[...] = v_last


def _swa_prompt(p, inv_freq, sink, nb=8, group=2):
    bsz, t, _ = p.shape
    blk = WINDOW
    rows = nb * blk
    cos, sin_signed = _rope_table(t, inv_freq, rows)
    table = pl.BlockSpec((rows, 128), lambda b, n: (n, 0))
    return pl.pallas_call(
        functools.partial(_swa_prompt_kernel, nb=nb, group=group),
        grid=(bsz, t // rows),
        in_specs=[pl.BlockSpec((1, rows, SWA_COLS), lambda b, n: (b, n, 0)), table, table,
                  pl.BlockSpec((1, N_HEADS), lambda b, n: (0, 0))],
        out_specs=[pl.BlockSpec((1, rows, RWKV_WIDTH), lambda b, n: (b, n, 0)),
                   pl.BlockSpec((1, blk, KV_WIDTH), lambda b, n: (b, 0, 0)),
                   pl.BlockSpec((1, blk, KV_WIDTH), lambda b, n: (b, 0, 0))],
        out_shape=[jax.ShapeDtypeStruct((bsz, t, RWKV_WIDTH), BF16),
                   jax.ShapeDtypeStruct((bsz, blk, KV_WIDTH), F32),
                   jax.ShapeDtypeStruct((bsz, blk, KV_WIDTH), F32)],
        scratch_shapes=[pltpu.VMEM((blk, KV_WIDTH), F32), pltpu.VMEM((blk, KV_WIDTH), F32)],
        compiler_params=_params(("arbitrary", "arbitrary"), 48),
        name="swa_prompt",
    )(p, cos, sin_signed, sink)


def _swa_sample_kernel(p_ref, ck_ref, cv_ref, invf_ref, sinkc_ref, y_ref, nk_ref, nv_ref, *, bt):
    p = p_ref[...]
    cos, sin_signed = _rope_tables(jnp.full((1, 1), float(PAST_LEN), F32), invf_ref[...])
    q = _rope(p[:, :RWKV_WIDTH], cos, sin_signed) * (HEAD ** -0.5)
    k = _rope(p[:, RWKV_WIDTH:RWKV_WIDTH + KV_WIDTH], cos, sin_signed)
    v = p[:, RWKV_WIDTH + KV_WIDTH:]
    k_t, v_t = k.T, v.T
    last = _iota((KV_WIDTH, WINDOW), 1) == WINDOW - 1
    head_mask = _iota((N_HEADS, RWKV_WIDTH), 0) == (_iota((N_HEADS, RWKV_WIDTH), 1) >> 6)
    tile_lanes = lambda x: jnp.concatenate([x[:, :HEAD]] * 4 + [x[:, HEAD:]] * 4, axis=1)
    tile_rows = lambda x: jnp.concatenate([x[:HEAD]] * 4 + [x[HEAD:]] * 4, axis=0)
    seqs = range(bt)
    k_new = [jnp.where(last, k_t[:, bi:bi + 1], pltpu.roll(ck_ref[bi], WINDOW - 1, axis=1)) for bi in seqs]
    v_new = [jnp.where(last, v_t[:, bi:bi + 1], pltpu.roll(cv_ref[bi], WINDOW - 1, axis=1)) for bi in seqs]
    for bi in seqs:
        nk_ref[bi] = k_new[bi]
        nv_ref[bi] = v_new[bi]
    q_bd = [jnp.where(head_mask, q[bi:bi + 1, :], 0.0) for bi in seqs]
    s = [_dot3(q_bd[bi], tile_rows(k_new[bi])) for bi in seqs]
    pr = [_softmax_sink(s[bi], sinkc_ref[...]) for bi in seqs]
    o = [_dot3_nt(pr[bi], v_new[bi]) for bi in seqs]
    y_ref[...] = jnp.concatenate(
        [jnp.sum(jnp.where(head_mask, tile_lanes(o[bi]), 0.0), axis=0, keepdims=True) for bi in seqs], axis=0)


def _swa_sample(p, ck, cv, inv_freq, sink_col, bt=8):
    n = p.shape[0]
    cache_spec = pl.BlockSpec((bt, KV_WIDTH, WINDOW), lambda i: (i, 0, 0))
    return pl.pallas_call(
        functools.partial(_swa_sample_kernel, bt=bt),
        grid=(n // bt,),
        in_specs=[pl.BlockSpec((bt, SWA_COLS), lambda i: (i, 0)), cache_spec, cache_spec,
                  pl.BlockSpec((1, 128), lambda i: (0, 0)), pl.BlockSpec((N_HEADS, 1), lambda i: (0, 0))],
        out_specs=[pl.BlockSpec((bt, RWKV_WIDTH), lambda i: (i, 0)), cache_spec, cache_spec],
        out_shape=[jax.ShapeDtypeStruct((n, RWKV_WIDTH), F32), jax.ShapeDtypeStruct(ck.shape, F32),
                   jax.ShapeDtypeStruct(cv.shape, F32)],
        compiler_params=_params(("parallel",), 32),
        name="swa_sample",
    )(p, ck, cv, inv_freq, sink_col)


def _mix_out(yr, ys, x, wo1, wo2):
    return x + _dot(yr, wo1) + _dot(ys, wo2)


def _out_cross_kernel(yr_ref, ys_ref, x_ref, wo1_ref, wo2_ref, gc_ref, wq_ref, mk_ref, mv_ref, wco_ref,
                      qs_ref, mks_ref, mvs_ref, o_ref, os_ref, *, bt):
    _cross_sample(qs_ref, mks_ref, mvs_ref, os_ref, bt)
    x1 = _mix_out(yr_ref[0], ys_ref[0], x_ref[0], wo1_ref[...], wo2_ref[...])
    q = _bf(_dot(_bf(_rmsnorm(x1, gc_ref[...])), wq_ref[...]))
    mk, mv = mk_ref[0], mv_ref[0]
    heads = range(X_HEADS)
    sl = lambda h: slice(h * X_HEAD, (h + 1) * X_HEAD)
    s = [_dot_nt(q[:, sl(h)], mk[:, sl(h)]) * (X_HEAD ** -0.5) for h in heads]
    e = [jnp.exp(s[h] - jnp.max(s[h], axis=-1, keepdims=True)) for h in heads]
    outs = [_dot(_bf(e[h]), mv[:, sl(h)]) / jnp.sum(e[h], axis=-1, keepdims=True) for h in heads]
    o_ref[0] = x1 + _dot(_bf(jnp.concatenate(outs, axis=1)), wco_ref[...])


def _out_cross(yr, ys, x, wo1, wo2, gain, wq, mk, mv, wco, q_s, mk_s, mv_s, tm=1024):
    bsz, t, d = x.shape
    n = q_s.shape[0]
    nt = t // tm
    bt = n // (bsz * nt)
    assert bt * bsz * nt == n
    tile = lambda w: pl.BlockSpec((1, tm, w), lambda b, i: (b, i, 0))
    const = lambda a: pl.BlockSpec(a.shape, lambda b, i: (0,) * a.ndim)
    mem = pl.BlockSpec((1, MEM_TOKENS, X_WIDTH), lambda b, i: (b, 0, 0))
    mem_shape = (n, MEM_TOKENS // 2, 2 * X_HEADS, X_HEAD)
    mem_s = pl.BlockSpec((bt,) + mem_shape[1:], lambda b, i: (b * nt + i, 0, 0, 0))
    vec_s = pl.BlockSpec((bt, X_HEADS, X_HEAD), lambda b, i: (b * nt + i, 0, 0))
    x2, o_s = pl.pallas_call(
        functools.partial(_out_cross_kernel, bt=bt),
        grid=(bsz, nt),
        in_specs=[tile(RWKV_WIDTH), tile(RWKV_WIDTH), tile(d), const(wo1), const(wo2), const(gain), const(wq),
                  mem, mem, const(wco), vec_s, mem_s, mem_s],
        out_specs=[tile(d), vec_s],
        out_shape=[jax.ShapeDtypeStruct((bsz, t, d), F32), jax.ShapeDtypeStruct((n, X_HEADS, X_HEAD), F32)],
        compiler_params=_params(("parallel", "parallel"), 60),
        name="out_cross",
    )(yr, ys, x, wo1, wo2, gain, wq, mk, mv, wco, q_s.reshape(n, X_HEADS, X_HEAD), mk_s.reshape(mem_shape),
      mv_s.reshape(mem_shape))
    return x2, o_s.reshape(n, X_WIDTH)


def _out_q_kernel(yt_ref, r_ref, k_ref, v_ref, g_ref, rk_ref, lnw_ref, lnb_ref, ys_ref, x_ref, wo1_ref, wo2_ref,
                  gc_ref, wq_ref, x1_ref, q_ref):
    yr = _rwkv_post(yt_ref[...].T, r_ref[...], k_ref[...], v_ref[...], g_ref[...], rk_ref[...], lnw_ref[...],
                    lnb_ref[...], _block_ones(RWKV_WIDTH, 6), True)
    x1 = x_ref[...] + _dot3(yr, wo1_ref[...]) + _dot3(ys_ref[...], wo2_ref[...])
    x1_ref[...] = x1
    q_ref[...] = _dot3(_rmsnorm(x1, gc_ref[...]), wq_ref[...])


def _cross_sample(q_ref, mk_ref, mv_ref, o_ref, bt):
    seqs = range(bt)
    both = lambda x: x + pltpu.roll(x, X_HEADS, axis=0)
    q8 = [jnp.concatenate([q_ref[bi]] * 2, axis=0) for bi in seqs]
    s = [jnp.sum(mk_ref[bi] * q8[bi][None], axis=-1, keepdims=True) * (X_HEAD ** -0.5) for bi in seqs]
    mx = [jnp.max(s[bi], axis=0) for bi in seqs]
    mx = [jnp.maximum(m, pltpu.roll(m, X_HEADS, axis=0)) for m in mx]
    e = [jnp.exp(s[bi] - mx[bi][None]) for bi in seqs]
    den = [both(jnp.sum(e[bi], axis=0)) for bi in seqs]
    o8 = [jnp.sum(e[bi] * mv_ref[bi], axis=0) / den[bi] for bi in seqs]
    for bi in seqs:
        o_ref[bi] = both(o8[bi])[:X_HEADS]


def _lin_res_kernel(x_ref, a_ref, w_ref, o_ref):
    o_ref[...] = x_ref[...] + _dot3(a_ref[...], w_ref[...])


def _single_step(kernel_fn, out_shape, name, *args):
    full = lambda a: pl.BlockSpec(a.shape, lambda i: (0,) * len(a.shape))
    outs = out_shape if isinstance(out_shape, (list, tuple)) else [out_shape]
    out_specs = [full(o) for o in outs]
    return pl.pallas_call(
        kernel_fn, grid=(1,), in_specs=[full(a) for a in args],
        out_specs=out_specs if isinstance(out_shape, (list, tuple)) else out_specs[0],
        out_shape=out_shape, compiler_params=_params(("arbitrary",), 32), name=name,
    )(*args)


def _out_q_sample(rwkv_parts, ys, x, wo1, wo2, gain, wq):
    n, d = x.shape
    return _single_step(_out_q_kernel, [jax.ShapeDtypeStruct((n, d), F32), jax.ShapeDtypeStruct((n, X_WIDTH), F32)],
                        "out_q_sample", *rwkv_parts, ys, x, wo1, wo2, gain, wq)


def _cross_out_sample(x1, o, wco):
    return _single_step(_lin_res_kernel, jax.ShapeDtypeStruct(x1.shape, F32), "cross_out_sample", x1, o, wco)


def _route(logits):
    lane_i = _iota(logits.shape, 1)
    lane = lane_i.astype(F32)
    big = 1024.0
    n_exp = float(N_GROUPS * EXPERTS_PER_GROUP)
    lg = jnp.where((lane >= n_exp) & (lane < n_exp + N_GROUPS), logits, NEG)
    g_max = jnp.max(lg, axis=-1, keepdims=True)
    g_idx = jnp.min(jnp.where(lg == g_max, lane, big), axis=-1, keepdims=True) - n_exp
    p_sel = 1.0 / jnp.sum(jnp.exp(lg - g_max), axis=-1, keepdims=True)
    le = jnp.where((lane_i >> 3).astype(F32) == g_idx, logits, NEG)
    t1 = jnp.max(le, axis=-1, keepdims=True)
    i1 = jnp.min(jnp.where(le == t1, lane, big), axis=-1, keepdims=True)
    le2 = jnp.where(lane == i1, NEG, le)
    t2 = jnp.max(le2, axis=-1, keepdims=True)
    i2 = jnp.min(jnp.where(le2 == t2, lane, big), axis=-1, keepdims=True)
    e2 = jnp.exp(t2 - t1)
    w1 = 1.0 / (1.0 + e2)
    return p_sel * (jnp.where(lane == i1, w1, 0.0) + jnp.where(lane == i2, e2 * w1, 0.0)), g_idx


def _moe_kernel(x_ref, gf_ref, wr_ref, br_ref, wg_ref, wu_ref, wd_ref, gfin_ref, tri_ref, o_ref, h_ref, gate_ref,
                pos_ref, acc_ref, *, tm, blocks, precise):
    g = pl.program_id(1)

    @pl.when(g == 0)
    def _init():
        h = _rmsnorm(x_ref[...], gf_ref[...])
        h_ref[...] = _bf(h)
        gate, g_idx = _route((_dot3 if precise else _dot1)(h, wr_ref[...]) + br_ref[...])
        gate_ref[...] = gate
        lane = _iota((tm, 128), 1).astype(F32)
        own = lane == g_idx
        count = _dot(tri_ref[...], _mask_bf(own))
        pos_ref[...] = jnp.where(own, count, -1.0)
        acc_ref[...] = jnp.zeros_like(acc_ref)

    lane = _iota((tm, 128), 1)
    pos = jnp.sum(jnp.where(lane == g, pos_ref[...], 0.0), axis=-1, keepdims=True)
    n_g = jnp.sum(jnp.where(pos >= 0.0, 1.0, 0.0))
    width = EXPERTS_PER_GROUP * EXPERT_HIDDEN
    expand = _mask_bf(_iota((128, width), 0) == g * EXPERTS_PER_GROUP + (_iota((128, width), 1) >> 7))
    start = 0
    for bm in blocks:
        @pl.when(n_g > start)
        def _block(start=start, bm=bm):
            sel = _mask_bf(pos - float(start) == _iota((tm, bm), 1).astype(F32))
            hs = _bf(_dot_tn(sel, h_ref[...]))
            gh, gl = _split(gate_ref[...])
            gs = _dot_tn(sel, gh) + _dot_tn(sel, gl) if precise else _dot_tn(sel, gh)
            hg = _dot(hs, wg_ref[0])
            hu = _dot(hs, wu_ref[0])
            act = hg * _sigmoid(hg) * hu * (_dot_sel_rhs if precise else _dot1)(gs, expand)
            acc_ref[...] += _dot(sel, _bf(_dot(_bf(act), wd_ref[0])))

        start += bm

    @pl.when(g == N_GROUPS - 1)
    def _fin():
        o_ref[...] = _rmsnorm(x_ref[...] + acc_ref[...], gfin_ref[...])


def _moe(x, gain, w_router, b_router, wg, wu, wd, gain_final, tm, blocks, precise):
    n, d = x.shape
    assert sum(blocks) == tm
    width = EXPERTS_PER_GROUP * EXPERT_HIDDEN
    const = lambda a: pl.BlockSpec(a.shape, lambda i, g: (0,) * a.ndim)
    earlier = jnp.tril(jnp.ones((tm, tm), BF16), -1)
    return pl.pallas_call(
        functools.partial(_moe_kernel, tm=tm, blocks=blocks, precise=precise),
        grid=(n // tm, N_GROUPS),
        in_specs=[pl.BlockSpec((tm, d), lambda i, g: (i, 0)), const(gain), const(w_router), const(b_router),
                  pl.BlockSpec((1, d, width), lambda i, g: (g, 0, 0)),
                  pl.BlockSpec((1, d, width), lambda i, g: (g, 0, 0)),
                  pl.BlockSpec((1, width, d), lambda i, g: (g, 0, 0)), const(gain_final), const(earlier)],
        out_specs=pl.BlockSpec((tm, d), lambda i, g: (i, 0)),
        out_shape=jax.ShapeDtypeStruct((n, d), F32),
        scratch_shapes=[pltpu.VMEM((tm, d), BF16), pltpu.VMEM((tm, 128), F32), pltpu.VMEM((tm, 128), F32),
                        pltpu.VMEM((tm, d), F32)],
        compiler_params=_params(("parallel", "arbitrary"), 60),
        name="moe",
    )(x, gain, w_router, b_router, wg, wu, wd, gain_final, earlier)


def kernel(x_prompt, x_sample, mem_prompt, state_rwkv_shift, state_rwkv_wkv, cache_swa_k, cache_swa_v, cache_mem_k, cache_mem_v, norm_mix, w_in, mu_shift, w_decay0, w_decay_up, a0, w_a_up, w_g_up, k_k, k_a, r_k, ln_x_w, ln_x_b, attn_sink, w_out, norm_cross, norm_mem, w_cq, w_ck, w_cv, w_co, norm_ffn, w_group_router, b_group_router, w_expert_router, b_expert_router, w_exp_gate, w_exp_up, w_exp_down, norm_final):
    bsz, t, d = x_prompt.shape
    n_s = x_sample.shape[0]
    wb = cache_swa_k.shape[2]
    width = EXPERTS_PER_GROUP * EXPERT_HIDDEN

    w_in_bf = w_in[0].astype(BF16)
    wo1, wo2 = w_out[0, :RWKV_WIDTH].astype(BF16), w_out[0, RWKV_WIDTH:].astype(BF16)
    wq, wco = w_cq[0].astype(BF16), w_co[0].astype(BF16)
    w_ckv = jnp.concatenate([w_ck[0], w_cv[0]], axis=1).astype(BF16)
    w_router = jnp.pad(jnp.concatenate([w_expert_router[0], w_group_router[0]], axis=1), ((0, 0), (0, 128 - 36)))
    b_router = jnp.pad(jnp.concatenate([b_expert_router[0], b_group_router[0]]), (0, 128 - 36)).reshape(1, 128)
    wg = jnp.transpose(w_exp_gate[0], (0, 2, 1, 3)).reshape(N_GROUPS, d, width).astype(BF16)
    wu = jnp.transpose(w_exp_up[0], (0, 2, 1, 3)).reshape(N_GROUPS, d, width).astype(BF16)
    wd = w_exp_down[0].reshape(N_GROUPS, width, d).astype(BF16)
    row = lambda a: a.reshape(1, -1)
    lp = dict(mu_shift=mu_shift, w_decay0=w_decay0, w_decay_up=w_decay_up[0], a0=a0, w_a_up=w_a_up[0],
              w_g_up=w_g_up[0], k_k=k_k, k_a=k_a, r_k=row(r_k), ln_x_w=ln_x_w, ln_x_b=ln_x_b)
    half = HEAD // 2
    inv_freq = ROPE_THETA ** (-jnp.arange(half, dtype=F32) * 2.0 / HEAD)
    inv_freq = jnp.tile(inv_freq, 128 // half).reshape(1, 128)

    xs = x_sample.reshape(n_s, d)
    ps_rwkv, ps_swa = _norm_matmul(xs, norm_mix, w_in[0], RWKV_COLS, n_s)
    (r_s, k2_s, v_s_tok, g_s), yt_s, wkv_s = _rwkv_sample(ps_rwkv, state_rwkv_shift[0],
                                                           jnp.transpose(state_rwkv_wkv[0], (1, 2, 3, 0)), lp)
    cache_t = lambda c: jnp.transpose(c[0].reshape(n_s, wb, KV_WIDTH), (0, 2, 1))
    ys_swa, k_s, v_s = _swa_sample(ps_swa, cache_t(cache_swa_k), cache_t(cache_swa_v), inv_freq,
                                   attn_sink.reshape(N_HEADS, 1))
    x1s, q_s = _out_q_sample((yt_s, r_s, k2_s, v_s_tok, g_s, lp["r_k"], lp["ln_x_w"], lp["ln_x_b"]), ys_swa, xs,
                             w_out[0, :RWKV_WIDTH], w_out[0, RWKV_WIDTH:], norm_cross, w_cq[0])

    p_rwkv, p_swa = _norm_matmul(x_prompt.reshape(bsz * t, d), norm_mix, w_in_bf, RWKV_COLS, 1024)
    p_rwkv = p_rwkv.reshape(bsz, t, RWKV_COLS)
    y_rwkv, wkv_p = _rwkv_prompt(p_rwkv, lp)
    y_swa, k_p, v_p = _swa_prompt(p_swa.reshape(bsz, t, SWA_COLS), inv_freq, attn_sink)
    mk_p, mv_p = _norm_matmul(mem_prompt.reshape(bsz * MEM_TOKENS, d), norm_mem, w_ckv, X_WIDTH, 256)
    mk_b = mk_p.astype(BF16).reshape(bsz, MEM_TOKENS, X_WIDTH)
    mv_b = mv_p.astype(BF16).reshape(bsz, MEM_TOKENS, X_WIDTH)
    x2, o_s = _out_cross(y_rwkv, y_swa, x_prompt, wo1, wo2, norm_cross, wq, mk_b, mv_b, wco, q_s, cache_mem_k[0],
                         cache_mem_v[0])
    y_prompt = _moe(x2.reshape(bsz * t, d), norm_ffn, w_router, b_router, wg, wu, wd, row(norm_final), 1024,
                    (256, 128, 128, 256, 256), False)
    y_sample = _moe(_cross_out_sample(x1s, o_s, w_co[0]), norm_ffn, w_router, b_router, wg, wu, wd,
                    row(norm_final), n_s, (n_s,), True)

    kv_shape = (1, -1, wb, KV_WIDTH // HEAD, HEAD)
    mem_shape = (1, bsz, MEM_TOKENS, X_HEADS, X_HEAD)
    cache_back = lambda c: jnp.transpose(c, (0, 2, 1)).reshape(kv_shape)
    return (y_prompt.reshape(bsz, t, d), y_sample.reshape(n_s, 1, d),
            p_rwkv[:, -1][None], wkv_p[None], k_p.reshape(kv_shape), v_p.reshape(kv_shape),
            mk_p.reshape(mem_shape), mv_p.reshape(mem_shape),
            ps_rwkv[None], jnp.transpose(wkv_s, (3, 0, 1, 2))[None], cache_back(k_s), cache_back(v_s))
```

```python
import functools
import math

import jax
import jax.numpy as jnp
from jax import lax
from jax.experimental import pallas as pl
from jax.experimental.pallas import tpu as pltpu

F32 = jnp.float32
BF16 = jnp.bfloat16

D_MODEL = 1024
RWKV_WIDTH = 512
HEAD = 64
N_HEADS = 8
RWKV_COLS = 1792
SWA_COLS = 768
KV_WIDTH = 128
WINDOW = 128
PAST_LEN = 8192
ROPE_THETA = 10000.0
MEM_TOKENS = 256
X_HEADS = 4
X_HEAD = 128
X_WIDTH = 512
N_GROUPS = 4
EXPERTS_PER_GROUP = 8
EXPERT_HIDDEN = 128
NORM_EPS = 1e-5
GN_EPS = 64e-5
L2_EPS = 1e-12
CHUNK = 64
NEG = -1e30

LANES = 128
MIB = 1024 * 1024
VMEM_SMALL, VMEM_MID, VMEM_LARGE = 32, 48, 60

PROJ_ROWS = 1024
MEMKV_ROWS = 256
RWKV_ROWS = 8 * CHUNK
SWA_BLOCKS = 8
MOE_ROWS = 1024
MOE_BLOCKS = (256, 128, 128, 256, 256)


def _log2(n):
    assert n & (n - 1) == 0
    return n.bit_length() - 1


def _params(semantics, vmem_mib):
    return pltpu.CompilerParams(dimension_semantics=semantics, vmem_limit_bytes=vmem_mib * MIB)


def _bf(x):
    return x.astype(BF16)


def _dot(a, b):
    return jnp.dot(a, b, preferred_element_type=F32)


def _dot_nt(a, b):
    return lax.dot_general(a, b, (((1,), (1,)), ((), ())), preferred_element_type=F32)


def _dot_tn(a, b):
    return lax.dot_general(a, b, (((0,), (0,)), ((), ())), preferred_element_type=F32)


def _split(x):
    hi = x.astype(BF16)
    lo = (x - hi.astype(F32)).astype(BF16)
    return hi, lo


def _dot3(a, b, dot=_dot):
    ah, al = _split(a)
    bh, bl = _split(b)
    return dot(ah, bh) + (dot(ah, bl) + dot(al, bh))


def _dot3_nt(a, b):
    return _dot3(a, b, _dot_nt)


def _dot_sel_rhs(a, sel):
    ah, al = _split(a)
    return _dot(ah, sel) + _dot(al, sel)


def _dot_sel_lhs(sel, b):
    bh, bl = _split(b)
    return _dot(sel, bh) + _dot(sel, bl)


def _rmsnorm(x, gain):
    return x * lax.rsqrt(jnp.mean(x * x, axis=-1, keepdims=True) + NORM_EPS) * gain


def _sigmoid(x):
    return 1.0 / (1.0 + jnp.exp(-x))


def _iota(shape, dim):
    return lax.broadcasted_iota(jnp.int32, shape, dim)


def _mask_bf(cond):
    return jnp.where(cond, 1.0, 0.0).astype(BF16)


def _block_ones(n, blk):
    return _mask_bf((_iota((n, n), 0) >> _log2(blk)) == (_iota((n, n), 1) >> _log2(blk)))


def _softmax_sink(s, sink_col):
    m = jnp.maximum(jnp.max(s, axis=-1, keepdims=True), sink_col)
    e = jnp.exp(s - m)
    den = jnp.sum(e, axis=-1, keepdims=True) + jnp.exp(sink_col - m)
    return e / den


def _rope(x, cos, sin_signed, on_mxu=False):
    width = x.shape[1]
    reps = width // LANES
    if on_mxu:
        src, dst = _iota((LANES, LANES), 0), _iota((LANES, LANES), 1)
        partner = jnp.where((dst & (HEAD - 1)) < HEAD // 2, dst + HEAD // 2, dst - HEAD // 2)
        swap = _mask_bf(src == partner)
        rot = jnp.concatenate([_dot(_bf(x[:, c * LANES:(c + 1) * LANES]), swap) for c in range(reps)], axis=1)
    else:
        first_half = (_iota(x.shape, 1) & (HEAD - 1)) < HEAD // 2
        rot = jnp.where(first_half, pltpu.roll(x, width - HEAD // 2, axis=1), pltpu.roll(x, HEAD // 2, axis=1))
    if reps > 1:
        cos = jnp.concatenate([cos] * reps, axis=1)
        sin_signed = jnp.concatenate([sin_signed] * reps, axis=1)
    return x * cos + rot * sin_signed


def _rope_tables(pos, inv_freq):
    ang = pos * inv_freq
    cos, sin = jnp.cos(ang), jnp.sin(ang)
    first_half = (_iota(ang.shape, 1) & (HEAD - 1)) < HEAD // 2
    return cos, jnp.where(first_half, -sin, sin)


def _norm_matmul_kernel(x_ref, g_ref, w_ref, o1_ref, o2_ref, *, split):
    h = _rmsnorm(x_ref[...], g_ref[...])
    w = w_ref[...]
    p = _dot(_bf(h), w) if w.dtype == BF16 else _dot3(h, w)
    o1_ref[...] = p[:, :split]
    o2_ref[...] = p[:, split:]


def _norm_matmul(x, gain, w, split, tm):
    n, d = x.shape
    cols = w.shape[1]
    return pl.pallas_call(
        functools.partial(_norm_matmul_kernel, split=split),
        grid=(n // tm,),
        in_specs=[pl.BlockSpec((tm, d), lambda i: (i, 0)),
                  pl.BlockSpec((1, d), lambda i: (0, 0)),
                  pl.BlockSpec((d, cols), lambda i: (0, 0))],
        out_specs=[pl.BlockSpec((tm, split), lambda i: (i, 0)),
                   pl.BlockSpec((tm, cols - split), lambda i: (i, 0))],
        out_shape=[jax.ShapeDtypeStruct((n, split), F32), jax.ShapeDtypeStruct((n, cols - split), F32)],
        compiler_params=_params(("parallel",), VMEM_MID),
        name="norm_matmul",
    )(x, gain, w)


def _dot1(a, b):
    return _dot(_bf(a), _bf(b))


def _head_sums(precise):
    dot = _dot_sel_rhs if precise else _dot1
    ones_pair = _block_ones(LANES, HEAD)
    return lambda x: jnp.concatenate(
        [dot(x[:, c * LANES:(c + 1) * LANES], ones_pair) for c in range(x.shape[1] // LANES)], axis=1)


def _rwkv_prep(p, prev, mu, w0, wd, a0, wa, wg, k_k, k_a, precise):
    mm, head_sum = (_dot3 if precise else _dot1), _head_sums(precise)
    c = RWKV_WIDTH
    xm = p + (prev - p) * mu
    r, k, v = xm[:, :c], xm[:, c:2 * c], xm[:, 2 * c:3 * c]
    xw, xa, xg = xm[:, 3 * c:3 * c + 64], xm[:, 3 * c + 64:3 * c + 128], xm[:, 3 * c + 128:]
    logw = -math.exp(-0.5) * _sigmoid(w0 + _dot3(jnp.tanh(xw), wd))
    rate = _sigmoid(a0 + mm(xa, wa))
    gate = mm(_sigmoid(xg), wg)
    kk = k * k_k
    kk = kk * lax.rsqrt(jnp.maximum(head_sum(kk * kk), L2_EPS * L2_EPS))
    k2 = k * (1.0 + (rate - 1.0) * k_a)
    return r, logw, k2, v, -kk, kk * rate, gate


def _rwkv_post(y, r, k2, v, gate, r_k, ln_w, ln_b, precise):
    head_sum = _head_sums(precise)
    mean = head_sum(y) * (1.0 / HEAD)
    d = y - mean
    var = head_sum(d * d) * (1.0 / HEAD)
    yn = d * lax.rsqrt(var + GN_EPS) * ln_w + ln_b
    bonus = head_sum(r * k2 * r_k) * v
    return (yn + bonus) * gate


def _rwkv_prompt_kernel(p_ref, mu_ref, w0_ref, wd_ref, a0_ref, wa_ref, wg_ref, kk_ref, ka_ref, rk_ref, lnw_ref,
                        lnb_ref, y_ref, s_ref, carry_ref, st_ref, ar_ref, bk_ref, bp_ref, kp_ref, v_ref, yacc_ref,
                        *, tc):
    c = pl.program_id(1)

    @pl.when(c == 0)
    def _init():
        carry_ref[...] = jnp.zeros_like(carry_ref)
        st_ref[...] = jnp.zeros_like(st_ref)

    p = p_ref[0]
    first_row = _iota((tc, 1), 0) == 0
    prev = jnp.where(first_row, carry_ref[0:1, :], pltpu.roll(p, 1, axis=0))
    carry_ref[0:1, :] = p[tc - 1:tc, :]
    r, logw, k2, v, a, b, gate = _rwkv_prep(p, prev, mu_ref[...], w0_ref[...], wd_ref[...], a0_ref[...],
                                             wa_ref[...], wg_ref[...], kk_ref[...], ka_ref[...], False)
    ri, ci = _iota((LANES, LANES), 0), _iota((LANES, LANES), 1)
    tri = _mask_bf(((ri >> _log2(CHUNK)) == (ci >> _log2(CHUNK))) & (ci <= ri))
    cum = jnp.concatenate([_dot_sel_lhs(tri, logw[i0:i0 + LANES]) for i0 in range(0, tc, LANES)], axis=0)
    n_ch = tc // CHUNK
    p_end = [jnp.exp(cum[(ci_ + 1) * CHUNK - 1:(ci_ + 1) * CHUNK, :]) for ci_ in range(n_ch)]
    e_inv = jnp.exp(-cum)
    e_end = e_inv * jnp.concatenate([jnp.broadcast_to(pe, (CHUNK, RWKV_WIDTH)) for pe in p_end], axis=0)
    ar_ref[0], ar_ref[1] = _bf(a * jnp.exp(cum - logw)), _bf(r * jnp.exp(cum))
    bk_ref[0], bk_ref[1] = _bf(b * e_inv), _bf(k2 * e_inv)
    bp_ref[...] = _bf(b * e_end)
    kp_ref[...] = _bf(k2 * e_end)
    v_ref[...] = _bf(v)

    rr, cc = _iota((CHUNK, 2 * HEAD), 0), _iota((CHUNK, 2 * HEAD), 1) & (HEAD - 1)
    strict, incl, eye = cc < rr, cc <= rr, cc == rr
    lane = _iota((CHUNK, 2 * HEAD), 1)
    left, right = lane < HEAD, lane >= HEAD
    zero = jnp.zeros((), BF16)

    def stack(x):
        return jnp.concatenate([jnp.where(left, x, zero.astype(x.dtype)), jnp.where(right, x, zero.astype(x.dtype))],
                               axis=0)

    def diag_blocks(x):
        return jnp.where(left, x[:HEAD], x[HEAD:])

    n_ch, n_pairs = tc // CHUNK, N_HEADS // 2
    ids = [(ci_, pr) for ci_ in range(n_ch) for pr in range(n_pairs)]
    rows = lambda ci_: slice(ci_ * CHUNK, (ci_ + 1) * CHUNK)
    cols = lambda pr: slice(pr * 2 * HEAD, (pr + 1) * 2 * HEAD)
    at = {i: ar_ref[0, rows(i[0]), cols(i[1])] for i in ids}
    rt = {i: ar_ref[1, rows(i[0]), cols(i[1])] for i in ids}
    vv = {i: v_ref[rows(i[0]), cols(i[1])] for i in ids}
    bp = {i: bp_ref[rows(i[0]), cols(i[1])] for i in ids}
    g = {i: _dot_nt(jnp.concatenate([at[i], rt[i]], axis=0),
                    jnp.concatenate([stack(bk_ref[0, rows(i[0]), cols(i[1])]),
                                     stack(bk_ref[1, rows(i[0]), cols(i[1])])], axis=0)) for i in ids}
    vk = {i: diag_blocks(_dot_tn(vv[i], kp_ref[rows(i[0]), cols(i[1])])) for i in ids}
    a_ab = {i: jnp.where(strict, g[i][:CHUNK, :2 * HEAD], 0.0) for i in ids}
    a_rb = {i: _bf(jnp.where(incl, g[i][CHUNK:, :2 * HEAD], 0.0)) for i in ids}
    av = {i: _dot(_bf(jnp.concatenate([jnp.where(strict, g[i][:CHUNK, 2 * HEAD:], 0.0),
                                       jnp.where(incl, g[i][CHUNK:, 2 * HEAD:], 0.0)], axis=0)), stack(vv[i]))
          for i in ids}
    t = {i: jnp.where(eye, 1.0, a_ab[i]) for i in ids}
    apow = {i: _bf(a_ab[i]) for i in ids}
    apow = {i: _bf(_dot(apow[i], stack(apow[i]))) for i in ids}
    for step in range(4):
        both = {i: _dot(jnp.concatenate([_bf(t[i]), apow[i]], axis=0), stack(apow[i])) for i in ids}
        t = {i: t[i] + both[i][:CHUNK] for i in ids}
        apow = {i: _bf(both[i][CHUNK:]) for i in ids}
    t = {i: t[i] + _dot(_bf(t[i]), stack(apow[i])) for i in ids}
    wu = {i: _bf(_dot(_bf(t[i]), jnp.concatenate([stack(at[i]), stack(_bf(av[i][:CHUNK]))], axis=1)))
          for i in ids}
    wy_y0 = {i: jnp.concatenate([rt[i].astype(F32), av[i][CHUNK:]], axis=1)
             + _dot(a_rb[i], jnp.concatenate([stack(wu[i][:, :2 * HEAD]), stack(wu[i][:, 2 * HEAD:])], axis=1))
             for i in ids}
    mbn = {i: _dot_tn(wu[i], bp[i]) for i in ids}
    rb, cb = _iota((2 * HEAD, 2 * HEAD), 0) < HEAD, _iota((2 * HEAD, 2 * HEAD), 1) < HEAD
    mb = {i: _bf(jnp.where(rb == cb, mbn[i][:2 * HEAD], 0.0)) for i in ids}
    n0 = {i: vk[i] + diag_blocks(mbn[i][2 * HEAD:]) for i in ids}
    state = [st_ref[:, cols(pr)] for pr in range(n_pairs)]
    for ci_ in range(n_ch):
        sb = [_bf(s) for s in state]
        ys = [wy_y0[(ci_, pr)][:, 2 * HEAD:] + _dot_nt(_bf(wy_y0[(ci_, pr)][:, :2 * HEAD]), stack(sb[pr]))
              for pr in range(n_pairs)]
        state = [state[pr] * p_end[ci_][:, cols(pr)] + _dot(sb[pr], mb[(ci_, pr)])
                 + n0[(ci_, pr)] for pr in range(n_pairs)]
        for pr in range(n_pairs):
            yacc_ref[rows(ci_), cols(pr)] = ys[pr]
    for pr in range(n_pairs):
        st_ref[:, cols(pr)] = state[pr]
    y_ref[0] = _bf(_rwkv_post(yacc_ref[...], r, k2, v, gate, rk_ref[...], lnw_ref[...], lnb_ref[...], False))
    for h in range(N_HEADS):
        s_ref[0, h] = state[h // 2][:, (h % 2) * HEAD:(h % 2 + 1) * HEAD]


def _rwkv_prompt(p, lp, tc=RWKV_ROWS):
    bsz, t, _ = p.shape
    row = lambda n: pl.BlockSpec((1, n), lambda b, c: (0, 0))
    mat = lambda m, n: pl.BlockSpec((m, n), lambda b, c: (0, 0))
    tile = lambda dt: pltpu.VMEM((tc, RWKV_WIDTH), dt)
    two = pltpu.VMEM((2, tc, RWKV_WIDTH), BF16)
    return pl.pallas_call(
        functools.partial(_rwkv_prompt_kernel, tc=tc),
        grid=(bsz, t // tc),
        in_specs=[pl.BlockSpec((1, tc, RWKV_COLS), lambda b, c: (b, c, 0)),
                  row(RWKV_COLS), row(512), mat(64, 512), row(512), mat(64, 512), mat(128, 512),
                  row(512), row(512), row(512), row(512), row(512)],
        out_specs=[pl.BlockSpec((1, tc, RWKV_WIDTH), lambda b, c: (b, c, 0)),
                   pl.BlockSpec((1, N_HEADS, HEAD, HEAD), lambda b, c: (b, 0, 0, 0))],
        out_shape=[jax.ShapeDtypeStruct((bsz, t, RWKV_WIDTH), BF16),
                   jax.ShapeDtypeStruct((bsz, N_HEADS, HEAD, HEAD), F32)],
        scratch_shapes=[pltpu.VMEM((8, RWKV_COLS), F32), pltpu.VMEM((HEAD, RWKV_WIDTH), F32), two, two,
                        tile(BF16), tile(BF16), tile(BF16), tile(F32)],
        compiler_params=_params(("arbitrary", "arbitrary"), VMEM_MID),
        name="rwkv_prompt",
    )(p, lp["mu_shift"], lp["w_decay0"], lp["w_decay_up"], lp["a0"], lp["w_a_up"], lp["w_g_up"], lp["k_k"],
      lp["k_a"], lp["r_k"], lp["ln_x_w"], lp["ln_x_b"])


def _rwkv_sample_prep_kernel(p_ref, prev_ref, mu_ref, w0_ref, wd_ref, a0_ref, wa_ref, wg_ref, kk_ref, ka_ref,
                             r_ref, k_ref, v_ref, g_ref, *t_refs):
    r, logw, k2, v, a, b, gate = _rwkv_prep(p_ref[...], prev_ref[...], mu_ref[...], w0_ref[...], wd_ref[...],
                                             a0_ref[...], wa_ref[...], wg_ref[...], kk_ref[...], ka_ref[...], True)
    r_ref[...] = r
    k_ref[...] = k2
    v_ref[...] = v
    g_ref[...] = gate
    for ref, x in zip(t_refs, (r, jnp.exp(logw), k2, v, a, b)):
        ref[...] = x.T


def _rwkv_sample_step_kernel(s_ref, r_ref, w_ref, k_ref, v_ref, a_ref, b_ref, so_ref, y_ref, *, unroll):
    a, w, b, k, r = a_ref[...], w_ref[...], b_ref[...], k_ref[...], r_ref[...]

    def rows(i0, carry):
        base = pl.multiple_of(i0 * unroll, unroll)
        v8 = v_ref[pl.ds(base, unroll), :]
        s = [s_ref[0, base + u] for u in range(unroll)]
        sa = [jnp.sum(s[u] * a, axis=0, keepdims=True) for u in range(unroll)]
        s = [s[u] * w + sa[u] * b + v8[u:u + 1, :] * k for u in range(unroll)]
        for u in range(unroll):
            so_ref[0, base + u] = s[u]
        y_ref[pl.ds(base, unroll), :] = jnp.concatenate(
            [jnp.sum(s[u] * r, axis=0, keepdims=True) for u in range(unroll)], axis=0)
        return carry

    lax.fori_loop(0, HEAD // unroll, rows, 0)


def _rwkv_sample(p, prev, wkv_t, lp, unroll=8):
    n = p.shape[0]
    full = lambda a: pl.BlockSpec(a.shape, lambda i: (0,) * a.ndim)
    names = ("mu_shift", "w_decay0", "w_decay_up", "a0", "w_a_up", "w_g_up", "k_k", "k_a")
    prep_in = [p, prev] + [lp[k] for k in names]
    vecs = pl.pallas_call(
        _rwkv_sample_prep_kernel,
        grid=(1,),
        in_specs=[full(a) for a in prep_in],
        out_specs=[pl.BlockSpec((n, RWKV_WIDTH), lambda i: (0, 0))] * 4
        + [pl.BlockSpec((RWKV_WIDTH, n), lambda i: (0, 0))] * 6,
        out_shape=[jax.ShapeDtypeStruct((n, RWKV_WIDTH), F32)] * 4
        + [jax.ShapeDtypeStruct((RWKV_WIDTH, n), F32)] * 6,
        compiler_params=_params(("arbitrary",), VMEM_SMALL),
        name="rwkv_sample_prep",
    )(*prep_in)
    vec_t = pl.BlockSpec((HEAD, n), lambda h: (h, 0))
    st_spec = pl.BlockSpec((1, HEAD, HEAD, n), lambda h: (h, 0, 0, 0))
    wkv_new, y_t = pl.pallas_call(
        functools.partial(_rwkv_sample_step_kernel, unroll=unroll),
        grid=(N_HEADS,),
        in_specs=[st_spec] + [vec_t] * 6,
        out_specs=[st_spec, vec_t],
        out_shape=[jax.ShapeDtypeStruct(wkv_t.shape, F32), jax.ShapeDtypeStruct((RWKV_WIDTH, n), F32)],
        compiler_params=_params(("parallel",), VMEM_SMALL),
        name="rwkv_sample_step",
    )(wkv_t, *vecs[4:])
    return vecs[:4], y_t, wkv_new


def _rope_table_kernel(invf_ref, cos_ref, sin_ref, *, rows):
    pos = (pl.program_id(0) * rows + _iota((rows, 1), 0)).astype(F32)
    cos_ref[...], sin_ref[...] = _rope_tables(pos, invf_ref[...])


def _rope_table(t, inv_freq, rows):
    spec = pl.BlockSpec((rows, LANES), lambda i: (i, 0))
    return pl.pallas_call(
        functools.partial(_rope_table_kernel, rows=rows),
        grid=(t // rows,),
        in_specs=[pl.BlockSpec((1, LANES), lambda i: (0, 0))],
        out_specs=[spec, spec],
        out_shape=[jax.ShapeDtypeStruct((t, LANES), F32)] * 2,
        compiler_params=_params(("parallel",), VMEM_SMALL),
        name="rope_table",
    )(inv_freq)


def _swa_prompt_kernel(p_ref, cos_ref, sin_ref, sink_ref, y_ref, ko_ref, vo_ref, kprev_ref, vprev_ref, *, nb):
    n = pl.program_id(1)
    blk = WINDOW

    @pl.when(n == 0)
    def _init():
        kprev_ref[...] = jnp.zeros_like(kprev_ref)
        vprev_ref[...] = jnp.zeros_like(vprev_ref)

    p = p_ref[0]
    cos, sin_signed = cos_ref[...], sin_ref[...]
    q = _rope(p[:, :RWKV_WIDTH], cos, sin_signed, on_mxu=True) * (HEAD ** -0.5)
    k = _rope(p[:, RWKV_WIDTH:RWKV_WIDTH + KV_WIDTH], cos, sin_signed)
    v = p[:, RWKV_WIDTH + KV_WIDTH:]
    kcat = jnp.concatenate([kprev_ref[...], k], axis=0)
    vcat = jnp.concatenate([vprev_ref[...], v], axis=0)
    own = _iota((4 * blk, blk), 1) <= (_iota((4 * blk, blk), 0) & (blk - 1))
    no_prev = jnp.where(n == 0, NEG, 0.0)
    ones = jnp.ones((blk, blk), BF16)
    zero = jnp.zeros((), BF16)
    rows = lambda j, m: slice(j * blk, (j + m) * blk)
    heads = lambda g: range(4 * g, 4 * g + 4)
    cols = lambda g: slice(g * HEAD, (g + 1) * HEAD)
    sink_col = [jnp.concatenate([jnp.broadcast_to(sink_ref[0:1, h:h + 1], (blk, 1)) for h in heads(g)], axis=0)
                for g in range(2)]
    units = [(j, g) for j in range(nb) for g in range(2)]
    qg = {(j, g): _bf(jnp.concatenate([q[rows(j, 1), h * HEAD:(h + 1) * HEAD] for h in heads(g)], axis=0))
          for j, g in units}
    s2 = {(j, g): _dot_nt(qg[(j, g)], _bf(kcat[rows(j, 2), cols(g)])) for j, g in units}
    s = {(j, g): jnp.where(own, s2[(j, g)][:, blk:],
                           s2[(j, g)][:, :blk] + no_prev if j == 0 else s2[(j, g)][:, :blk]) for j, g in units}
    m = {u: jnp.maximum(jnp.max(s[u], axis=-1, keepdims=True), sink_col[u[1]]) for u in units}
    e = {u: _bf(jnp.exp(s[u] - m[u])) for u in units}
    inv = {u: 1.0 / (_dot(e[u], ones)[:, :HEAD] + jnp.exp(sink_col[u[1]] - m[u])) for u in units}
    e2 = {u: jnp.concatenate([jnp.where(own, zero, e[u]), jnp.where(own, e[u], zero)], axis=1) for u in units}
    o = {(j, g): _dot(e2[(j, g)], _bf(vcat[rows(j, 2), cols(g)])) * inv[(j, g)] for j, g in units}
    y_ref[0] = _bf(jnp.concatenate(
        [jnp.concatenate([o[(j, g)][i * blk:(i + 1) * blk] for g in range(2) for i in range(4)], axis=1)
         for j in range(nb)], axis=0))
    k_last, v_last = k[(nb - 1) * blk:], v[(nb - 1) * blk:]
    ko_ref[0] = k_last
    vo_ref[0] = v_last
    kprev_ref[...] = k_last
    vprev_ref[...] = v_last


def _swa_prompt(p, inv_freq, sink, nb=SWA_BLOCKS):
    bsz, t, _ = p.shape
    blk = WINDOW
    rows = nb * blk
    cos, sin_signed = _rope_table(t, inv_freq, rows)
    table = pl.BlockSpec((rows, LANES), lambda b, n: (n, 0))
    return pl.pallas_call(
        functools.partial(_swa_prompt_kernel, nb=nb),
        grid=(bsz, t // rows),
        in_specs=[pl.BlockSpec((1, rows, SWA_COLS), lambda b, n: (b, n, 0)), table, table,
                  pl.BlockSpec((1, N_HEADS), lambda b, n: (0, 0))],
        out_specs=[pl.BlockSpec((1, rows, RWKV_WIDTH), lambda b, n: (b, n, 0)),
                   pl.BlockSpec((1, blk, KV_WIDTH), lambda b, n: (b, 0, 0)),
                   pl.BlockSpec((1, blk, KV_WIDTH), lambda b, n: (b, 0, 0))],
        out_shape=[jax.ShapeDtypeStruct((bsz, t, RWKV_WIDTH), BF16),
                   jax.ShapeDtypeStruct((bsz, blk, KV_WIDTH), F32),
                   jax.ShapeDtypeStruct((bsz, blk, KV_WIDTH), F32)],
        scratch_shapes=[pltpu.VMEM((blk, KV_WIDTH), F32), pltpu.VMEM((blk, KV_WIDTH), F32)],
        compiler_params=_params(("arbitrary", "arbitrary"), VMEM_MID),
        name="swa_prompt",
    )(p, cos, sin_signed, sink)


def _swa_sample_kernel(p_ref, ck_ref, cv_ref, invf_ref, sinkc_ref, y_ref, nk_ref, nv_ref, *, bt):
    p = p_ref[...]
    cos, sin_signed = _rope_tables(jnp.full((1, 1), float(PAST_LEN), F32), invf_ref[...])
    q = _rope(p[:, :RWKV_WIDTH], cos, sin_signed) * (HEAD ** -0.5)
    k = _rope(p[:, RWKV_WIDTH:RWKV_WIDTH + KV_WIDTH], cos, sin_signed)
    v = p[:, RWKV_WIDTH + KV_WIDTH:]
    k_t, v_t = k.T, v.T
    last = _iota((KV_WIDTH, WINDOW), 1) == WINDOW - 1
    head_mask = _iota((N_HEADS, RWKV_WIDTH), 0) == (_iota((N_HEADS, RWKV_WIDTH), 1) >> _log2(HEAD))
    tile_lanes = lambda x: jnp.concatenate([x[:, :HEAD]] * 4 + [x[:, HEAD:]] * 4, axis=1)
    tile_rows = lambda x: jnp.concatenate([x[:HEAD]] * 4 + [x[HEAD:]] * 4, axis=0)
    seqs = range(bt)
    k_new = [jnp.where(last, k_t[:, bi:bi + 1], pltpu.roll(ck_ref[bi], WINDOW - 1, axis=1)) for bi in seqs]
    v_new = [jnp.where(last, v_t[:, bi:bi + 1], pltpu.roll(cv_ref[bi], WINDOW - 1, axis=1)) for bi in seqs]
    for bi in seqs:
        nk_ref[bi] = k_new[bi]
        nv_ref[bi] = v_new[bi]
    q_bd = [jnp.where(head_mask, q[bi:bi + 1, :], 0.0) for bi in seqs]
    s = [_dot3(q_bd[bi], tile_rows(k_new[bi])) for bi in seqs]
    pr = [_softmax_sink(s[bi], sinkc_ref[...]) for bi in seqs]
    o = [_dot3_nt(pr[bi], v_new[bi]) for bi in seqs]
    y_ref[...] = jnp.concatenate(
        [jnp.sum(jnp.where(head_mask, tile_lanes(o[bi]), 0.0), axis=0, keepdims=True) for bi in seqs], axis=0)


def _swa_sample(p, ck, cv, inv_freq, sink_col, bt=8):
    n = p.shape[0]
    cache_spec = pl.BlockSpec((bt, KV_WIDTH, WINDOW), lambda i: (i, 0, 0))
    return pl.pallas_call(
        functools.partial(_swa_sample_kernel, bt=bt),
        grid=(n // bt,),
        in_specs=[pl.BlockSpec((bt, SWA_COLS), lambda i: (i, 0)), cache_spec, cache_spec,
                  pl.BlockSpec((1, LANES), lambda i: (0, 0)), pl.BlockSpec((N_HEADS, 1), lambda i: (0, 0))],
        out_specs=[pl.BlockSpec((bt, RWKV_WIDTH), lambda i: (i, 0)), cache_spec, cache_spec],
        out_shape=[jax.ShapeDtypeStruct((n, RWKV_WIDTH), F32), jax.ShapeDtypeStruct(ck.shape, F32),
                   jax.ShapeDtypeStruct(cv.shape, F32)],
        compiler_params=_params(("parallel",), VMEM_SMALL),
        name="swa_sample",
    )(p, ck, cv, inv_freq, sink_col)


def _mix_out(yr, ys, x, wo1, wo2):
    return x + _dot(yr, wo1) + _dot(ys, wo2)


def _out_cross_kernel(yr_ref, ys_ref, x_ref, wo1_ref, wo2_ref, gc_ref, wq_ref, mk_ref, mv_ref, wco_ref,
                      qs_ref, mks_ref, mvs_ref, o_ref, os_ref, *, bt):
    _cross_sample(qs_ref, mks_ref, mvs_ref, os_ref, bt)
    x1 = _mix_out(yr_ref[0], ys_ref[0], x_ref[0], wo1_ref[...], wo2_ref[...])
    q = _bf(_dot(_bf(_rmsnorm(x1, gc_ref[...])), wq_ref[...]))
    mk, mv = mk_ref[0], mv_ref[0]
    heads = range(X_HEADS)
    sl = lambda h: slice(h * X_HEAD, (h + 1) * X_HEAD)
    s = [_dot_nt(q[:, sl(h)], mk[:, sl(h)]) * (X_HEAD ** -0.5) for h in heads]
    e = [jnp.exp(s[h] - jnp.max(s[h], axis=-1, keepdims=True)) for h in heads]
    outs = [_dot(_bf(e[h]), mv[:, sl(h)]) / jnp.sum(e[h], axis=-1, keepdims=True) for h in heads]
    o_ref[0] = x1 + _dot(_bf(jnp.concatenate(outs, axis=1)), wco_ref[...])


def _out_cross(yr, ys, x, wo1, wo2, gain, wq, mk, mv, wco, q_s, mk_s, mv_s, tm=PROJ_ROWS):
    bsz, t, d = x.shape
    n = q_s.shape[0]
    nt = t // tm
    bt = n // (bsz * nt)
    assert bt * bsz * nt == n
    tile = lambda w: pl.BlockSpec((1, tm, w), lambda b, i: (b, i, 0))
    const = lambda a: pl.BlockSpec(a.shape, lambda b, i: (0,) * a.ndim)
    mem = pl.BlockSpec((1, MEM_TOKENS, X_WIDTH), lambda b, i: (b, 0, 0))
    mem_shape = (n, MEM_TOKENS // 2, 2 * X_HEADS, X_HEAD)
    mem_s = pl.BlockSpec((bt,) + mem_shape[1:], lambda b, i: (b * nt + i, 0, 0, 0))
    vec_s = pl.BlockSpec((bt, X_HEADS, X_HEAD), lambda b, i: (b * nt + i, 0, 0))
    x2, o_s = pl.pallas_call(
        functools.partial(_out_cross_kernel, bt=bt),
        grid=(bsz, nt),
        in_specs=[tile(RWKV_WIDTH), tile(RWKV_WIDTH), tile(d), const(wo1), const(wo2), const(gain), const(wq),
                  mem, mem, const(wco), vec_s, mem_s, mem_s],
        out_specs=[tile(d), vec_s],
        out_shape=[jax.ShapeDtypeStruct((bsz, t, d), F32), jax.ShapeDtypeStruct((n, X_HEADS, X_HEAD), F32)],
        compiler_params=_params(("parallel", "parallel"), VMEM_LARGE),
        name="out_cross",
    )(yr, ys, x, wo1, wo2, gain, wq, mk, mv, wco, q_s.reshape(n, X_HEADS, X_HEAD), mk_s.reshape(mem_shape),
      mv_s.reshape(mem_shape))
    return x2, o_s.reshape(n, X_WIDTH)


def _out_q_kernel(yt_ref, r_ref, k_ref, v_ref, g_ref, rk_ref, lnw_ref, lnb_ref, ys_ref, x_ref, wo1_ref, wo2_ref,
                  gc_ref, wq_ref, x1_ref, q_ref):
    yr = _rwkv_post(yt_ref[...].T, r_ref[...], k_ref[...], v_ref[...], g_ref[...], rk_ref[...], lnw_ref[...],
                    lnb_ref[...], True)
    x1 = x_ref[...] + _dot3(yr, wo1_ref[...]) + _dot3(ys_ref[...], wo2_ref[...])
    x1_ref[...] = x1
    q_ref[...] = _dot3(_rmsnorm(x1, gc_ref[...]), wq_ref[...])


def _cross_sample(q_ref, mk_ref, mv_ref, o_ref, bt):
    seqs = range(bt)
    both = lambda x: x + pltpu.roll(x, X_HEADS, axis=0)
    q8 = [jnp.concatenate([q_ref[bi]] * 2, axis=0) for bi in seqs]
    s = [jnp.sum(mk_ref[bi] * q8[bi][None], axis=-1, keepdims=True) * (X_HEAD ** -0.5) for bi in seqs]
    mx = [jnp.max(s[bi], axis=0) for bi in seqs]
    mx = [jnp.maximum(m, pltpu.roll(m, X_HEADS, axis=0)) for m in mx]
    e = [jnp.exp(s[bi] - mx[bi][None]) for bi in seqs]
    den = [both(jnp.sum(e[bi], axis=0)) for bi in seqs]
    o8 = [jnp.sum(e[bi] * mv_ref[bi], axis=0) / den[bi] for bi in seqs]
    for bi in seqs:
        o_ref[bi] = both(o8[bi])[:X_HEADS]


def _lin_res_kernel(x_ref, a_ref, w_ref, o_ref):
    o_ref[...] = x_ref[...] + _dot3(a_ref[...], w_ref[...])


def _single_step(kernel_fn, out_shape, name, *args):
    full = lambda a: pl.BlockSpec(a.shape, lambda i: (0,) * len(a.shape))
    outs = out_shape if isinstance(out_shape, (list, tuple)) else [out_shape]
    out_specs = [full(o) for o in outs]
    return pl.pallas_call(
        kernel_fn, grid=(1,), in_specs=[full(a) for a in args],
        out_specs=out_specs if isinstance(out_shape, (list, tuple)) else out_specs[0],
        out_shape=out_shape, compiler_params=_params(("arbitrary",), VMEM_SMALL), name=name,
    )(*args)


def _out_q_sample(rwkv_parts, ys, x, wo1, wo2, gain, wq):
    n, d = x.shape
    return _single_step(_out_q_kernel, [jax.ShapeDtypeStruct((n, d), F32), jax.ShapeDtypeStruct((n, X_WIDTH), F32)],
                        "out_q_sample", *rwkv_parts, ys, x, wo1, wo2, gain, wq)


def _cross_out_sample(x1, o, wco):
    return _single_step(_lin_res_kernel, jax.ShapeDtypeStruct(x1.shape, F32), "cross_out_sample", x1, o, wco)


def _route(logits):
    lane_i = _iota(logits.shape, 1)
    lane = lane_i.astype(F32)
    big = 1024.0
    n_exp = float(N_GROUPS * EXPERTS_PER_GROUP)
    lg = jnp.where((lane >= n_exp) & (lane < n_exp + N_GROUPS), logits, NEG)
    g_max = jnp.max(lg, axis=-1, keepdims=True)
    g_idx = jnp.min(jnp.where(lg == g_max, lane, big), axis=-1, keepdims=True) - n_exp
    p_sel = 1.0 / jnp.sum(jnp.exp(lg - g_max), axis=-1, keepdims=True)
    le = jnp.where((lane_i >> _log2(EXPERTS_PER_GROUP)).astype(F32) == g_idx, logits, NEG)
    t1 = jnp.max(le, axis=-1, keepdims=True)
    i1 = jnp.min(jnp.where(le == t1, lane, big), axis=-1, keepdims=True)
    le2 = jnp.where(lane == i1, NEG, le)
    t2 = jnp.max(le2, axis=-1, keepdims=True)
    i2 = jnp.min(jnp.where(le2 == t2, lane, big), axis=-1, keepdims=True)
    e2 = jnp.exp(t2 - t1)
    w1 = 1.0 / (1.0 + e2)
    return p_sel * (jnp.where(lane == i1, w1, 0.0) + jnp.where(lane == i2, e2 * w1, 0.0)), g_idx


def _moe_kernel(x_ref, gf_ref, wr_ref, br_ref, wg_ref, wu_ref, wd_ref, gfin_ref, tri_ref, o_ref, h_ref, gate_ref,
                pos_ref, acc_ref, *, tm, blocks, precise):
    g = pl.program_id(1)

    @pl.when(g == 0)
    def _init():
        h = _rmsnorm(x_ref[...], gf_ref[...])
        h_ref[...] = _bf(h)
        gate, g_idx = _route((_dot3 if precise else _dot1)(h, wr_ref[...]) + br_ref[...])
        gate_ref[...] = gate
        lane = _iota((tm, LANES), 1).astype(F32)
        own = lane == g_idx
        count = _dot(tri_ref[...], _mask_bf(own))
        pos_ref[...] = jnp.where(own, count, -1.0)
        acc_ref[...] = jnp.zeros_like(acc_ref)

    lane = _iota((tm, LANES), 1)
    pos = jnp.sum(jnp.where(lane == g, pos_ref[...], 0.0), axis=-1, keepdims=True)
    n_g = jnp.sum(jnp.where(pos >= 0.0, 1.0, 0.0))
    width = EXPERTS_PER_GROUP * EXPERT_HIDDEN
    expand = _mask_bf(_iota((LANES, width), 0)
                      == g * EXPERTS_PER_GROUP + (_iota((LANES, width), 1) >> _log2(EXPERT_HIDDEN)))
    start = 0
    for bm in blocks:
        @pl.when(n_g > start)
        def _block(start=start, bm=bm):
            sel = _mask_bf(pos - float(start) == _iota((tm, bm), 1).astype(F32))
            hs = _bf(_dot_tn(sel, h_ref[...]))
            if precise:
                gh, gl = _split(gate_ref[...])
                gs = _dot_tn(sel, gh) + _dot_tn(sel, gl)
            else:
                gs = _dot_tn(sel, _bf(gate_ref[...]))
            hg = _dot(hs, wg_ref[0])
            hu = _dot(hs, wu_ref[0])
            act = hg * _sigmoid(hg) * hu * (_dot_sel_rhs if precise else _dot1)(gs, expand)
            acc_ref[...] += _dot(sel, _bf(_dot(_bf(act), wd_ref[0])))

        start += bm

    @pl.when(g == N_GROUPS - 1)
    def _fin():
        o_ref[...] = _rmsnorm(x_ref[...] + acc_ref[...], gfin_ref[...])


def _moe(x, gain, w_router, b_router, wg, wu, wd, gain_final, tm, blocks, precise):
    n, d = x.shape
    assert sum(blocks) == tm
    width = EXPERTS_PER_GROUP * EXPERT_HIDDEN
    const = lambda a: pl.BlockSpec(a.shape, lambda i, g: (0,) * a.ndim)
    earlier = jnp.tril(jnp.ones((tm, tm), BF16), -1)
    return pl.pallas_call(
        functools.partial(_moe_kernel, tm=tm, blocks=blocks, precise=precise),
        grid=(n // tm, N_GROUPS),
        in_specs=[pl.BlockSpec((tm, d), lambda i, g: (i, 0)), const(gain), const(w_router), const(b_router),
                  pl.BlockSpec((1, d, width), lambda i, g: (g, 0, 0)),
                  pl.BlockSpec((1, d, width), lambda i, g: (g, 0, 0)),
                  pl.BlockSpec((1, width, d), lambda i, g: (g, 0, 0)), const(gain_final), const(earlier)],
        out_specs=pl.BlockSpec((tm, d), lambda i, g: (i, 0)),
        out_shape=jax.ShapeDtypeStruct((n, d), F32),
        scratch_shapes=[pltpu.VMEM((tm, d), BF16), pltpu.VMEM((tm, LANES), F32), pltpu.VMEM((tm, LANES), F32),
                        pltpu.VMEM((tm, d), F32)],
        compiler_params=_params(("parallel", "arbitrary"), VMEM_LARGE),
        name="moe",
    )(x, gain, w_router, b_router, wg, wu, wd, gain_final, earlier)


def kernel(x_prompt, x_sample, mem_prompt, state_rwkv_shift, state_rwkv_wkv, cache_swa_k, cache_swa_v, cache_mem_k, cache_mem_v, norm_mix, w_in, mu_shift, w_decay0, w_decay_up, a0, w_a_up, w_g_up, k_k, k_a, r_k, ln_x_w, ln_x_b, attn_sink, w_out, norm_cross, norm_mem, w_cq, w_ck, w_cv, w_co, norm_ffn, w_group_router, b_group_router, w_expert_router, b_expert_router, w_exp_gate, w_exp_up, w_exp_down, norm_final):
    bsz, t, d = x_prompt.shape
    n_s = x_sample.shape[0]
    wb = cache_swa_k.shape[2]
    width = EXPERTS_PER_GROUP * EXPERT_HIDDEN

    w_in_bf = w_in[0].astype(BF16)
    wo1, wo2 = w_out[0, :RWKV_WIDTH].astype(BF16), w_out[0, RWKV_WIDTH:].astype(BF16)
    wq, wco = w_cq[0].astype(BF16), w_co[0].astype(BF16)
    w_ckv = jnp.concatenate([w_ck[0], w_cv[0]], axis=1).astype(BF16)
    pad = LANES - N_GROUPS * (EXPERTS_PER_GROUP + 1)
    w_router = jnp.pad(jnp.concatenate([w_expert_router[0], w_group_router[0]], axis=1), ((0, 0), (0, pad)))
    b_router = jnp.pad(jnp.concatenate([b_expert_router[0], b_group_router[0]]), (0, pad)).reshape(1, LANES)
    wg = jnp.transpose(w_exp_gate[0], (0, 2, 1, 3)).reshape(N_GROUPS, d, width).astype(BF16)
    wu = jnp.transpose(w_exp_up[0], (0, 2, 1, 3)).reshape(N_GROUPS, d, width).astype(BF16)
    wd = w_exp_down[0].reshape(N_GROUPS, width, d).astype(BF16)
    row = lambda a: a.reshape(1, -1)
    lp = dict(mu_shift=mu_shift, w_decay0=w_decay0, w_decay_up=w_decay_up[0], a0=a0, w_a_up=w_a_up[0],
              w_g_up=w_g_up[0], k_k=k_k, k_a=k_a, r_k=row(r_k), ln_x_w=ln_x_w, ln_x_b=ln_x_b)
    half = HEAD // 2
    inv_freq = ROPE_THETA ** (-jnp.arange(half, dtype=F32) * 2.0 / HEAD)
    inv_freq = jnp.tile(inv_freq, LANES // half).reshape(1, LANES)

    xs = x_sample.reshape(n_s, d)
    ps_rwkv, ps_swa = _norm_matmul(xs, norm_mix, w_in[0], RWKV_COLS, n_s)
    (r_s, k2_s, v_s_tok, g_s), yt_s, wkv_s = _rwkv_sample(ps_rwkv, state_rwkv_shift[0],
                                                           jnp.transpose(state_rwkv_wkv[0], (1, 2, 3, 0)), lp)
    cache_t = lambda c: jnp.transpose(c[0].reshape(n_s, wb, KV_WIDTH), (0, 2, 1))
    ys_swa, k_s, v_s = _swa_sample(ps_swa, cache_t(cache_swa_k), cache_t(cache_swa_v), inv_freq,
                                   attn_sink.reshape(N_HEADS, 1))
    x1s, q_s = _out_q_sample((yt_s, r_s, k2_s, v_s_tok, g_s, lp["r_k"], lp["ln_x_w"], lp["ln_x_b"]), ys_swa, xs,
                             w_out[0, :RWKV_WIDTH], w_out[0, RWKV_WIDTH:], norm_cross, w_cq[0])

    p_rwkv, p_swa = _norm_matmul(x_prompt.reshape(bsz * t, d), norm_mix, w_in_bf, RWKV_COLS, PROJ_ROWS)
    p_rwkv = p_rwkv.reshape(bsz, t, RWKV_COLS)
    y_rwkv, wkv_p = _rwkv_prompt(p_rwkv, lp)
    y_swa, k_p, v_p = _swa_prompt(p_swa.reshape(bsz, t, SWA_COLS), inv_freq, attn_sink)
    mk_p, mv_p = _norm_matmul(mem_prompt.reshape(bsz * MEM_TOKENS, d), norm_mem, w_ckv, X_WIDTH, MEMKV_ROWS)
    mk_b = mk_p.astype(BF16).reshape(bsz, MEM_TOKENS, X_WIDTH)
    mv_b = mv_p.astype(BF16).reshape(bsz, MEM_TOKENS, X_WIDTH)
    x2, o_s = _out_cross(y_rwkv, y_swa, x_prompt, wo1, wo2, norm_cross, wq, mk_b, mv_b, wco, q_s, cache_mem_k[0],
                         cache_mem_v[0])
    y_prompt = _moe(x2.reshape(bsz * t, d), norm_ffn, w_router, b_router, wg, wu, wd, row(norm_final), MOE_ROWS,
                    MOE_BLOCKS, False)
    y_sample = _moe(_cross_out_sample(x1s, o_s, w_co[0]), norm_ffn, w_router, b_router, wg, wu, wd,
                    row(norm_final), n_s, (n_s,), True)

    kv_shape = (1, -1, wb, KV_WIDTH // HEAD, HEAD)
    mem_shape = (1, bsz, MEM_TOKENS, X_HEADS, X_HEAD)
    cache_back = lambda c: jnp.transpose(c, (0, 2, 1)).reshape(kv_shape)
    return (y_prompt.reshape(bsz, t, d), y_sample.reshape(n_s, 1, d),
            p_rwkv[:, -1][None], wkv_p[None], k_p.reshape(kv_shape), v_p.reshape(kv_shape),
            mk_p.reshape(mem_shape), mv_p.reshape(mem_shape),
            ps_rwkv[None], jnp.transpose(wkv_s, (3, 0, 1, 2))[None], cache_back(k_s), cache_back(v_s))
```

```python
import functools
import math

import jax
import jax.numpy as jnp
from jax import lax
from jax.experimental import pallas as pl
from jax.experimental.pallas import tpu as pltpu

F32 = jnp.float32
BF16 = jnp.bfloat16

D_MODEL = 1024
RWKV_WIDTH = 512
HEAD = 64
N_HEADS = 8
RWKV_COLS = 1792
SWA_COLS = 768
KV_WIDTH = 128
WINDOW = 128
PAST_LEN = 8192
ROPE_THETA = 10000.0
MEM_TOKENS = 256
X_HEADS = 4
X_HEAD = 128
X_WIDTH = 512
N_GROUPS = 4
EXPERTS_PER_GROUP = 8
EXPERT_HIDDEN = 128
NORM_EPS = 1e-5
GN_EPS = 64e-5
L2_EPS = 1e-12
CHUNK = 64
NEG = -1e30

LANES = 128
MIB = 1024 * 1024
VMEM_SMALL, VMEM_MID, VMEM_LARGE = 32, 48, 60

PROJ_ROWS = 1024
MEMKV_ROWS = 256
RWKV_ROWS = 8 * CHUNK
SWA_BLOCKS = 8
MOE_ROWS = 1024
MOE_BLOCKS = (256, 128, 128, 256, 256)


def _log2(n):
    assert n & (n - 1) == 0
    return n.bit_length() - 1


def _params(semantics, vmem_mib):
    return pltpu.CompilerParams(dimension_semantics=semantics, vmem_limit_bytes=vmem_mib * MIB)


def _bf(x):
    return x.astype(BF16)


def _dot(a, b):
    return jnp.dot(a, b, preferred_element_type=F32)


def _dot_nt(a, b):
    return lax.dot_general(a, b, (((1,), (1,)), ((), ())), preferred_element_type=F32)


def _dot_tn(a, b):
    return lax.dot_general(a, b, (((0,), (0,)), ((), ())), preferred_element_type=F32)


def _split(x):
    hi = x.astype(BF16)
    lo = (x - hi.astype(F32)).astype(BF16)
    return hi, lo


def _dot3(a, b, dot=_dot):
    ah, al = _split(a)
    bh, bl = _split(b)
    return dot(ah, bh) + (dot(ah, bl) + dot(al, bh))


def _dot3_nt(a, b):
    return _dot3(a, b, _dot_nt)


def _dot_sel_rhs(a, sel):
    ah, al = _split(a)
    return _dot(ah, sel) + _dot(al, sel)


def _dot_sel_lhs(sel, b):
    bh, bl = _split(b)
    return _dot(sel, bh) + _dot(sel, bl)


def _rmsnorm(x, gain):
    return x * lax.rsqrt(jnp.mean(x * x, axis=-1, keepdims=True) + NORM_EPS) * gain


def _sigmoid(x):
    return 1.0 / (1.0 + jnp.exp(-x))


def _iota(shape, dim):
    return lax.broadcasted_iota(jnp.int32, shape, dim)


def _mask_bf(cond):
    return jnp.where(cond, 1.0, 0.0).astype(BF16)


def _block_ones(n, blk):
    return _mask_bf((_iota((n, n), 0) >> _log2(blk)) == (_iota((n, n), 1) >> _log2(blk)))


def _softmax_sink(s, sink_col):
    m = jnp.maximum(jnp.max(s, axis=-1, keepdims=True), sink_col)
    e = jnp.exp(s - m)
    den = jnp.sum(e, axis=-1, keepdims=True) + jnp.exp(sink_col - m)
    return e / den


def _rope(x, cos, sin_signed, on_mxu=False):
    width = x.shape[1]
    reps = width // LANES
    if on_mxu:
        src, dst = _iota((LANES, LANES), 0), _iota((LANES, LANES), 1)
        partner = jnp.where((dst & (HEAD - 1)) < HEAD // 2, dst + HEAD // 2, dst - HEAD // 2)
        swap = _mask_bf(src == partner)
        rot = jnp.concatenate([_dot(_bf(x[:, c * LANES:(c + 1) * LANES]), swap) for c in range(reps)], axis=1)
    else:
        first_half = (_iota(x.shape, 1) & (HEAD - 1)) < HEAD // 2
        rot = jnp.where(first_half, pltpu.roll(x, width - HEAD // 2, axis=1), pltpu.roll(x, HEAD // 2, axis=1))
    if reps > 1:
        cos = jnp.concatenate([cos] * reps, axis=1)
        sin_signed = jnp.concatenate([sin_signed] * reps, axis=1)
    return x * cos + rot * sin_signed


def _rope_tables(pos, inv_freq):
    ang = pos * inv_freq
    cos, sin = jnp.cos(ang), jnp.sin(ang)
    first_half = (_iota(ang.shape, 1) & (HEAD - 1)) < HEAD // 2
    return cos, jnp.where(first_half, -sin, sin)


def _norm_matmul_kernel(x_ref, g_ref, w_ref, o1_ref, o2_ref, *, split):
    h = _rmsnorm(x_ref[...], g_ref[...])
    w = w_ref[...]
    p = _dot(_bf(h), w) if w.dtype == BF16 else _dot3(h, w)
    o1_ref[...] = p[:, :split]
    o2_ref[...] = p[:, split:]


def _in_proj_shift_kernel(x_ref, g_ref, w_ref, mu_ref, xm_ref, last_ref, o2_ref, carry_ref, *, split, steps_per_seq):
    i = pl.program_id(0)

    @pl.when(i % steps_per_seq == 0)
    def _start():
        carry_ref[...] = jnp.zeros_like(carry_ref)

    p = _dot(_bf(_rmsnorm(x_ref[...], g_ref[...])), w_ref[...])
    p1 = p[:, :split]
    rows = p1.shape[0]
    prev = jnp.where(_iota((rows, 1), 0) == 0, carry_ref[0:1, :], pltpu.roll(p1, 1, axis=0))
    xm_ref[...] = p1 + (prev - p1) * mu_ref[...]
    carry_ref[0:1, :] = p1[rows - 1:rows, :]
    last_ref[0] = p1[rows - 1:rows, :]
    o2_ref[...] = p[:, split:]


def _in_proj_shift(x, gain, w, mu, split, tm, seq_len):
    n, d = x.shape
    cols = w.shape[1]
    steps_per_seq = seq_len // tm
    return pl.pallas_call(
        functools.partial(_in_proj_shift_kernel, split=split, steps_per_seq=steps_per_seq),
        grid=(n // tm,),
        in_specs=[pl.BlockSpec((tm, d), lambda i: (i, 0)),
                  pl.BlockSpec((1, d), lambda i: (0, 0)),
                  pl.BlockSpec((d, cols), lambda i: (0, 0)),
                  pl.BlockSpec((1, split), lambda i: (0, 0))],
        out_specs=[pl.BlockSpec((tm, split), lambda i: (i, 0)),
                   pl.BlockSpec((1, 1, split), lambda i: (i // steps_per_seq, 0, 0)),
                   pl.BlockSpec((tm, cols - split), lambda i: (i, 0))],
        out_shape=[jax.ShapeDtypeStruct((n, split), F32), jax.ShapeDtypeStruct((n // seq_len, 1, split), F32),
                   jax.ShapeDtypeStruct((n, cols - split), F32)],
        scratch_shapes=[pltpu.VMEM((8, split), F32)],
        compiler_params=_params(("arbitrary",), VMEM_LARGE),
        name="in_proj_shift",
    )(x, gain, w, mu)


def _norm_matmul(x, gain, w, split, tm):
    n, d = x.shape
    cols = w.shape[1]
    return pl.pallas_call(
        functools.partial(_norm_matmul_kernel, split=split),
        grid=(n // tm,),
        in_specs=[pl.BlockSpec((tm, d), lambda i: (i, 0)),
                  pl.BlockSpec((1, d), lambda i: (0, 0)),
                  pl.BlockSpec((d, cols), lambda i: (0, 0))],
        out_specs=[pl.BlockSpec((tm, split), lambda i: (i, 0)),
                   pl.BlockSpec((tm, cols - split), lambda i: (i, 0))],
        out_shape=[jax.ShapeDtypeStruct((n, split), F32), jax.ShapeDtypeStruct((n, cols - split), F32)],
        compiler_params=_params(("parallel",), VMEM_MID),
        name="norm_matmul",
    )(x, gain, w)


def _dot1(a, b):
    return _dot(_bf(a), _bf(b))


def _head_sums(precise):
    dot = _dot_sel_rhs if precise else _dot1
    ones_pair = _block_ones(LANES, HEAD)
    return lambda x: jnp.concatenate(
        [dot(x[:, c * LANES:(c + 1) * LANES], ones_pair) for c in range(x.shape[1] // LANES)], axis=1)


def _rwkv_prep(xm, w0, wd, a0, wa, wg, k_k, k_a, precise):
    mm, head_sum = (_dot3 if precise else _dot1), _head_sums(precise)
    c = RWKV_WIDTH
    r, k, v = xm[:, :c], xm[:, c:2 * c], xm[:, 2 * c:3 * c]
    xw, xa, xg = xm[:, 3 * c:3 * c + 64], xm[:, 3 * c + 64:3 * c + 128], xm[:, 3 * c + 128:]
    logw = -math.exp(-0.5) * _sigmoid(w0 + _dot3(jnp.tanh(xw), wd))
    rate = _sigmoid(a0 + mm(xa, wa))
    gate = mm(_sigmoid(xg), wg)
    kk = k * k_k
    kk = kk * lax.rsqrt(jnp.maximum(head_sum(kk * kk), L2_EPS * L2_EPS))
    k2 = k * (1.0 + (rate - 1.0) * k_a)
    return r, logw, k2, v, -kk, kk * rate, gate


def _rwkv_post(y, r, k2, v, gate, r_k, ln_w, ln_b, precise):
    head_sum = _head_sums(precise)
    mean = head_sum(y) * (1.0 / HEAD)
    d = y - mean
    var = head_sum(d * d) * (1.0 / HEAD)
    yn = d * lax.rsqrt(var + GN_EPS) * ln_w + ln_b
    bonus = head_sum(r * k2 * r_k) * v
    return (yn + bonus) * gate


def _rwkv_prompt_kernel(xm_ref, w0_ref, wd_ref, a0_ref, wa_ref, wg_ref, kk_ref, ka_ref, rk_ref, lnw_ref,
                        lnb_ref, y_ref, s_ref, st_ref, ar_ref, bk_ref, bp_ref, kp_ref, v_ref, yacc_ref, *, tc):
    c = pl.program_id(1)

    @pl.when(c == 0)
    def _init():
        st_ref[...] = jnp.zeros_like(st_ref)

    r, logw, k2, v, a, b, gate = _rwkv_prep(xm_ref[0], w0_ref[...], wd_ref[...], a0_ref[...], wa_ref[...],
                                             wg_ref[...], kk_ref[...], ka_ref[...], False)
    ri, ci = _iota((LANES, LANES), 0), _iota((LANES, LANES), 1)
    tri = _mask_bf(((ri >> _log2(CHUNK)) == (ci >> _log2(CHUNK))) & (ci <= ri))
    cum = jnp.concatenate([_dot_sel_lhs(tri, logw[i0:i0 + LANES]) for i0 in range(0, tc, LANES)], axis=0)
    n_ch = tc // CHUNK
    p_end = [jnp.exp(cum[(ci_ + 1) * CHUNK - 1:(ci_ + 1) * CHUNK, :]) for ci_ in range(n_ch)]
    e_inv = jnp.exp(-cum)
    e_end = e_inv * jnp.concatenate([jnp.broadcast_to(pe, (CHUNK, RWKV_WIDTH)) for pe in p_end], axis=0)
    ar_ref[0], ar_ref[1] = _bf(a * jnp.exp(cum - logw)), _bf(r * jnp.exp(cum))
    bk_ref[0], bk_ref[1] = _bf(b * e_inv), _bf(k2 * e_inv)
    bp_ref[...] = _bf(b * e_end)
    kp_ref[...] = _bf(k2 * e_end)
    v_ref[...] = _bf(v)

    rr, cc = _iota((CHUNK, 2 * HEAD), 0), _iota((CHUNK, 2 * HEAD), 1) & (HEAD - 1)
    strict, incl, eye = cc < rr, cc <= rr, cc == rr
    lane = _iota((CHUNK, 2 * HEAD), 1)
    left, right = lane < HEAD, lane >= HEAD
    zero = jnp.zeros((), BF16)

    def stack(x):
        return jnp.concatenate([jnp.where(left, x, zero.astype(x.dtype)), jnp.where(right, x, zero.astype(x.dtype))],
                               axis=0)

    def diag_blocks(x):
        return jnp.where(left, x[:HEAD], x[HEAD:])

    n_ch, n_pairs = tc // CHUNK, N_HEADS // 2
    ids = [(ci_, pr) for ci_ in range(n_ch) for pr in range(n_pairs)]
    rows = lambda ci_: slice(ci_ * CHUNK, (ci_ + 1) * CHUNK)
    cols = lambda pr: slice(pr * 2 * HEAD, (pr + 1) * 2 * HEAD)
    at = {i: ar_ref[0, rows(i[0]), cols(i[1])] for i in ids}
    rt = {i: ar_ref[1, rows(i[0]), cols(i[1])] for i in ids}
    vv = {i: v_ref[rows(i[0]), cols(i[1])] for i in ids}
    bp = {i: bp_ref[rows(i[0]), cols(i[1])] for i in ids}
    g = {i: _dot_nt(jnp.concatenate([at[i], rt[i]], axis=0),
                    jnp.concatenate([stack(bk_ref[0, rows(i[0]), cols(i[1])]),
                                     stack(bk_ref[1, rows(i[0]), cols(i[1])])], axis=0)) for i in ids}
    vk = {i: diag_blocks(_dot_tn(vv[i], kp_ref[rows(i[0]), cols(i[1])])) for i in ids}
    a_ab = {i: jnp.where(strict, g[i][:CHUNK, :2 * HEAD], 0.0) for i in ids}
    a_rb = {i: _bf(jnp.where(incl, g[i][CHUNK:, :2 * HEAD], 0.0)) for i in ids}
    av = {i: _dot(_bf(jnp.concatenate([jnp.where(strict, g[i][:CHUNK, 2 * HEAD:], 0.0),
                                       jnp.where(incl, g[i][CHUNK:, 2 * HEAD:], 0.0)], axis=0)), stack(vv[i]))
          for i in ids}
    t = {i: jnp.where(eye, 1.0, a_ab[i]) for i in ids}
    apow = {i: _bf(a_ab[i]) for i in ids}
    apow = {i: _bf(_dot(apow[i], stack(apow[i]))) for i in ids}
    for step in range(4):
        both = {i: _dot(jnp.concatenate([_bf(t[i]), apow[i]], axis=0), stack(apow[i])) for i in ids}
        t = {i: t[i] + both[i][:CHUNK] for i in ids}
        apow = {i: _bf(both[i][CHUNK:]) for i in ids}
    t = {i: t[i] + _dot(_bf(t[i]), stack(apow[i])) for i in ids}
    wu = {i: _bf(_dot(_bf(t[i]), jnp.concatenate([stack(at[i]), stack(_bf(av[i][:CHUNK]))], axis=1)))
          for i in ids}
    wy_y0 = {i: jnp.concatenate([rt[i].astype(F32), av[i][CHUNK:]], axis=1)
             + _dot(a_rb[i], jnp.concatenate([stack(wu[i][:, :2 * HEAD]), stack(wu[i][:, 2 * HEAD:])], axis=1))
             for i in ids}
    mbn = {i: _dot_tn(wu[i], bp[i]) for i in ids}
    rb, cb = _iota((2 * HEAD, 2 * HEAD), 0) < HEAD, _iota((2 * HEAD, 2 * HEAD), 1) < HEAD
    mb = {i: _bf(jnp.where(rb == cb, mbn[i][:2 * HEAD], 0.0)) for i in ids}
    n0 = {i: vk[i] + diag_blocks(mbn[i][2 * HEAD:]) for i in ids}
    state = [st_ref[:, cols(pr)] for pr in range(n_pairs)]
    for ci_ in range(n_ch):
        sb = [_bf(s) for s in state]
        ys = [wy_y0[(ci_, pr)][:, 2 * HEAD:] + _dot_nt(_bf(wy_y0[(ci_, pr)][:, :2 * HEAD]), stack(sb[pr]))
              for pr in range(n_pairs)]
        state = [state[pr] * p_end[ci_][:, cols(pr)] + _dot(sb[pr], mb[(ci_, pr)])
                 + n0[(ci_, pr)] for pr in range(n_pairs)]
        for pr in range(n_pairs):
            yacc_ref[rows(ci_), cols(pr)] = ys[pr]
    for pr in range(n_pairs):
        st_ref[:, cols(pr)] = state[pr]
    y_ref[0] = _bf(_rwkv_post(yacc_ref[...], r, k2, v, gate, rk_ref[...], lnw_ref[...], lnb_ref[...], False))
    for h in range(N_HEADS):
        s_ref[0, h] = state[h // 2][:, (h % 2) * HEAD:(h % 2 + 1) * HEAD]


def _rwkv_prompt(xm, lp, tc=RWKV_ROWS):
    bsz, t, _ = xm.shape
    row = lambda n: pl.BlockSpec((1, n), lambda b, c: (0, 0))
    mat = lambda m, n: pl.BlockSpec((m, n), lambda b, c: (0, 0))
    tile = lambda dt: pltpu.VMEM((tc, RWKV_WIDTH), dt)
    two = pltpu.VMEM((2, tc, RWKV_WIDTH), BF16)
    return pl.pallas_call(
        functools.partial(_rwkv_prompt_kernel, tc=tc),
        grid=(bsz, t // tc),
        in_specs=[pl.BlockSpec((1, tc, RWKV_COLS), lambda b, c: (b, c, 0)),
                  row(512), mat(64, 512), row(512), mat(64, 512), mat(128, 512),
                  row(512), row(512), row(512), row(512), row(512)],
        out_specs=[pl.BlockSpec((1, tc, RWKV_WIDTH), lambda b, c: (b, c, 0)),
                   pl.BlockSpec((1, N_HEADS, HEAD, HEAD), lambda b, c: (b, 0, 0, 0))],
        out_shape=[jax.ShapeDtypeStruct((bsz, t, RWKV_WIDTH), BF16),
                   jax.ShapeDtypeStruct((bsz, N_HEADS, HEAD, HEAD), F32)],
        scratch_shapes=[pltpu.VMEM((HEAD, RWKV_WIDTH), F32), two, two, tile(BF16), tile(BF16), tile(BF16),
                        tile(F32)],
        compiler_params=_params(("arbitrary", "arbitrary"), VMEM_MID),
        name="rwkv_prompt",
    )(xm, lp["w_decay0"], lp["w_decay_up"], lp["a0"], lp["w_a_up"], lp["w_g_up"], lp["k_k"],
      lp["k_a"], lp["r_k"], lp["ln_x_w"], lp["ln_x_b"])


def _rwkv_sample_prep_kernel(p_ref, prev_ref, mu_ref, w0_ref, wd_ref, a0_ref, wa_ref, wg_ref, kk_ref, ka_ref,
                             r_ref, k_ref, v_ref, g_ref, *t_refs):
    p = p_ref[...]
    xm = p + (prev_ref[...] - p) * mu_ref[...]
    r, logw, k2, v, a, b, gate = _rwkv_prep(xm, w0_ref[...], wd_ref[...], a0_ref[...], wa_ref[...], wg_ref[...],
                                             kk_ref[...], ka_ref[...], True)
    r_ref[...] = r
    k_ref[...] = k2
    v_ref[...] = v
    g_ref[...] = gate
    for ref, x in zip(t_refs, (r, jnp.exp(logw), k2, v, a, b)):
        ref[...] = x.T


def _rwkv_sample_step_kernel(s_ref, r_ref, w_ref, k_ref, v_ref, a_ref, b_ref, so_ref, y_ref, *, unroll):
    a, w, b, k, r = a_ref[...], w_ref[...], b_ref[...], k_ref[...], r_ref[...]

    def rows(i0, carry):
        base = pl.multiple_of(i0 * unroll, unroll)
        v8 = v_ref[pl.ds(base, unroll), :]
        s = [s_ref[0, base + u] for u in range(unroll)]
        sa = [jnp.sum(s[u] * a, axis=0, keepdims=True) for u in range(unroll)]
        s = [s[u] * w + sa[u] * b + v8[u:u + 1, :] * k for u in range(unroll)]
        for u in range(unroll):
            so_ref[0, base + u] = s[u]
        y_ref[pl.ds(base, unroll), :] = jnp.concatenate(
            [jnp.sum(s[u] * r, axis=0, keepdims=True) for u in range(unroll)], axis=0)
        return carry

    lax.fori_loop(0, HEAD // unroll, rows, 0)


def _rwkv_sample(p, prev, wkv_t, lp, unroll=8):
    n = p.shape[0]
    full = lambda a: pl.BlockSpec(a.shape, lambda i: (0,) * a.ndim)
    names = ("mu_shift", "w_decay0", "w_decay_up", "a0", "w_a_up", "w_g_up", "k_k", "k_a")
    prep_in = [p, prev] + [lp[k] for k in names]
    vecs = pl.pallas_call(
        _rwkv_sample_prep_kernel,
        grid=(1,),
        in_specs=[full(a) for a in prep_in],
        out_specs=[pl.BlockSpec((n, RWKV_WIDTH), lambda i: (0, 0))] * 4
        + [pl.BlockSpec((RWKV_WIDTH, n), lambda i: (0, 0))] * 6,
        out_shape=[jax.ShapeDtypeStruct((n, RWKV_WIDTH), F32)] * 4
        + [jax.ShapeDtypeStruct((RWKV_WIDTH, n), F32)] * 6,
        compiler_params=_params(("arbitrary",), VMEM_SMALL),
        name="rwkv_sample_prep",
    )(*prep_in)
    vec_t = pl.BlockSpec((HEAD, n), lambda h: (h, 0))
    st_spec = pl.BlockSpec((1, HEAD, HEAD, n), lambda h: (h, 0, 0, 0))
    wkv_new, y_t = pl.pallas_call(
        functools.partial(_rwkv_sample_step_kernel, unroll=unroll),
        grid=(N_HEADS,),
        in_specs=[st_spec] + [vec_t] * 6,
        out_specs=[st_spec, vec_t],
        out_shape=[jax.ShapeDtypeStruct(wkv_t.shape, F32), jax.ShapeDtypeStruct((RWKV_WIDTH, n), F32)],
        compiler_params=_params(("parallel",), VMEM_SMALL),
        name="rwkv_sample_step",
    )(wkv_t, *vecs[4:])
    return vecs[:4], y_t, wkv_new


def _rope_table_kernel(invf_ref, cos_ref, sin_ref, *, rows):
    pos = (pl.program_id(0) * rows + _iota((rows, 1), 0)).astype(F32)
    cos_ref[...], sin_ref[...] = _rope_tables(pos, invf_ref[...])


def _rope_table(t, inv_freq, rows):
    spec = pl.BlockSpec((rows, LANES), lambda i: (i, 0))
    return pl.pallas_call(
        functools.partial(_rope_table_kernel, rows=rows),
        grid=(t // rows,),
        in_specs=[pl.BlockSpec((1, LANES), lambda i: (0, 0))],
        out_specs=[spec, spec],
        out_shape=[jax.ShapeDtypeStruct((t, LANES), F32)] * 2,
        compiler_params=_params(("parallel",), VMEM_SMALL),
        name="rope_table",
    )(inv_freq)


def _swa_prompt_kernel(p_ref, cos_ref, sin_ref, sink_ref, y_ref, ko_ref, vo_ref, kprev_ref, vprev_ref, *, nb):
    n = pl.program_id(1)
    blk = WINDOW

    @pl.when(n == 0)
    def _init():
        kprev_ref[...] = jnp.zeros_like(kprev_ref)
        vprev_ref[...] = jnp.zeros_like(vprev_ref)

    p = p_ref[0]
    cos, sin_signed = cos_ref[...], sin_ref[...]
    q = _rope(p[:, :RWKV_WIDTH], cos, sin_signed, on_mxu=True) * (HEAD ** -0.5)
    k = _rope(p[:, RWKV_WIDTH:RWKV_WIDTH + KV_WIDTH], cos, sin_signed)
    v = p[:, RWKV_WIDTH + KV_WIDTH:]
    kcat = jnp.concatenate([kprev_ref[...], k], axis=0)
    vcat = jnp.concatenate([vprev_ref[...], v], axis=0)
    own = _iota((4 * blk, blk), 1) <= (_iota((4 * blk, blk), 0) & (blk - 1))
    no_prev = jnp.where(n == 0, NEG, 0.0)
    ones = jnp.ones((blk, blk), BF16)
    zero = jnp.zeros((), BF16)
    rows = lambda j, m: slice(j * blk, (j + m) * blk)
    heads = lambda g: range(4 * g, 4 * g + 4)
    cols = lambda g: slice(g * HEAD, (g + 1) * HEAD)
    sink_col = [jnp.concatenate([jnp.broadcast_to(sink_ref[0:1, h:h + 1], (blk, 1)) for h in heads(g)], axis=0)
                for g in range(2)]
    units = [(j, g) for j in range(nb) for g in range(2)]
    qg = {(j, g): _bf(jnp.concatenate([q[rows(j, 1), h * HEAD:(h + 1) * HEAD] for h in heads(g)], axis=0))
          for j, g in units}
    s2 = {(j, g): _dot_nt(qg[(j, g)], _bf(kcat[rows(j, 2), cols(g)])) for j, g in units}
    s = {(j, g): jnp.where(own, s2[(j, g)][:, blk:],
                           s2[(j, g)][:, :blk] + no_prev if j == 0 else s2[(j, g)][:, :blk]) for j, g in units}
    m = {u: jnp.maximum(jnp.max(s[u], axis=-1, keepdims=True), sink_col[u[1]]) for u in units}
    e = {u: _bf(jnp.exp(s[u] - m[u])) for u in units}
    inv = {u: 1.0 / (_dot(e[u], ones)[:, :HEAD] + jnp.exp(sink_col[u[1]] - m[u])) for u in units}
    e2 = {u: jnp.concatenate([jnp.where(own, zero, e[u]), jnp.where(own, e[u], zero)], axis=1) for u in units}
    o = {(j, g): _dot(e2[(j, g)], _bf(vcat[rows(j, 2), cols(g)])) * inv[(j, g)] for j, g in units}
    y_ref[0] = _bf(jnp.concatenate(
        [jnp.concatenate([o[(j, g)][i * blk:(i + 1) * blk] for g in range(2) for i in range(4)], axis=1)
         for j in range(nb)], axis=0))
    k_last, v_last = k[(nb - 1) * blk:], v[(nb - 1) * blk:]
    ko_ref[0] = k_last
    vo_ref[0] = v_last
    kprev_ref[...] = k_last
    vprev_ref[...] = v_last


def _swa_prompt(p, inv_freq, sink, nb=SWA_BLOCKS):
    bsz, t, _ = p.shape
    blk = WINDOW
    rows = nb * blk
    cos, sin_signed = _rope_table(t, inv_freq, rows)
    table = pl.BlockSpec((rows, LANES), lambda b, n: (n, 0))
    return pl.pallas_call(
        functools.partial(_swa_prompt_kernel, nb=nb),
        grid=(bsz, t // rows),
        in_specs=[pl.BlockSpec((1, rows, SWA_COLS), lambda b, n: (b, n, 0)), table, table,
                  pl.BlockSpec((1, N_HEADS), lambda b, n: (0, 0))],
        out_specs=[pl.BlockSpec((1, rows, RWKV_WIDTH), lambda b, n: (b, n, 0)),
                   pl.BlockSpec((1, blk, KV_WIDTH), lambda b, n: (b, 0, 0)),
                   pl.BlockSpec((1, blk, KV_WIDTH), lambda b, n: (b, 0, 0))],
        out_shape=[jax.ShapeDtypeStruct((bsz, t, RWKV_WIDTH), BF16),
                   jax.ShapeDtypeStruct((bsz, blk, KV_WIDTH), F32),
                   jax.ShapeDtypeStruct((bsz, blk, KV_WIDTH), F32)],
        scratch_shapes=[pltpu.VMEM((blk, KV_WIDTH), F32), pltpu.VMEM((blk, KV_WIDTH), F32)],
        compiler_params=_params(("arbitrary", "arbitrary"), VMEM_MID),
        name="swa_prompt",
    )(p, cos, sin_signed, sink)


def _swa_sample_kernel(p_ref, ck_ref, cv_ref, invf_ref, sinkc_ref, y_ref, nk_ref, nv_ref, *, bt):
    p = p_ref[...]
    cos, sin_signed = _rope_tables(jnp.full((1, 1), float(PAST_LEN), F32), invf_ref[...])
    q = _rope(p[:, :RWKV_WIDTH], cos, sin_signed) * (HEAD ** -0.5)
    k = _rope(p[:, RWKV_WIDTH:RWKV_WIDTH + KV_WIDTH], cos, sin_signed)
    v = p[:, RWKV_WIDTH + KV_WIDTH:]
    k_t, v_t = k.T, v.T
    last = _iota((KV_WIDTH, WINDOW), 1) == WINDOW - 1
    head_mask = _iota((N_HEADS, RWKV_WIDTH), 0) == (_iota((N_HEADS, RWKV_WIDTH), 1) >> _log2(HEAD))
    tile_lanes = lambda x: jnp.concatenate([x[:, :HEAD]] * 4 + [x[:, HEAD:]] * 4, axis=1)
    tile_rows = lambda x: jnp.concatenate([x[:HEAD]] * 4 + [x[HEAD:]] * 4, axis=0)
    seqs = range(bt)
    k_new = [jnp.where(last, k_t[:, bi:bi + 1], pltpu.roll(ck_ref[bi], WINDOW - 1, axis=1)) for bi in seqs]
    v_new = [jnp.where(last, v_t[:, bi:bi + 1], pltpu.roll(cv_ref[bi], WINDOW - 1, axis=1)) for bi in seqs]
    for bi in seqs:
        nk_ref[bi] = k_new[bi]
        nv_ref[bi] = v_new[bi]
    q_bd = [jnp.where(head_mask, q[bi:bi + 1, :], 0.0) for bi in seqs]
    s = [_dot3(q_bd[bi], tile_rows(k_new[bi])) for bi in seqs]
    pr = [_softmax_sink(s[bi], sinkc_ref[...]) for bi in seqs]
    o = [_dot3_nt(pr[bi], v_new[bi]) for bi in seqs]
    y_ref[...] = jnp.concatenate(
        [jnp.sum(jnp.where(head_mask, tile_lanes(o[bi]), 0.0), axis=0, keepdims=True) for bi in seqs], axis=0)


def _swa_sample(p, ck, cv, inv_freq, sink_col, bt=8):
    n = p.shape[0]
    cache_spec = pl.BlockSpec((bt, KV_WIDTH, WINDOW), lambda i: (i, 0, 0))
    return pl.pallas_call(
        functools.partial(_swa_sample_kernel, bt=bt),
        grid=(n // bt,),
        in_specs=[pl.BlockSpec((bt, SWA_COLS), lambda i: (i, 0)), cache_spec, cache_spec,
                  pl.BlockSpec((1, LANES), lambda i: (0, 0)), pl.BlockSpec((N_HEADS, 1), lambda i: (0, 0))],
        out_specs=[pl.BlockSpec((bt, RWKV_WIDTH), lambda i: (i, 0)), cache_spec, cache_spec],
        out_shape=[jax.ShapeDtypeStruct((n, RWKV_WIDTH), F32), jax.ShapeDtypeStruct(ck.shape, F32),
                   jax.ShapeDtypeStruct(cv.shape, F32)],
        compiler_params=_params(("parallel",), VMEM_SMALL),
        name="swa_sample",
    )(p, ck, cv, inv_freq, sink_col)


def _mix_out(yr, ys, x, wo1, wo2):
    return x + _dot(yr, wo1) + _dot(ys, wo2)


def _out_cross_kernel(yr_ref, ys_ref, x_ref, wo1_ref, wo2_ref, gc_ref, wq_ref, mk_ref, mv_ref, wco_ref,
                      qs_ref, mks_ref, mvs_ref, o_ref, os_ref, *, bt):
    _cross_sample(qs_ref, mks_ref, mvs_ref, os_ref, bt)
    x1 = _mix_out(yr_ref[0], ys_ref[0], x_ref[0], wo1_ref[...], wo2_ref[...])
    q = _bf(_dot(_bf(_rmsnorm(x1, gc_ref[...])), wq_ref[...]))
    mk, mv = mk_ref[0], mv_ref[0]
    heads = range(X_HEADS)
    sl = lambda h: slice(h * X_HEAD, (h + 1) * X_HEAD)
    s = [_dot_nt(q[:, sl(h)], mk[:, sl(h)]) * (X_HEAD ** -0.5) for h in heads]
    e = [jnp.exp(s[h] - jnp.max(s[h], axis=-1, keepdims=True)) for h in heads]
    outs = [_dot(_bf(e[h]), mv[:, sl(h)]) / jnp.sum(e[h], axis=-1, keepdims=True) for h in heads]
    o_ref[0] = x1 + _dot(_bf(jnp.concatenate(outs, axis=1)), wco_ref[...])


def _out_cross(yr, ys, x, wo1, wo2, gain, wq, mk, mv, wco, q_s, mk_s, mv_s, tm=PROJ_ROWS):
    bsz, t, d = x.shape
    n = q_s.shape[0]
    nt = t // tm
    bt = n // (bsz * nt)
    assert bt * bsz * nt == n
    tile = lambda w: pl.BlockSpec((1, tm, w), lambda b, i: (b, i, 0))
    const = lambda a: pl.BlockSpec(a.shape, lambda b, i: (0,) * a.ndim)
    mem = pl.BlockSpec((1, MEM_TOKENS, X_WIDTH), lambda b, i: (b, 0, 0))
    mem_shape = (n, MEM_TOKENS // 2, 2 * X_HEADS, X_HEAD)
    mem_s = pl.BlockSpec((bt,) + mem_shape[1:], lambda b, i: (b * nt + i, 0, 0, 0))
    vec_s = pl.BlockSpec((bt, X_HEADS, X_HEAD), lambda b, i: (b * nt + i, 0, 0))
    x2, o_s = pl.pallas_call(
        functools.partial(_out_cross_kernel, bt=bt),
        grid=(bsz, nt),
        in_specs=[tile(RWKV_WIDTH), tile(RWKV_WIDTH), tile(d), const(wo1), const(wo2), const(gain), const(wq),
                  mem, mem, const(wco), vec_s, mem_s, mem_s],
        out_specs=[tile(d), vec_s],
        out_shape=[jax.ShapeDtypeStruct((bsz, t, d), F32), jax.ShapeDtypeStruct((n, X_HEADS, X_HEAD), F32)],
        compiler_params=_params(("parallel", "parallel"), VMEM_LARGE),
        name="out_cross",
    )(yr, ys, x, wo1, wo2, gain, wq, mk, mv, wco, q_s.reshape(n, X_HEADS, X_HEAD), mk_s.reshape(mem_shape),
      mv_s.reshape(mem_shape))
    return x2, o_s.reshape(n, X_WIDTH)


def _out_q_kernel(yt_ref, r_ref, k_ref, v_ref, g_ref, rk_ref, lnw_ref, lnb_ref, ys_ref, x_ref, wo1_ref, wo2_ref,
                  gc_ref, wq_ref, x1_ref, q_ref):
    yr = _rwkv_post(yt_ref[...].T, r_ref[...], k_ref[...], v_ref[...], g_ref[...], rk_ref[...], lnw_ref[...],
                    lnb_ref[...], True)
    x1 = x_ref[...] + _dot3(yr, wo1_ref[...]) + _dot3(ys_ref[...], wo2_ref[...])
    x1_ref[...] = x1
    q_ref[...] = _dot3(_rmsnorm(x1, gc_ref[...]), wq_ref[...])


def _cross_sample(q_ref, mk_ref, mv_ref, o_ref, bt):
    seqs = range(bt)
    both = lambda x: x + pltpu.roll(x, X_HEADS, axis=0)
    q8 = [jnp.concatenate([q_ref[bi]] * 2, axis=0) for bi in seqs]
    s = [jnp.sum(mk_ref[bi] * q8[bi][None], axis=-1, keepdims=True) * (X_HEAD ** -0.5) for bi in seqs]
    mx = [jnp.max(s[bi], axis=0) for bi in seqs]
    mx = [jnp.maximum(m, pltpu.roll(m, X_HEADS, axis=0)) for m in mx]
    e = [jnp.exp(s[bi] - mx[bi][None]) for bi in seqs]
    den = [both(jnp.sum(e[bi], axis=0)) for bi in seqs]
    o8 = [jnp.sum(e[bi] * mv_ref[bi], axis=0) / den[bi] for bi in seqs]
    for bi in seqs:
        o_ref[bi] = both(o8[bi])[:X_HEADS]


def _lin_res_kernel(x_ref, a_ref, w_ref, o_ref):
    o_ref[...] = x_ref[...] + _dot3(a_ref[...], w_ref[...])


def _single_step(kernel_fn, out_shape, name, *args):
    full = lambda a: pl.BlockSpec(a.shape, lambda i: (0,) * len(a.shape))
    outs = out_shape if isinstance(out_shape, (list, tuple)) else [out_shape]
    out_specs = [full(o) for o in outs]
    return pl.pallas_call(
        kernel_fn, grid=(1,), in_specs=[full(a) for a in args],
        out_specs=out_specs if isinstance(out_shape, (list, tuple)) else out_specs[0],
        out_shape=out_shape, compiler_params=_params(("arbitrary",), VMEM_SMALL), name=name,
    )(*args)


def _out_q_sample(rwkv_parts, ys, x, wo1, wo2, gain, wq):
    n, d = x.shape
    return _single_step(_out_q_kernel, [jax.ShapeDtypeStruct((n, d), F32), jax.ShapeDtypeStruct((n, X_WIDTH), F32)],
                        "out_q_sample", *rwkv_parts, ys, x, wo1, wo2, gain, wq)


def _cross_out_sample(x1, o, wco):
    return _single_step(_lin_res_kernel, jax.ShapeDtypeStruct(x1.shape, F32), "cross_out_sample", x1, o, wco)


def _route(logits):
    lane_i = _iota(logits.shape, 1)
    lane = lane_i.astype(F32)
    big = 1024.0
    n_exp = float(N_GROUPS * EXPERTS_PER_GROUP)
    lg = jnp.where((lane >= n_exp) & (lane < n_exp + N_GROUPS), logits, NEG)
    g_max = jnp.max(lg, axis=-1, keepdims=True)
    g_idx = jnp.min(jnp.where(lg == g_max, lane, big), axis=-1, keepdims=True) - n_exp
    p_sel = 1.0 / jnp.sum(jnp.exp(lg - g_max), axis=-1, keepdims=True)
    le = jnp.where((lane_i >> _log2(EXPERTS_PER_GROUP)).astype(F32) == g_idx, logits, NEG)
    t1 = jnp.max(le, axis=-1, keepdims=True)
    i1 = jnp.min(jnp.where(le == t1, lane, big), axis=-1, keepdims=True)
    le2 = jnp.where(lane == i1, NEG, le)
    t2 = jnp.max(le2, axis=-1, keepdims=True)
    i2 = jnp.min(jnp.where(le2 == t2, lane, big), axis=-1, keepdims=True)
    e2 = jnp.exp(t2 - t1)
    w1 = 1.0 / (1.0 + e2)
    return p_sel * (jnp.where(lane == i1, w1, 0.0) + jnp.where(lane == i2, e2 * w1, 0.0)), g_idx


def _moe_kernel(x_ref, gf_ref, wr_ref, br_ref, wg_ref, wu_ref, wd_ref, gfin_ref, tri_ref, o_ref, h_ref, gate_ref,
                pos_ref, acc_ref, *, tm, blocks, precise):
    g = pl.program_id(1)

    @pl.when(g == 0)
    def _init():
        h = _rmsnorm(x_ref[...], gf_ref[...])
        h_ref[...] = _bf(h)
        gate, g_idx = _route((_dot3 if precise else _dot1)(h, wr_ref[...]) + br_ref[...])
        gate_ref[...] = gate
        lane = _iota((tm, LANES), 1).astype(F32)
        own = lane == g_idx
        count = _dot(tri_ref[...], _mask_bf(own))
        pos_ref[...] = jnp.where(own, count, -1.0)
        acc_ref[...] = jnp.zeros_like(acc_ref)

    lane = _iota((tm, LANES), 1)
    pos = jnp.sum(jnp.where(lane == g, pos_ref[...], 0.0), axis=-1, keepdims=True)
    n_g = jnp.sum(jnp.where(pos >= 0.0, 1.0, 0.0))
    width = EXPERTS_PER_GROUP * EXPERT_HIDDEN
    expand = _mask_bf(_iota((LANES, width), 0)
                      == g * EXPERTS_PER_GROUP + (_iota((LANES, width), 1) >> _log2(EXPERT_HIDDEN)))
    start = 0
    for bm in blocks:
        @pl.when(n_g > start)
        def _block(start=start, bm=bm):
            sel = _mask_bf(pos - float(start) == _iota((tm, bm), 1).astype(F32))
            hs = _bf(_dot_tn(sel, h_ref[...]))
            if precise:
                gh, gl = _split(gate_ref[...])
                gs = _dot_tn(sel, gh) + _dot_tn(sel, gl)
            else:
                gs = _dot_tn(sel, _bf(gate_ref[...]))
            hg = _dot(hs, wg_ref[0])
            hu = _dot(hs, wu_ref[0])
            act = hg * _sigmoid(hg) * hu * (_dot_sel_rhs if precise else _dot1)(gs, expand)
            acc_ref[...] += _dot(sel, _bf(_dot(_bf(act), wd_ref[0])))

        start += bm

    @pl.when(g == N_GROUPS - 1)
    def _fin():
        o_ref[...] = _rmsnorm(x_ref[...] + acc_ref[...], gfin_ref[...])


def _moe(x, gain, w_router, b_router, wg, wu, wd, gain_final, tm, blocks, precise):
    n, d = x.shape
    assert sum(blocks) == tm
    width = EXPERTS_PER_GROUP * EXPERT_HIDDEN
    const = lambda a: pl.BlockSpec(a.shape, lambda i, g: (0,) * a.ndim)
    earlier = jnp.tril(jnp.ones((tm, tm), BF16), -1)
    return pl.pallas_call(
        functools.partial(_moe_kernel, tm=tm, blocks=blocks, precise=precise),
        grid=(n // tm, N_GROUPS),
        in_specs=[pl.BlockSpec((tm, d), lambda i, g: (i, 0)), const(gain), const(w_router), const(b_router),
                  pl.BlockSpec((1, d, width), lambda i, g: (g, 0, 0)),
                  pl.BlockSpec((1, d, width), lambda i, g: (g, 0, 0)),
                  pl.BlockSpec((1, width, d), lambda i, g: (g, 0, 0)), const(gain_final), const(earlier)],
        out_specs=pl.BlockSpec((tm, d), lambda i, g: (i, 0)),
        out_shape=jax.ShapeDtypeStruct((n, d), F32),
        scratch_shapes=[pltpu.VMEM((tm, d), BF16), pltpu.VMEM((tm, LANES), F32), pltpu.VMEM((tm, LANES), F32),
                        pltpu.VMEM((tm, d), F32)],
        compiler_params=_params(("parallel", "arbitrary"), VMEM_LARGE),
        name="moe",
    )(x, gain, w_router, b_router, wg, wu, wd, gain_final, earlier)


def kernel(x_prompt, x_sample, mem_prompt, state_rwkv_shift, state_rwkv_wkv, cache_swa_k, cache_swa_v, cache_mem_k, cache_mem_v, norm_mix, w_in, mu_shift, w_decay0, w_decay_up, a0, w_a_up, w_g_up, k_k, k_a, r_k, ln_x_w, ln_x_b, attn_sink, w_out, norm_cross, norm_mem, w_cq, w_ck, w_cv, w_co, norm_ffn, w_group_router, b_group_router, w_expert_router, b_expert_router, w_exp_gate, w_exp_up, w_exp_down, norm_final):
    bsz, t, d = x_prompt.shape
    n_s = x_sample.shape[0]
    wb = cache_swa_k.shape[2]
    width = EXPERTS_PER_GROUP * EXPERT_HIDDEN

    w_in_bf = w_in[0].astype(BF16)
    wo1, wo2 = w_out[0, :RWKV_WIDTH].astype(BF16), w_out[0, RWKV_WIDTH:].astype(BF16)
    wq, wco = w_cq[0].astype(BF16), w_co[0].astype(BF16)
    w_ckv = jnp.concatenate([w_ck[0], w_cv[0]], axis=1).astype(BF16)
    pad = LANES - N_GROUPS * (EXPERTS_PER_GROUP + 1)
    w_router = jnp.pad(jnp.concatenate([w_expert_router[0], w_group_router[0]], axis=1), ((0, 0), (0, pad)))
    b_router = jnp.pad(jnp.concatenate([b_expert_router[0], b_group_router[0]]), (0, pad)).reshape(1, LANES)
    wg = jnp.transpose(w_exp_gate[0], (0, 2, 1, 3)).reshape(N_GROUPS, d, width).astype(BF16)
    wu = jnp.transpose(w_exp_up[0], (0, 2, 1, 3)).reshape(N_GROUPS, d, width).astype(BF16)
    wd = w_exp_down[0].reshape(N_GROUPS, width, d).astype(BF16)
    row = lambda a: a.reshape(1, -1)
    lp = dict(mu_shift=mu_shift, w_decay0=w_decay0, w_decay_up=w_decay_up[0], a0=a0, w_a_up=w_a_up[0],
              w_g_up=w_g_up[0], k_k=k_k, k_a=k_a, r_k=row(r_k), ln_x_w=ln_x_w, ln_x_b=ln_x_b)
    half = HEAD // 2
    inv_freq = ROPE_THETA ** (-jnp.arange(half, dtype=F32) * 2.0 / HEAD)
    inv_freq = jnp.tile(inv_freq, LANES // half).reshape(1, LANES)

    xs = x_sample.reshape(n_s, d)
    ps_rwkv, ps_swa = _norm_matmul(xs, norm_mix, w_in[0], RWKV_COLS, n_s)
    (r_s, k2_s, v_s_tok, g_s), yt_s, wkv_s = _rwkv_sample(ps_rwkv, state_rwkv_shift[0],
                                                           jnp.transpose(state_rwkv_wkv[0], (1, 2, 3, 0)), lp)
    cache_t = lambda c: jnp.transpose(c[0].reshape(n_s, wb, KV_WIDTH), (0, 2, 1))
    ys_swa, k_s, v_s = _swa_sample(ps_swa, cache_t(cache_swa_k), cache_t(cache_swa_v), inv_freq,
                                   attn_sink.reshape(N_HEADS, 1))
    x1s, q_s = _out_q_sample((yt_s, r_s, k2_s, v_s_tok, g_s, lp["r_k"], lp["ln_x_w"], lp["ln_x_b"]), ys_swa, xs,
                             w_out[0, :RWKV_WIDTH], w_out[0, RWKV_WIDTH:], norm_cross, w_cq[0])

    xm_rwkv, shift_p, p_swa = _in_proj_shift(x_prompt.reshape(bsz * t, d), norm_mix, w_in_bf, mu_shift, RWKV_COLS,
                                             PROJ_ROWS, t)
    y_rwkv, wkv_p = _rwkv_prompt(xm_rwkv.reshape(bsz, t, RWKV_COLS), lp)
    y_swa, k_p, v_p = _swa_prompt(p_swa.reshape(bsz, t, SWA_COLS), inv_freq, attn_sink)
    mk_p, mv_p = _norm_matmul(mem_prompt.reshape(bsz * MEM_TOKENS, d), norm_mem, w_ckv, X_WIDTH, MEMKV_ROWS)
    mk_b = mk_p.astype(BF16).reshape(bsz, MEM_TOKENS, X_WIDTH)
    mv_b = mv_p.astype(BF16).reshape(bsz, MEM_TOKENS, X_WIDTH)
    x2, o_s = _out_cross(y_rwkv, y_swa, x_prompt, wo1, wo2, norm_cross, wq, mk_b, mv_b, wco, q_s, cache_mem_k[0],
                         cache_mem_v[0])
    y_prompt = _moe(x2.reshape(bsz * t, d), norm_ffn, w_router, b_router, wg, wu, wd, row(norm_final), MOE_ROWS,
                    MOE_BLOCKS, False)
    y_sample = _moe(_cross_out_sample(x1s, o_s, w_co[0]), norm_ffn, w_router, b_router, wg, wu, wd,
                    row(norm_final), n_s, (n_s,), True)

    kv_shape = (1, -1, wb, KV_WIDTH // HEAD, HEAD)
    mem_shape = (1, bsz, MEM_TOKENS, X_HEADS, X_HEAD)
    cache_back = lambda c: jnp.transpose(c, (0, 2, 1)).reshape(kv_shape)
    return (y_prompt.reshape(bsz, t, d), y_sample.reshape(n_s, 1, d),
            shift_p.reshape(1, bsz, RWKV_COLS), wkv_p[None], k_p.reshape(kv_shape), v_p.reshape(kv_shape),
            mk_p.reshape(mem_shape), mv_p.reshape(mem_shape),
            ps_rwkv[None], jnp.transpose(wkv_s, (3, 0, 1, 2))[None], cache_back(k_s), cache_back(v_s))
```

```python
import functools
import math

import jax
import jax.numpy as jnp
from jax import lax
from jax.experimental import pallas as pl
from jax.experimental.pallas import tpu as pltpu

F32 = jnp.float32
BF16 = jnp.bfloat16

D_MODEL = 1024
RWKV_WIDTH = 512
HEAD = 64
N_HEADS = 8
RWKV_COLS = 1792
SWA_COLS = 768
KV_WIDTH = 128
WINDOW = 128
PAST_LEN = 8192
ROPE_THETA = 10000.0
MEM_TOKENS = 256
X_HEADS = 4
X_HEAD = 128
X_WIDTH = 512
N_GROUPS = 4
EXPERTS_PER_GROUP = 8
EXPERT_HIDDEN = 128
NORM_EPS = 1e-5
GN_EPS = 64e-5
L2_EPS = 1e-12
CHUNK = 64
NEG = -1e30

LANES = 128
MIB = 1024 * 1024
VMEM_SMALL, VMEM_MID, VMEM_LARGE = 32, 48, 63

PROJ_ROWS = 1024
MEMKV_ROWS = 256
RWKV_ROWS = 8 * CHUNK
SWA_BLOCKS = 8
MOE_ROWS = 1024
MOE_BLOCKS = (256, 128, 128, 256, 256)


def _log2(n):
    assert n & (n - 1) == 0
    return n.bit_length() - 1


def _params(semantics, vmem_mib):
    return pltpu.CompilerParams(dimension_semantics=semantics, vmem_limit_bytes=vmem_mib * MIB)


def _bf(x):
    return x.astype(BF16)


def _dot(a, b):
    return jnp.dot(a, b, preferred_element_type=F32)


def _dot_nt(a, b):
    return lax.dot_general(a, b, (((1,), (1,)), ((), ())), preferred_element_type=F32)


def _dot_tn(a, b):
    return lax.dot_general(a, b, (((0,), (0,)), ((), ())), preferred_element_type=F32)


def _split(x):
    hi = x.astype(BF16)
    lo = (x - hi.astype(F32)).astype(BF16)
    return hi, lo


def _dot3(a, b, dot=_dot):
    ah, al = _split(a)
    bh, bl = _split(b)
    return dot(ah, bh) + (dot(ah, bl) + dot(al, bh))


def _dot3_nt(a, b):
    return _dot3(a, b, _dot_nt)


def _dot_sel_rhs(a, sel):
    ah, al = _split(a)
    return _dot(ah, sel) + _dot(al, sel)


def _dot_sel_lhs(sel, b):
    bh, bl = _split(b)
    return _dot(sel, bh) + _dot(sel, bl)


def _rmsnorm(x, gain):
    return x * lax.rsqrt(jnp.mean(x * x, axis=-1, keepdims=True) + NORM_EPS) * gain


def _sigmoid(x):
    return 1.0 / (1.0 + jnp.exp(-x))


def _iota(shape, dim):
    return lax.broadcasted_iota(jnp.int32, shape, dim)


def _mask_bf(cond):
    return jnp.where(cond, 1.0, 0.0).astype(BF16)


def _block_ones(n, blk):
    return _mask_bf((_iota((n, n), 0) >> _log2(blk)) == (_iota((n, n), 1) >> _log2(blk)))


def _softmax_sink(s, sink_col):
    m = jnp.maximum(jnp.max(s, axis=-1, keepdims=True), sink_col)
    e = jnp.exp(s - m)
    den = jnp.sum(e, axis=-1, keepdims=True) + jnp.exp(sink_col - m)
    return e / den


def _rope(x, cos, sin_signed, on_mxu=False):
    width = x.shape[1]
    reps = width // LANES
    if on_mxu:
        src, dst = _iota((LANES, LANES), 0), _iota((LANES, LANES), 1)
        partner = jnp.where((dst & (HEAD - 1)) < HEAD // 2, dst + HEAD // 2, dst - HEAD // 2)
        swap = _mask_bf(src == partner)
        rot = jnp.concatenate([_dot(_bf(x[:, c * LANES:(c + 1) * LANES]), swap) for c in range(reps)], axis=1)
    else:
        first_half = (_iota(x.shape, 1) & (HEAD - 1)) < HEAD // 2
        rot = jnp.where(first_half, pltpu.roll(x, width - HEAD // 2, axis=1), pltpu.roll(x, HEAD // 2, axis=1))
    if reps > 1:
        cos = jnp.concatenate([cos] * reps, axis=1)
        sin_signed = jnp.concatenate([sin_signed] * reps, axis=1)
    return x * cos + rot * sin_signed


def _rope_tables(pos, inv_freq):
    ang = pos * inv_freq
    cos, sin = jnp.cos(ang), jnp.sin(ang)
    first_half = (_iota(ang.shape, 1) & (HEAD - 1)) < HEAD // 2
    return cos, jnp.where(first_half, -sin, sin)


def _norm_matmul_kernel(x_ref, g_ref, w_ref, o1_ref, o2_ref, *, split):
    h = _rmsnorm(x_ref[...], g_ref[...])
    w = w_ref[...]
    p = _dot(_bf(h), w) if w.dtype == BF16 else _dot3(h, w)
    o1_ref[...] = p[:, :split]
    o2_ref[...] = p[:, split:]


def _in_proj_shift_kernel(x_ref, g_ref, w_ref, mu_ref, xm_ref, last_ref, o2_ref, carry_ref, *, split, steps_per_seq):
    i = pl.program_id(0)

    @pl.when(i % steps_per_seq == 0)
    def _start():
        carry_ref[...] = jnp.zeros_like(carry_ref)

    p = _dot(_bf(_rmsnorm(x_ref[...], g_ref[...])), w_ref[...])
    p1 = p[:, :split]
    rows = p1.shape[0]
    prev = jnp.where(_iota((rows, 1), 0) == 0, carry_ref[0:1, :], pltpu.roll(p1, 1, axis=0))
    xm_ref[...] = p1 + (prev - p1) * mu_ref[...]
    carry_ref[0:1, :] = p1[rows - 1:rows, :]
    last_ref[0] = p1[rows - 1:rows, :]
    o2_ref[...] = p[:, split:]


def _in_proj_shift(x, gain, w, mu, split, tm, seq_len):
    n, d = x.shape
    cols = w.shape[1]
    steps_per_seq = seq_len // tm
    return pl.pallas_call(
        functools.partial(_in_proj_shift_kernel, split=split, steps_per_seq=steps_per_seq),
        grid=(n // tm,),
        in_specs=[pl.BlockSpec((tm, d), lambda i: (i, 0)),
                  pl.BlockSpec((1, d), lambda i: (0, 0)),
                  pl.BlockSpec((d, cols), lambda i: (0, 0)),
                  pl.BlockSpec((1, split), lambda i: (0, 0))],
        out_specs=[pl.BlockSpec((tm, split), lambda i: (i, 0)),
                   pl.BlockSpec((1, 1, split), lambda i: (i // steps_per_seq, 0, 0)),
                   pl.BlockSpec((tm, cols - split), lambda i: (i, 0))],
        out_shape=[jax.ShapeDtypeStruct((n, split), F32), jax.ShapeDtypeStruct((n // seq_len, 1, split), F32),
                   jax.ShapeDtypeStruct((n, cols - split), F32)],
        scratch_shapes=[pltpu.VMEM((8, split), F32)],
        compiler_params=_params(("arbitrary",), VMEM_LARGE),
        name="in_proj_shift",
    )(x, gain, w, mu)


def _norm_matmul(x, gain, w, split, tm):
    n, d = x.shape
    cols = w.shape[1]
    return pl.pallas_call(
        functools.partial(_norm_matmul_kernel, split=split),
        grid=(n // tm,),
        in_specs=[pl.BlockSpec((tm, d), lambda i: (i, 0)),
                  pl.BlockSpec((1, d), lambda i: (0, 0)),
                  pl.BlockSpec((d, cols), lambda i: (0, 0))],
        out_specs=[pl.BlockSpec((tm, split), lambda i: (i, 0)),
                   pl.BlockSpec((tm, cols - split), lambda i: (i, 0))],
        out_shape=[jax.ShapeDtypeStruct((n, split), F32), jax.ShapeDtypeStruct((n, cols - split), F32)],
        compiler_params=_params(("parallel",), VMEM_MID),
        name="norm_matmul",
    )(x, gain, w)


def _dot1(a, b):
    return _dot(_bf(a), _bf(b))


def _head_sums(precise):
    dot = _dot_sel_rhs if precise else _dot1
    ones_pair = _block_ones(LANES, HEAD)
    return lambda x: jnp.concatenate(
        [dot(x[:, c * LANES:(c + 1) * LANES], ones_pair) for c in range(x.shape[1] // LANES)], axis=1)


def _rwkv_prep(xm, w0, wd, a0, wa, wg, k_k, k_a, precise):
    mm, head_sum = (_dot3 if precise else _dot1), _head_sums(precise)
    c = RWKV_WIDTH
    r, k, v = xm[:, :c], xm[:, c:2 * c], xm[:, 2 * c:3 * c]
    xw, xa, xg = xm[:, 3 * c:3 * c + 64], xm[:, 3 * c + 64:3 * c + 128], xm[:, 3 * c + 128:]
    logw = -math.exp(-0.5) * _sigmoid(w0 + _dot3(jnp.tanh(xw), wd))
    rate = _sigmoid(a0 + mm(xa, wa))
    gate = mm(_sigmoid(xg), wg)
    kk = k * k_k
    kk = kk * lax.rsqrt(jnp.maximum(head_sum(kk * kk), L2_EPS * L2_EPS))
    k2 = k * (1.0 + (rate - 1.0) * k_a)
    return r, logw, k2, v, -kk, kk * rate, gate


def _rwkv_post(y, r, k2, v, gate, r_k, ln_w, ln_b, precise):
    head_sum = _head_sums(precise)
    mean = head_sum(y) * (1.0 / HEAD)
    d = y - mean
    var = head_sum(d * d) * (1.0 / HEAD)
    yn = d * lax.rsqrt(var + GN_EPS) * ln_w + ln_b
    bonus = head_sum(r * k2 * r_k) * v
    return (yn + bonus) * gate


def _rwkv_prompt_kernel(xm_ref, w0_ref, wd_ref, a0_ref, wa_ref, wg_ref, kk_ref, ka_ref, rk_ref, lnw_ref,
                        lnb_ref, y_ref, s_ref, st_ref, ar_ref, bk_ref, bp_ref, kp_ref, v_ref, yacc_ref, *, tc):
    c = pl.program_id(1)

    @pl.when(c == 0)
    def _init():
        st_ref[...] = jnp.zeros_like(st_ref)

    r, logw, k2, v, a, b, gate = _rwkv_prep(xm_ref[0], w0_ref[...], wd_ref[...], a0_ref[...], wa_ref[...],
                                             wg_ref[...], kk_ref[...], ka_ref[...], False)
    ri, ci = _iota((LANES, LANES), 0), _iota((LANES, LANES), 1)
    tri = _mask_bf(((ri >> _log2(CHUNK)) == (ci >> _log2(CHUNK))) & (ci <= ri))
    cum = jnp.concatenate([_dot_sel_lhs(tri, logw[i0:i0 + LANES]) for i0 in range(0, tc, LANES)], axis=0)
    n_ch = tc // CHUNK
    p_end = [jnp.exp(cum[(ci_ + 1) * CHUNK - 1:(ci_ + 1) * CHUNK, :]) for ci_ in range(n_ch)]
    e_inv = jnp.exp(-cum)
    e_end = e_inv * jnp.concatenate([jnp.broadcast_to(pe, (CHUNK, RWKV_WIDTH)) for pe in p_end], axis=0)
    ar_ref[0], ar_ref[1] = _bf(a * jnp.exp(cum - logw)), _bf(r * jnp.exp(cum))
    bk_ref[0], bk_ref[1] = _bf(b * e_inv), _bf(k2 * e_inv)
    bp_ref[...] = _bf(b * e_end)
    kp_ref[...] = _bf(k2 * e_end)
    v_ref[...] = _bf(v)

    rr, cc = _iota((CHUNK, 2 * HEAD), 0), _iota((CHUNK, 2 * HEAD), 1) & (HEAD - 1)
    strict, incl, eye = cc < rr, cc <= rr, cc == rr
    lane = _iota((CHUNK, 2 * HEAD), 1)
    left, right = lane < HEAD, lane >= HEAD
    zero = jnp.zeros((), BF16)

    def stack(x):
        return jnp.concatenate([jnp.where(left, x, zero.astype(x.dtype)), jnp.where(right, x, zero.astype(x.dtype))],
                               axis=0)

    def diag_blocks(x):
        return jnp.where(left, x[:HEAD], x[HEAD:])

    n_ch, n_pairs = tc // CHUNK, N_HEADS // 2
    ids = [(ci_, pr) for ci_ in range(n_ch) for pr in range(n_pairs)]
    rows = lambda ci_: slice(ci_ * CHUNK, (ci_ + 1) * CHUNK)
    cols = lambda pr: slice(pr * 2 * HEAD, (pr + 1) * 2 * HEAD)
    at = {i: ar_ref[0, rows(i[0]), cols(i[1])] for i in ids}
    rt = {i: ar_ref[1, rows(i[0]), cols(i[1])] for i in ids}
    vv = {i: v_ref[rows(i[0]), cols(i[1])] for i in ids}
    bp = {i: bp_ref[rows(i[0]), cols(i[1])] for i in ids}
    g = {i: _dot_nt(jnp.concatenate([at[i], rt[i]], axis=0),
                    jnp.concatenate([stack(bk_ref[0, rows(i[0]), cols(i[1])]),
                                     stack(bk_ref[1, rows(i[0]), cols(i[1])])], axis=0)) for i in ids}
    vk = {i: diag_blocks(_dot_tn(vv[i], kp_ref[rows(i[0]), cols(i[1])])) for i in ids}
    a_ab = {i: jnp.where(strict, g[i][:CHUNK, :2 * HEAD], 0.0) for i in ids}
    a_rb = {i: _bf(jnp.where(incl, g[i][CHUNK:, :2 * HEAD], 0.0)) for i in ids}
    av = {i: _dot(_bf(jnp.concatenate([jnp.where(strict, g[i][:CHUNK, 2 * HEAD:], 0.0),
                                       jnp.where(incl, g[i][CHUNK:, 2 * HEAD:], 0.0)], axis=0)), stack(vv[i]))
          for i in ids}
    t = {i: jnp.where(eye, 1.0, a_ab[i]) for i in ids}
    apow = {i: _bf(a_ab[i]) for i in ids}
    apow = {i: _bf(_dot(apow[i], stack(apow[i]))) for i in ids}
    for step in range(4):
        both = {i: _dot(jnp.concatenate([_bf(t[i]), apow[i]], axis=0), stack(apow[i])) for i in ids}
        t = {i: t[i] + both[i][:CHUNK] for i in ids}
        apow = {i: _bf(both[i][CHUNK:]) for i in ids}
    t = {i: t[i] + _dot(_bf(t[i]), stack(apow[i])) for i in ids}
    wu = {i: _bf(_dot(_bf(t[i]), jnp.concatenate([stack(at[i]), stack(_bf(av[i][:CHUNK]))], axis=1)))
          for i in ids}
    wy_y0 = {i: jnp.concatenate([rt[i].astype(F32), av[i][CHUNK:]], axis=1)
             + _dot(a_rb[i], jnp.concatenate([stack(wu[i][:, :2 * HEAD]), stack(wu[i][:, 2 * HEAD:])], axis=1))
             for i in ids}
    mbn = {i: _dot_tn(wu[i], bp[i]) for i in ids}
    rb, cb = _iota((2 * HEAD, 2 * HEAD), 0) < HEAD, _iota((2 * HEAD, 2 * HEAD), 1) < HEAD
    mb = {i: _bf(jnp.where(rb == cb, mbn[i][:2 * HEAD], 0.0)) for i in ids}
    n0 = {i: vk[i] + diag_blocks(mbn[i][2 * HEAD:]) for i in ids}
    state = [st_ref[:, cols(pr)] for pr in range(n_pairs)]
    for ci_ in range(n_ch):
        sb = [_bf(s) for s in state]
        ys = [wy_y0[(ci_, pr)][:, 2 * HEAD:] + _dot_nt(_bf(wy_y0[(ci_, pr)][:, :2 * HEAD]), stack(sb[pr]))
              for pr in range(n_pairs)]
        state = [state[pr] * p_end[ci_][:, cols(pr)] + _dot(sb[pr], mb[(ci_, pr)])
                 + n0[(ci_, pr)] for pr in range(n_pairs)]
        for pr in range(n_pairs):
            yacc_ref[rows(ci_), cols(pr)] = ys[pr]
    for pr in range(n_pairs):
        st_ref[:, cols(pr)] = state[pr]
    y_ref[0] = _bf(_rwkv_post(yacc_ref[...], r, k2, v, gate, rk_ref[...], lnw_ref[...], lnb_ref[...], False))
    for h in range(N_HEADS):
        s_ref[0, h] = state[h // 2][:, (h % 2) * HEAD:(h % 2 + 1) * HEAD]


def _rwkv_prompt(xm, lp, tc=RWKV_ROWS):
    bsz, t, _ = xm.shape
    row = lambda n: pl.BlockSpec((1, n), lambda b, c: (0, 0))
    mat = lambda m, n: pl.BlockSpec((m, n), lambda b, c: (0, 0))
    tile = lambda dt: pltpu.VMEM((tc, RWKV_WIDTH), dt)
    two = pltpu.VMEM((2, tc, RWKV_WIDTH), BF16)
    return pl.pallas_call(
        functools.partial(_rwkv_prompt_kernel, tc=tc),
        grid=(bsz, t // tc),
        in_specs=[pl.BlockSpec((1, tc, RWKV_COLS), lambda b, c: (b, c, 0)),
                  row(512), mat(64, 512), row(512), mat(64, 512), mat(128, 512),
                  row(512), row(512), row(512), row(512), row(512)],
        out_specs=[pl.BlockSpec((1, tc, RWKV_WIDTH), lambda b, c: (b, c, 0)),
                   pl.BlockSpec((1, N_HEADS, HEAD, HEAD), lambda b, c: (b, 0, 0, 0))],
        out_shape=[jax.ShapeDtypeStruct((bsz, t, RWKV_WIDTH), BF16),
                   jax.ShapeDtypeStruct((bsz, N_HEADS, HEAD, HEAD), F32)],
        scratch_shapes=[pltpu.VMEM((HEAD, RWKV_WIDTH), F32), two, two, tile(BF16), tile(BF16), tile(BF16),
                        tile(F32)],
        compiler_params=_params(("arbitrary", "arbitrary"), VMEM_MID),
        name="rwkv_prompt",
    )(xm, lp["w_decay0"], lp["w_decay_up"], lp["a0"], lp["w_a_up"], lp["w_g_up"], lp["k_k"],
      lp["k_a"], lp["r_k"], lp["ln_x_w"], lp["ln_x_b"])


def _rwkv_sample_prep_kernel(p_ref, prev_ref, mu_ref, w0_ref, wd_ref, a0_ref, wa_ref, wg_ref, kk_ref, ka_ref,
                             r_ref, k_ref, v_ref, g_ref, *t_refs):
    p = p_ref[...]
    xm = p + (prev_ref[...] - p) * mu_ref[...]
    r, logw, k2, v, a, b, gate = _rwkv_prep(xm, w0_ref[...], wd_ref[...], a0_ref[...], wa_ref[...], wg_ref[...],
                                             kk_ref[...], ka_ref[...], True)
    r_ref[...] = r
    k_ref[...] = k2
    v_ref[...] = v
    g_ref[...] = gate
    for ref, x in zip(t_refs, (r, jnp.exp(logw), k2, v, a, b)):
        ref[...] = x.T


def _rwkv_sample_step_kernel(s_ref, r_ref, w_ref, k_ref, v_ref, a_ref, b_ref, so_ref, y_ref, *, unroll):
    a, w, b, k, r = a_ref[...], w_ref[...], b_ref[...], k_ref[...], r_ref[...]

    def rows(i0, carry):
        base = pl.multiple_of(i0 * unroll, unroll)
        v8 = v_ref[pl.ds(base, unroll), :]
        s = [s_ref[0, base + u] for u in range(unroll)]
        sa = [jnp.sum(s[u] * a, axis=0, keepdims=True) for u in range(unroll)]
        s = [s[u] * w + sa[u] * b + v8[u:u + 1, :] * k for u in range(unroll)]
        for u in range(unroll):
            so_ref[0, base + u] = s[u]
        y_ref[pl.ds(base, unroll), :] = jnp.concatenate(
            [jnp.sum(s[u] * r, axis=0, keepdims=True) for u in range(unroll)], axis=0)
        return carry

    lax.fori_loop(0, HEAD // unroll, rows, 0)


def _rwkv_sample(p, prev, wkv_t, lp, unroll=8):
    n = p.shape[0]
    full = lambda a: pl.BlockSpec(a.shape, lambda i: (0,) * a.ndim)
    names = ("mu_shift", "w_decay0", "w_decay_up", "a0", "w_a_up", "w_g_up", "k_k", "k_a")
    prep_in = [p, prev] + [lp[k] for k in names]
    vecs = pl.pallas_call(
        _rwkv_sample_prep_kernel,
        grid=(1,),
        in_specs=[full(a) for a in prep_in],
        out_specs=[pl.BlockSpec((n, RWKV_WIDTH), lambda i: (0, 0))] * 4
        + [pl.BlockSpec((RWKV_WIDTH, n), lambda i: (0, 0))] * 6,
        out_shape=[jax.ShapeDtypeStruct((n, RWKV_WIDTH), F32)] * 4
        + [jax.ShapeDtypeStruct((RWKV_WIDTH, n), F32)] * 6,
        compiler_params=_params(("arbitrary",), VMEM_SMALL),
        name="rwkv_sample_prep",
    )(*prep_in)
    vec_t = pl.BlockSpec((HEAD, n), lambda h: (h, 0))
    st_spec = pl.BlockSpec((1, HEAD, HEAD, n), lambda h: (h, 0, 0, 0))
    wkv_new, y_t = pl.pallas_call(
        functools.partial(_rwkv_sample_step_kernel, unroll=unroll),
        grid=(N_HEADS,),
        in_specs=[st_spec] + [vec_t] * 6,
        out_specs=[st_spec, vec_t],
        out_shape=[jax.ShapeDtypeStruct(wkv_t.shape, F32), jax.ShapeDtypeStruct((RWKV_WIDTH, n), F32)],
        compiler_params=_params(("parallel",), VMEM_SMALL),
        name="rwkv_sample_step",
    )(wkv_t, *vecs[4:])
    return vecs[:4], y_t, wkv_new


def _rope_table_kernel(invf_ref, cos_ref, sin_ref, *, rows):
    pos = (pl.program_id(0) * rows + _iota((rows, 1), 0)).astype(F32)
    cos_ref[...], sin_ref[...] = _rope_tables(pos, invf_ref[...])


def _rope_table(t, inv_freq, rows):
    spec = pl.BlockSpec((rows, LANES), lambda i: (i, 0))
    return pl.pallas_call(
        functools.partial(_rope_table_kernel, rows=rows),
        grid=(t // rows,),
        in_specs=[pl.BlockSpec((1, LANES), lambda i: (0, 0))],
        out_specs=[spec, spec],
        out_shape=[jax.ShapeDtypeStruct((t, LANES), F32)] * 2,
        compiler_params=_params(("parallel",), VMEM_SMALL),
        name="rope_table",
    )(inv_freq)


def _swa_prompt_kernel(p_ref, cos_ref, sin_ref, sink_ref, y_ref, ko_ref, vo_ref, kprev_ref, vprev_ref, *, nb):
    n = pl.program_id(1)
    blk = WINDOW

    @pl.when(n == 0)
    def _init():
        kprev_ref[...] = jnp.zeros_like(kprev_ref)
        vprev_ref[...] = jnp.zeros_like(vprev_ref)

    p = p_ref[0]
    cos, sin_signed = cos_ref[...], sin_ref[...]
    q = _rope(p[:, :RWKV_WIDTH], cos, sin_signed, on_mxu=True) * (HEAD ** -0.5)
    k = _rope(p[:, RWKV_WIDTH:RWKV_WIDTH + KV_WIDTH], cos, sin_signed)
    v = p[:, RWKV_WIDTH + KV_WIDTH:]
    kcat = jnp.concatenate([kprev_ref[...], k], axis=0)
    vcat = jnp.concatenate([vprev_ref[...], v], axis=0)
    own = _iota((4 * blk, blk), 1) <= (_iota((4 * blk, blk), 0) & (blk - 1))
    no_prev = jnp.where(n == 0, NEG, 0.0)
    ones = jnp.ones((blk, blk), BF16)
    zero = jnp.zeros((), BF16)
    rows = lambda j, m: slice(j * blk, (j + m) * blk)
    heads = lambda g: range(4 * g, 4 * g + 4)
    cols = lambda g: slice(g * HEAD, (g + 1) * HEAD)
    sink_col = [jnp.concatenate([jnp.broadcast_to(sink_ref[0:1, h:h + 1], (blk, 1)) for h in heads(g)], axis=0)
                for g in range(2)]
    units = [(j, g) for j in range(nb) for g in range(2)]
    qg = {(j, g): _bf(jnp.concatenate([q[rows(j, 1), h * HEAD:(h + 1) * HEAD] for h in heads(g)], axis=0))
          for j, g in units}
    s2 = {(j, g): _dot_nt(qg[(j, g)], _bf(kcat[rows(j, 2), cols(g)])) for j, g in units}
    s = {(j, g): jnp.where(own, s2[(j, g)][:, blk:],
                           s2[(j, g)][:, :blk] + no_prev if j == 0 else s2[(j, g)][:, :blk]) for j, g in units}
    m = {u: jnp.maximum(jnp.max(s[u], axis=-1, keepdims=True), sink_col[u[1]]) for u in units}
    e = {u: _bf(jnp.exp(s[u] - m[u])) for u in units}
    inv = {u: 1.0 / (_dot(e[u], ones)[:, :HEAD] + jnp.exp(sink_col[u[1]] - m[u])) for u in units}
    e2 = {u: jnp.concatenate([jnp.where(own, zero, e[u]), jnp.where(own, e[u], zero)], axis=1) for u in units}
    o = {(j, g): _dot(e2[(j, g)], _bf(vcat[rows(j, 2), cols(g)])) * inv[(j, g)] for j, g in units}
    y_ref[0] = _bf(jnp.concatenate(
        [jnp.concatenate([o[(j, g)][i * blk:(i + 1) * blk] for g in range(2) for i in range(4)], axis=1)
         for j in range(nb)], axis=0))
    k_last, v_last = k[(nb - 1) * blk:], v[(nb - 1) * blk:]
    ko_ref[0] = k_last
    vo_ref[0] = v_last
    kprev_ref[...] = k_last
    vprev_ref[...] = v_last


def _swa_prompt(p, inv_freq, sink, nb=SWA_BLOCKS):
    bsz, t, _ = p.shape
    blk = WINDOW
    rows = nb * blk
    cos, sin_signed = _rope_table(t, inv_freq, rows)
    table = pl.BlockSpec((rows, LANES), lambda b, n: (n, 0))
    return pl.pallas_call(
        functools.partial(_swa_prompt_kernel, nb=nb),
        grid=(bsz, t // rows),
        in_specs=[pl.BlockSpec((1, rows, SWA_COLS), lambda b, n: (b, n, 0)), table, table,
                  pl.BlockSpec((1, N_HEADS), lambda b, n: (0, 0))],
        out_specs=[pl.BlockSpec((1, rows, RWKV_WIDTH), lambda b, n: (b, n, 0)),
                   pl.BlockSpec((1, blk, KV_WIDTH), lambda b, n: (b, 0, 0)),
                   pl.BlockSpec((1, blk, KV_WIDTH), lambda b, n: (b, 0, 0))],
        out_shape=[jax.ShapeDtypeStruct((bsz, t, RWKV_WIDTH), BF16),
                   jax.ShapeDtypeStruct((bsz, blk, KV_WIDTH), F32),
                   jax.ShapeDtypeStruct((bsz, blk, KV_WIDTH), F32)],
        scratch_shapes=[pltpu.VMEM((blk, KV_WIDTH), F32), pltpu.VMEM((blk, KV_WIDTH), F32)],
        compiler_params=_params(("arbitrary", "arbitrary"), VMEM_MID),
        name="swa_prompt",
    )(p, cos, sin_signed, sink)


def _swa_sample_kernel(p_ref, ck_ref, cv_ref, invf_ref, sinkc_ref, y_ref, nk_ref, nv_ref, *, bt):
    p = p_ref[...]
    cos, sin_signed = _rope_tables(jnp.full((1, 1), float(PAST_LEN), F32), invf_ref[...])
    q = _rope(p[:, :RWKV_WIDTH], cos, sin_signed) * (HEAD ** -0.5)
    k = _rope(p[:, RWKV_WIDTH:RWKV_WIDTH + KV_WIDTH], cos, sin_signed)
    v = p[:, RWKV_WIDTH + KV_WIDTH:]
    k_t, v_t = k.T, v.T
    last = _iota((KV_WIDTH, WINDOW), 1) == WINDOW - 1
    head_mask = _iota((N_HEADS, RWKV_WIDTH), 0) == (_iota((N_HEADS, RWKV_WIDTH), 1) >> _log2(HEAD))
    tile_lanes = lambda x: jnp.concatenate([x[:, :HEAD]] * 4 + [x[:, HEAD:]] * 4, axis=1)
    tile_rows = lambda x: jnp.concatenate([x[:HEAD]] * 4 + [x[HEAD:]] * 4, axis=0)
    seqs = range(bt)
    k_new = [jnp.where(last, k_t[:, bi:bi + 1], pltpu.roll(ck_ref[bi], WINDOW - 1, axis=1)) for bi in seqs]
    v_new = [jnp.where(last, v_t[:, bi:bi + 1], pltpu.roll(cv_ref[bi], WINDOW - 1, axis=1)) for bi in seqs]
    for bi in seqs:
        nk_ref[bi] = k_new[bi]
        nv_ref[bi] = v_new[bi]
    q_bd = [jnp.where(head_mask, q[bi:bi + 1, :], 0.0) for bi in seqs]
    s = [_dot3(q_bd[bi], tile_rows(k_new[bi])) for bi in seqs]
    pr = [_softmax_sink(s[bi], sinkc_ref[...]) for bi in seqs]
    o = [_dot3_nt(pr[bi], v_new[bi]) for bi in seqs]
    y_ref[...] = jnp.concatenate(
        [jnp.sum(jnp.where(head_mask, tile_lanes(o[bi]), 0.0), axis=0, keepdims=True) for bi in seqs], axis=0)


def _swa_sample(p, ck, cv, inv_freq, sink_col, bt=8):
    n = p.shape[0]
    cache_spec = pl.BlockSpec((bt, KV_WIDTH, WINDOW), lambda i: (i, 0, 0))
    return pl.pallas_call(
        functools.partial(_swa_sample_kernel, bt=bt),
        grid=(n // bt,),
        in_specs=[pl.BlockSpec((bt, SWA_COLS), lambda i: (i, 0)), cache_spec, cache_spec,
                  pl.BlockSpec((1, LANES), lambda i: (0, 0)), pl.BlockSpec((N_HEADS, 1), lambda i: (0, 0))],
        out_specs=[pl.BlockSpec((bt, RWKV_WIDTH), lambda i: (i, 0)), cache_spec, cache_spec],
        out_shape=[jax.ShapeDtypeStruct((n, RWKV_WIDTH), F32), jax.ShapeDtypeStruct(ck.shape, F32),
                   jax.ShapeDtypeStruct(cv.shape, F32)],
        compiler_params=_params(("parallel",), VMEM_SMALL),
        name="swa_sample",
    )(p, ck, cv, inv_freq, sink_col)


def _mix_out(yr, ys, x, wo1, wo2):
    return x + _dot(yr, wo1) + _dot(ys, wo2)


def _out_cross_kernel(yr_ref, ys_ref, x_ref, wo1_ref, wo2_ref, gc_ref, wq_ref, mk_ref, mv_ref, wco_ref,
                      qs_ref, mks_ref, mvs_ref, gf_ref, wr_ref, br_ref, tri_ref, o_ref, os_ref, h_ref, gate_ref,
                      pos_ref, *, bt):
    _cross_sample(qs_ref, mks_ref, mvs_ref, os_ref, bt)
    x1 = _mix_out(yr_ref[0], ys_ref[0], x_ref[0], wo1_ref[...], wo2_ref[...])
    q = _bf(_dot(_bf(_rmsnorm(x1, gc_ref[...])), wq_ref[...]))
    mk, mv = mk_ref[0], mv_ref[0]
    heads = range(X_HEADS)
    sl = lambda h: slice(h * X_HEAD, (h + 1) * X_HEAD)
    s = [_dot_nt(q[:, sl(h)], mk[:, sl(h)]) * (X_HEAD ** -0.5) for h in heads]
    e = [jnp.exp(s[h] - jnp.max(s[h], axis=-1, keepdims=True)) for h in heads]
    outs = [_dot(_bf(e[h]), mv[:, sl(h)]) / jnp.sum(e[h], axis=-1, keepdims=True) for h in heads]
    x2 = x1 + _dot(_bf(jnp.concatenate(outs, axis=1)), wco_ref[...])
    o_ref[0] = x2
    h_ref[0], gate_ref[0], pos_ref[0] = _moe_route(x2, gf_ref[...], wr_ref[...], br_ref[...], tri_ref[...], False)


def _out_cross(yr, ys, x, wo1, wo2, gain, wq, mk, mv, wco, q_s, mk_s, mv_s, gain_ffn, w_router, b_router,
               tm=PROJ_ROWS):
    bsz, t, d = x.shape
    n = q_s.shape[0]
    nt = t // tm
    bt = n // (bsz * nt)
    assert bt * bsz * nt == n
    tile = lambda w: pl.BlockSpec((1, tm, w), lambda b, i: (b, i, 0))
    const = lambda a: pl.BlockSpec(a.shape, lambda b, i: (0,) * a.ndim, pipeline_mode=pl.Buffered(1))
    mem = pl.BlockSpec((1, MEM_TOKENS, X_WIDTH), lambda b, i: (b, 0, 0))
    mem_shape = (n, MEM_TOKENS // 2, 2 * X_HEADS, X_HEAD)
    mem_s = pl.BlockSpec((bt,) + mem_shape[1:], lambda b, i: (b * nt + i, 0, 0, 0))
    vec_s = pl.BlockSpec((bt, X_HEADS, X_HEAD), lambda b, i: (b * nt + i, 0, 0))
    earlier = jnp.tril(jnp.ones((tm, tm), BF16), -1)
    x2, o_s, h, gate, pos = pl.pallas_call(
        functools.partial(_out_cross_kernel, bt=bt),
        grid=(bsz, nt),
        in_specs=[tile(RWKV_WIDTH), tile(RWKV_WIDTH), tile(d), const(wo1), const(wo2), const(gain), const(wq),
                  mem, mem, const(wco), vec_s, mem_s, mem_s, const(gain_ffn), const(w_router), const(b_router),
                  const(earlier)],
        out_specs=[tile(d), vec_s, tile(d), tile(LANES), tile(LANES)],
        out_shape=[jax.ShapeDtypeStruct((bsz, t, d), F32), jax.ShapeDtypeStruct((n, X_HEADS, X_HEAD), F32),
                   jax.ShapeDtypeStruct((bsz, t, d), BF16), jax.ShapeDtypeStruct((bsz, t, LANES), F32),
                   jax.ShapeDtypeStruct((bsz, t, LANES), F32)],
        compiler_params=_params(("parallel", "parallel"), VMEM_LARGE),
        name="out_cross",
    )(yr, ys, x, wo1, wo2, gain, wq, mk, mv, wco, q_s.reshape(n, X_HEADS, X_HEAD), mk_s.reshape(mem_shape),
      mv_s.reshape(mem_shape), gain_ffn, w_router, b_router, earlier)
    return x2, o_s.reshape(n, X_WIDTH), (h, gate, pos)


def _out_q_kernel(yt_ref, r_ref, k_ref, v_ref, g_ref, rk_ref, lnw_ref, lnb_ref, ys_ref, x_ref, wo1_ref, wo2_ref,
                  gc_ref, wq_ref, x1_ref, q_ref):
    yr = _rwkv_post(yt_ref[...].T, r_ref[...], k_ref[...], v_ref[...], g_ref[...], rk_ref[...], lnw_ref[...],
                    lnb_ref[...], True)
    x1 = x_ref[...] + _dot3(yr, wo1_ref[...]) + _dot3(ys_ref[...], wo2_ref[...])
    x1_ref[...] = x1
    q_ref[...] = _dot3(_rmsnorm(x1, gc_ref[...]), wq_ref[...])


def _cross_sample(q_ref, mk_ref, mv_ref, o_ref, bt):
    seqs = range(bt)
    both = lambda x: x + pltpu.roll(x, X_HEADS, axis=0)
    q8 = [jnp.concatenate([q_ref[bi]] * 2, axis=0) for bi in seqs]
    s = [jnp.sum(mk_ref[bi] * q8[bi][None], axis=-1, keepdims=True) * (X_HEAD ** -0.5) for bi in seqs]
    mx = [jnp.max(s[bi], axis=0) for bi in seqs]
    mx = [jnp.maximum(m, pltpu.roll(m, X_HEADS, axis=0)) for m in mx]
    e = [jnp.exp(s[bi] - mx[bi][None]) for bi in seqs]
    den = [both(jnp.sum(e[bi], axis=0)) for bi in seqs]
    o8 = [jnp.sum(e[bi] * mv_ref[bi], axis=0) / den[bi] for bi in seqs]
    for bi in seqs:
        o_ref[bi] = both(o8[bi])[:X_HEADS]


def _lin_res_kernel(x_ref, a_ref, w_ref, o_ref):
    o_ref[...] = x_ref[...] + _dot3(a_ref[...], w_ref[...])


def _single_step(kernel_fn, out_shape, name, *args):
    full = lambda a: pl.BlockSpec(a.shape, lambda i: (0,) * len(a.shape))
    outs = out_shape if isinstance(out_shape, (list, tuple)) else [out_shape]
    out_specs = [full(o) for o in outs]
    return pl.pallas_call(
        kernel_fn, grid=(1,), in_specs=[full(a) for a in args],
        out_specs=out_specs if isinstance(out_shape, (list, tuple)) else out_specs[0],
        out_shape=out_shape, compiler_params=_params(("arbitrary",), VMEM_SMALL), name=name,
    )(*args)


def _out_q_sample(rwkv_parts, ys, x, wo1, wo2, gain, wq):
    n, d = x.shape
    return _single_step(_out_q_kernel, [jax.ShapeDtypeStruct((n, d), F32), jax.ShapeDtypeStruct((n, X_WIDTH), F32)],
                        "out_q_sample", *rwkv_parts, ys, x, wo1, wo2, gain, wq)


def _cross_out_sample(x1, o, wco):
    return _single_step(_lin_res_kernel, jax.ShapeDtypeStruct(x1.shape, F32), "cross_out_sample", x1, o, wco)


def _route(logits):
    lane_i = _iota(logits.shape, 1)
    lane = lane_i.astype(F32)
    big = 1024.0
    n_exp = float(N_GROUPS * EXPERTS_PER_GROUP)
    lg = jnp.where((lane >= n_exp) & (lane < n_exp + N_GROUPS), logits, NEG)
    g_max = jnp.max(lg, axis=-1, keepdims=True)
    g_idx = jnp.min(jnp.where(lg == g_max, lane, big), axis=-1, keepdims=True) - n_exp
    p_sel = 1.0 / jnp.sum(jnp.exp(lg - g_max), axis=-1, keepdims=True)
    le = jnp.where((lane_i >> _log2(EXPERTS_PER_GROUP)).astype(F32) == g_idx, logits, NEG)
    t1 = jnp.max(le, axis=-1, keepdims=True)
    i1 = jnp.min(jnp.where(le == t1, lane, big), axis=-1, keepdims=True)
    le2 = jnp.where(lane == i1, NEG, le)
    t2 = jnp.max(le2, axis=-1, keepdims=True)
    i2 = jnp.min(jnp.where(le2 == t2, lane, big), axis=-1, keepdims=True)
    e2 = jnp.exp(t2 - t1)
    w1 = 1.0 / (1.0 + e2)
    return p_sel * (jnp.where(lane == i1, w1, 0.0) + jnp.where(lane == i2, e2 * w1, 0.0)), g_idx


def _moe_route(x, gain, w_router, b_router, tri, precise):
    h = _rmsnorm(x, gain)
    gate, g_idx = _route((_dot3 if precise else _dot1)(h, w_router) + b_router)
    own = _iota(gate.shape, 1).astype(F32) == g_idx
    return _bf(h), gate, jnp.where(own, _dot(tri, _mask_bf(own)), -1.0)


def _moe_group_step(g, h_ref, gate_ref, pos_ref, wg_ref, wu_ref, wd_ref, acc_ref, tm, blocks, precise):
    lane = _iota((tm, LANES), 1)
    pos = jnp.sum(jnp.where(lane == g, pos_ref[...], 0.0), axis=-1, keepdims=True)
    n_g = jnp.sum(jnp.where(pos >= 0.0, 1.0, 0.0))
    width = EXPERTS_PER_GROUP * EXPERT_HIDDEN
    expand = _mask_bf(_iota((LANES, width), 0)
                      == g * EXPERTS_PER_GROUP + (_iota((LANES, width), 1) >> _log2(EXPERT_HIDDEN)))
    start = 0
    for bm in blocks:
        @pl.when(n_g > start)
        def _block(start=start, bm=bm):
            sel = _mask_bf(pos - float(start) == _iota((tm, bm), 1).astype(F32))
            hs = _bf(_dot_tn(sel, h_ref[...]))
            if precise:
                gh, gl = _split(gate_ref[...])
                gs = _dot_tn(sel, gh) + _dot_tn(sel, gl)
            else:
                gs = _dot_tn(sel, _bf(gate_ref[...]))
            hg = _dot(hs, wg_ref[0])
            hu = _dot(hs, wu_ref[0])
            act = hg * _sigmoid(hg) * hu * (_dot_sel_rhs if precise else _dot1)(gs, expand)
            acc_ref[...] += _dot(sel, _bf(_dot(_bf(act), wd_ref[0])))

        start += bm


def _moe_kernel(x_ref, gf_ref, wr_ref, br_ref, wg_ref, wu_ref, wd_ref, gfin_ref, tri_ref, o_ref, h_ref, gate_ref,
                pos_ref, acc_ref, *, tm, blocks, precise):
    g = pl.program_id(1)

    @pl.when(g == 0)
    def _init():
        h_ref[...], gate_ref[...], pos_ref[...] = _moe_route(x_ref[...], gf_ref[...], wr_ref[...], br_ref[...],
                                                             tri_ref[...], precise)
        acc_ref[...] = jnp.zeros_like(acc_ref)

    _moe_group_step(g, h_ref, gate_ref, pos_ref, wg_ref, wu_ref, wd_ref, acc_ref, tm, blocks, precise)

    @pl.when(g == N_GROUPS - 1)
    def _fin():
        o_ref[...] = _rmsnorm(x_ref[...] + acc_ref[...], gfin_ref[...])


def _moe_routed_kernel(x_ref, h_ref, gate_ref, pos_ref, wg_ref, wu_ref, wd_ref, gfin_ref, o_ref, acc_ref, *, tm,
                       blocks):
    g = pl.program_id(1)

    @pl.when(g == 0)
    def _init():
        acc_ref[...] = jnp.zeros_like(acc_ref)

    _moe_group_step(g, h_ref, gate_ref, pos_ref, wg_ref, wu_ref, wd_ref, acc_ref, tm, blocks, False)

    @pl.when(g == N_GROUPS - 1)
    def _fin():
        o_ref[...] = _rmsnorm(x_ref[...] + acc_ref[...], gfin_ref[...])


def _moe_routed(x, h, gate, pos, wg, wu, wd, gain_final, tm, blocks):
    n, d = x.shape
    assert sum(blocks) == tm
    width = EXPERTS_PER_GROUP * EXPERT_HIDDEN
    rows = lambda w: pl.BlockSpec((tm, w), lambda i, g: (i, 0))
    weight = lambda a, b: pl.BlockSpec((1, a, b), lambda i, g: (g, 0, 0))
    return pl.pallas_call(
        functools.partial(_moe_routed_kernel, tm=tm, blocks=blocks),
        grid=(n // tm, N_GROUPS),
        in_specs=[rows(d), rows(d), rows(LANES), rows(LANES), weight(d, width), weight(d, width), weight(width, d),
                  pl.BlockSpec(gain_final.shape, lambda i, g: (0, 0))],
        out_specs=rows(d),
        out_shape=jax.ShapeDtypeStruct((n, d), F32),
        scratch_shapes=[pltpu.VMEM((tm, d), F32)],
        compiler_params=_params(("parallel", "arbitrary"), VMEM_LARGE),
        name="moe_routed",
    )(x, h, gate, pos, wg, wu, wd, gain_final)


def _moe(x, gain, w_router, b_router, wg, wu, wd, gain_final, tm, blocks, precise):
    n, d = x.shape
    assert sum(blocks) == tm
    width = EXPERTS_PER_GROUP * EXPERT_HIDDEN
    const = lambda a: pl.BlockSpec(a.shape, lambda i, g: (0,) * a.ndim)
    earlier = jnp.tril(jnp.ones((tm, tm), BF16), -1)
    return pl.pallas_call(
        functools.partial(_moe_kernel, tm=tm, blocks=blocks, precise=precise),
        grid=(n // tm, N_GROUPS),
        in_specs=[pl.BlockSpec((tm, d), lambda i, g: (i, 0)), const(gain), const(w_router), const(b_router),
                  pl.BlockSpec((1, d, width), lambda i, g: (g, 0, 0)),
                  pl.BlockSpec((1, d, width), lambda i, g: (g, 0, 0)),
                  pl.BlockSpec((1, width, d), lambda i, g: (g, 0, 0)), const(gain_final), const(earlier)],
        out_specs=pl.BlockSpec((tm, d), lambda i, g: (i, 0)),
        out_shape=jax.ShapeDtypeStruct((n, d), F32),
        scratch_shapes=[pltpu.VMEM((tm, d), BF16), pltpu.VMEM((tm, LANES), F32), pltpu.VMEM((tm, LANES), F32),
                        pltpu.VMEM((tm, d), F32)],
        compiler_params=_params(("parallel", "arbitrary"), VMEM_LARGE),
        name="moe",
    )(x, gain, w_router, b_router, wg, wu, wd, gain_final, earlier)


def kernel(x_prompt, x_sample, mem_prompt, state_rwkv_shift, state_rwkv_wkv, cache_swa_k, cache_swa_v, cache_mem_k, cache_mem_v, norm_mix, w_in, mu_shift, w_decay0, w_decay_up, a0, w_a_up, w_g_up, k_k, k_a, r_k, ln_x_w, ln_x_b, attn_sink, w_out, norm_cross, norm_mem, w_cq, w_ck, w_cv, w_co, norm_ffn, w_group_router, b_group_router, w_expert_router, b_expert_router, w_exp_gate, w_exp_up, w_exp_down, norm_final):
    bsz, t, d = x_prompt.shape
    n_s = x_sample.shape[0]
    wb = cache_swa_k.shape[2]
    width = EXPERTS_PER_GROUP * EXPERT_HIDDEN

    w_in_bf = w_in[0].astype(BF16)
    wo1, wo2 = w_out[0, :RWKV_WIDTH].astype(BF16), w_out[0, RWKV_WIDTH:].astype(BF16)
    wq, wco = w_cq[0].astype(BF16), w_co[0].astype(BF16)
    w_ckv = jnp.concatenate([w_ck[0], w_cv[0]], axis=1).astype(BF16)
    pad = LANES - N_GROUPS * (EXPERTS_PER_GROUP + 1)
    w_router = jnp.pad(jnp.concatenate([w_expert_router[0], w_group_router[0]], axis=1), ((0, 0), (0, pad)))
    b_router = jnp.pad(jnp.concatenate([b_expert_router[0], b_group_router[0]]), (0, pad)).reshape(1, LANES)
    wg = jnp.transpose(w_exp_gate[0], (0, 2, 1, 3)).reshape(N_GROUPS, d, width).astype(BF16)
    wu = jnp.transpose(w_exp_up[0], (0, 2, 1, 3)).reshape(N_GROUPS, d, width).astype(BF16)
    wd = w_exp_down[0].reshape(N_GROUPS, width, d).astype(BF16)
    row = lambda a: a.reshape(1, -1)
    lp = dict(mu_shift=mu_shift, w_decay0=w_decay0, w_decay_up=w_decay_up[0], a0=a0, w_a_up=w_a_up[0],
              w_g_up=w_g_up[0], k_k=k_k, k_a=k_a, r_k=row(r_k), ln_x_w=ln_x_w, ln_x_b=ln_x_b)
    half = HEAD // 2
    inv_freq = ROPE_THETA ** (-jnp.arange(half, dtype=F32) * 2.0 / HEAD)
    inv_freq = jnp.tile(inv_freq, LANES // half).reshape(1, LANES)

    xs = x_sample.reshape(n_s, d)
    ps_rwkv, ps_swa = _norm_matmul(xs, norm_mix, w_in[0], RWKV_COLS, n_s)
    (r_s, k2_s, v_s_tok, g_s), yt_s, wkv_s = _rwkv_sample(ps_rwkv, state_rwkv_shift[0],
                                                           jnp.transpose(state_rwkv_wkv[0], (1, 2, 3, 0)), lp)
    cache_t = lambda c: jnp.transpose(c[0].reshape(n_s, wb, KV_WIDTH), (0, 2, 1))
    ys_swa, k_s, v_s = _swa_sample(ps_swa, cache_t(cache_swa_k), cache_t(cache_swa_v), inv_freq,
                                   attn_sink.reshape(N_HEADS, 1))
    x1s, q_s = _out_q_sample((yt_s, r_s, k2_s, v_s_tok, g_s, lp["r_k"], lp["ln_x_w"], lp["ln_x_b"]), ys_swa, xs,
                             w_out[0, :RWKV_WIDTH], w_out[0, RWKV_WIDTH:], norm_cross, w_cq[0])

    xm_rwkv, shift_p, p_swa = _in_proj_shift(x_prompt.reshape(bsz * t, d), norm_mix, w_in_bf, mu_shift, RWKV_COLS,
                                             PROJ_ROWS, t)
    y_rwkv, wkv_p = _rwkv_prompt(xm_rwkv.reshape(bsz, t, RWKV_COLS), lp)
    y_swa, k_p, v_p = _swa_prompt(p_swa.reshape(bsz, t, SWA_COLS), inv_freq, attn_sink)
    mk_p, mv_p = _norm_matmul(mem_prompt.reshape(bsz * MEM_TOKENS, d), norm_mem, w_ckv, X_WIDTH, MEMKV_ROWS)
    mk_b = mk_p.astype(BF16).reshape(bsz, MEM_TOKENS, X_WIDTH)
    mv_b = mv_p.astype(BF16).reshape(bsz, MEM_TOKENS, X_WIDTH)
    x2, o_s, routing = _out_cross(y_rwkv, y_swa, x_prompt, wo1, wo2, norm_cross, wq, mk_b, mv_b, wco, q_s,
                                  cache_mem_k[0], cache_mem_v[0], norm_ffn, w_router, b_router, tm=MOE_ROWS)
    flat = lambda a: a.reshape(bsz * t, a.shape[-1])
    y_prompt = _moe_routed(flat(x2), *(flat(a) for a in routing), wg, wu, wd, row(norm_final), MOE_ROWS, MOE_BLOCKS)
    y_sample = _moe(_cross_out_sample(x1s, o_s, w_co[0]), norm_ffn, w_router, b_router, wg, wu, wd,
                    row(norm_final), n_s, (n_s,), True)

    kv_shape = (1, -1, wb, KV_WIDTH // HEAD, HEAD)
    mem_shape = (1, bsz, MEM_TOKENS, X_HEADS, X_HEAD)
    cache_back = lambda c: jnp.transpose(c, (0, 2, 1)).reshape(kv_shape)
    return (y_prompt.reshape(bsz, t, d), y_sample.reshape(n_s, 1, d),
            shift_p.reshape(1, bsz, RWKV_COLS), wkv_p[None], k_p.reshape(kv_shape), v_p.reshape(kv_shape),
            mk_p.reshape(mem_shape), mv_p.reshape(mem_shape),
            ps_rwkv[None], jnp.transpose(wkv_s, (3, 0, 1, 2))[None], cache_back(k_s), cache_back(v_s))
```
